```python
import math
import jax, jax.numpy as jnp
from jax import lax
import numpy as np

D_MODEL = 2048
BATCH = 1
SEQ = 16384
DEPTH = 2

HEAD_DIM = 128
N_FOX_HEADS = 8
N_DIL_HEADS = 8
DIL_PAIRS = ((128, 1), (512, 4), (2048, 16))
Q_BLOCK = 128
D_FF = 5632
S5_GROUP = 16
S5_WIDTH = 1024
S5_GROUPS = S5_WIDTH // S5_GROUP
S5_STATE = 64
N_EVEN = (DEPTH + 1) // 2
N_ODD = DEPTH // 2
ALPHA = (2.0 * DEPTH) ** 0.25
BETA = (8.0 * DEPTH) ** -0.25
LN_EPS = 1e-5
FOX_W = N_FOX_HEADS * HEAD_DIM
DIL_W = N_DIL_HEADS * HEAD_DIM
ATTN_IN = 3 * FOX_W + N_FOX_HEADS + 3 * DIL_W
ATTN_OUT = FOX_W + DIL_W

kernel_name = "hybrid_fox_dilated_s5_macaron_deepnorm"


def layer_norm(x, g, b):
    xf = x.astype(jnp.float32)
    mu = xf.mean(-1, keepdims=True)
    var = jnp.square(xf - mu).mean(-1, keepdims=True)
    return ((xf - mu) * lax.rsqrt(var + LN_EPS) * g + b).astype(x.dtype)


def swiglu(x, w_gate, w_up, w_down):
    return (jax.nn.silu(x @ w_gate) * (x @ w_up)) @ w_down


def forgetting_attention(q, k, v, log_f):
    B, S, H, Dh = q.shape
    nb = S // Q_BLOCK
    c = jnp.cumsum(log_f.astype(jnp.float32), axis=1).transpose(0, 2, 1)
    kpos = jnp.arange(S)
    scale = Dh ** -0.5

    def block(i):
        start = i * Q_BLOCK
        qb = lax.dynamic_slice_in_dim(q, start, Q_BLOCK, axis=1)
        cb = lax.dynamic_slice_in_dim(c, start, Q_BLOCK, axis=2)
        s = jnp.einsum('bqhd,bkhd->bhqk', qb, k).astype(jnp.float32) * scale
        s = s + cb[..., :, None] - c[..., None, :]
        qpos = start + jnp.arange(Q_BLOCK)
        causal = kpos[None, :] <= qpos[:, None]
        p = jax.nn.softmax(jnp.where(causal, s, -jnp.inf), axis=-1)
        return jnp.einsum('bhqk,bkhd->bqhd', p.astype(v.dtype), v)

    out = lax.map(block, jnp.arange(nb))
    return out.transpose(1, 0, 2, 3, 4).reshape(B, S, H, Dh)


def dilated_branch(q, k, v, window, dilation):
    B, S, H, Dh = q.shape
    span = window // dilation
    L = S // dilation
    blk = span
    nb = -(-L // blk)
    pad = nb * blk - L

    def to_blocks(t):
        t = t.reshape(B, L, dilation, H, Dh).transpose(0, 2, 1, 3, 4).reshape(B * dilation, L, H, Dh)
        t = jnp.pad(t, ((0, 0), (0, pad), (0, 0), (0, 0)))
        return t.reshape(B * dilation, nb, blk, H, Dh)

    qb, kb, vb = to_blocks(q), to_blocks(k), to_blocks(v)
    prev = lambda t: jnp.concatenate([jnp.zeros_like(t[:, :1]), t[:, :-1]], axis=1)
    kc = jnp.concatenate([prev(kb), kb], axis=2)
    vc = jnp.concatenate([prev(vb), vb], axis=2)
    j = jnp.arange(blk)[:, None]
    m = jnp.arange(2 * blk)[None, :]
    dist = blk + j - m
    band = (dist >= 0) & (dist <= span)
    has_prev = (jnp.arange(nb) > 0)[:, None, None] | (m >= blk)[None]
    mask = band[None] & has_prev
    s = jnp.einsum('gnqhd,gnkhd->gnhqk', qb, kc).astype(jnp.float32) * (Dh ** -0.5)
    s = jnp.where(mask[None, :, None], s, -jnp.inf)
    lse = jax.nn.logsumexp(s, axis=-1)
    p = jnp.exp(s - lse[..., None])
    o = jnp.einsum('gnhqk,gnkhd->gnqhd', p.astype(vc.dtype), vc)
    o = o.reshape(B * dilation, nb * blk, H, Dh)[:, :L]
    o = o.reshape(B, dilation, L, H, Dh).transpose(0, 2, 1, 3, 4).reshape(B, S, H, Dh)
    lse = lse.transpose(0, 1, 3, 2).reshape(B * dilation, nb * blk, H)[:, :L]
    lse = lse.reshape(B, dilation, L, H).transpose(0, 2, 1, 3).reshape(B, S, H)
    return o, lse


def dilated_attention(q, k, v):
    branches = [dilated_branch(q, k, v, w, d) for (w, d) in DIL_PAIRS]
    outs = jnp.stack([o for o, _ in branches], axis=0)
    lses = jnp.stack([l for _, l in branches], axis=0)
    wts = jax.nn.softmax(lses, axis=0)
    return jnp.einsum('gbsh,gbshd->bshd', wts.astype(outs.dtype), outs)


def attention_mixer(x, w_in, b_f, w_out):
    B, S, _ = x.shape
    proj = x @ w_in
    fox_qkv, f_logit, dil_qkv = jnp.split(proj, [3 * FOX_W, 3 * FOX_W + N_FOX_HEADS], axis=-1)
    qa, ka, va = [t.reshape(B, S, N_FOX_HEADS, HEAD_DIM) for t in jnp.split(fox_qkv, 3, axis=-1)]
    log_f = jax.nn.log_sigmoid((f_logit + b_f).astype(jnp.float32))
    ya = forgetting_attention(qa, ka, va, log_f)
    qb, kb, vb = [t.reshape(B, S, N_DIL_HEADS, HEAD_DIM) for t in jnp.split(dil_qkv, 3, axis=-1)]
    yb = dilated_attention(qb, kb, vb)
    y = jnp.concatenate([ya.reshape(B, S, FOX_W), yb.reshape(B, S, DIL_W)], axis=-1)
    return y @ w_out


def ssm_combine(e1, e2):
    a1r, a1i, b1r, b1i = e1
    a2r, a2i, b2r, b2i = e2
    return (a2r * a1r - a2i * a1i,
            a2r * a1i + a2i * a1r,
            a2r * b1r - a2i * b1i + b2r,
            a2r * b1i + a2i * b1r + b2i)


def s5_mixer(x, w_in, lam_re, lam_im, log_dt, b_re, b_im, c_re, c_im, d_skip, w_glu_out, w_glu_gate):
    B, S, _ = x.shape
    f32 = jnp.float32
    u = (x @ w_in).astype(f32).reshape(B, S, S5_GROUPS, S5_GROUP)
    lr, li = lam_re.astype(f32), lam_im.astype(f32)
    dt = jnp.exp(log_dt.astype(f32))[:, None]
    mag = jnp.exp(lr * dt)
    a_re, a_im = mag * jnp.cos(li * dt), mag * jnp.sin(li * dt)
    den = lr * lr + li * li
    coef_re = ((a_re - 1.0) * lr + a_im * li) / den
    coef_im = (a_im * lr - (a_re - 1.0) * li) / den
    br, bi = b_re.astype(f32), b_im.astype(f32)
    bb_re = coef_re[..., None] * br - coef_im[..., None] * bi
    bb_im = coef_re[..., None] * bi + coef_im[..., None] * br
    bu_re = jnp.einsum('bsgh,gph->bsgp', u, bb_re)
    bu_im = jnp.einsum('bsgh,gph->bsgp', u, bb_im)
    ar = jnp.broadcast_to(a_re, bu_re.shape)
    ai = jnp.broadcast_to(a_im, bu_re.shape)
    _, _, xr, xi = lax.associative_scan(ssm_combine, (ar, ai, bu_re, bu_im), axis=1)
    y = jnp.einsum('bsgp,ghp->bsgh', xr, c_re.astype(f32)) - jnp.einsum('bsgp,ghp->bsgh', xi, c_im.astype(f32))
    y = (y + d_skip.astype(f32).reshape(S5_GROUPS, S5_GROUP) * u).reshape(B, S, S5_WIDTH)
    z = jax.nn.gelu(y).astype(x.dtype)
    return (z @ w_glu_out) * jax.nn.sigmoid(z @ w_glu_gate)


def setup_inputs(seed: int = 0) -> dict:
    key = jax.random.key(seed)
    ks = jax.random.split(key, 24)
    nrm = lambda k, shape, scale: jax.random.normal(k, shape, jnp.float32) * scale
    n_idx = jnp.arange(S5_STATE, dtype=jnp.float32)
    return {
        "x": nrm(ks[0], (BATCH, SEQ, D_MODEL), 1.0),
        "ffn1_w_gate": nrm(ks[1], (DEPTH, D_MODEL, D_FF), D_MODEL ** -0.5),
        "ffn1_w_up": nrm(ks[2], (DEPTH, D_MODEL, D_FF), D_MODEL ** -0.5),
        "ffn1_w_down": nrm(ks[3], (DEPTH, D_FF, D_MODEL), BETA * D_FF ** -0.5),
        "ffn2_w_gate": nrm(ks[4], (DEPTH, D_MODEL, D_FF), D_MODEL ** -0.5),
        "ffn2_w_up": nrm(ks[5], (DEPTH, D_MODEL, D_FF), D_MODEL ** -0.5),
        "ffn2_w_down": nrm(ks[6], (DEPTH, D_FF, D_MODEL), BETA * D_FF ** -0.5),
        "ln_gain": 1.0 + nrm(ks[7], (DEPTH, 3, D_MODEL), 0.02),
        "ln_bias": nrm(ks[8], (DEPTH, 3, D_MODEL), 0.02),
        "attn_w_in": nrm(ks[9], (N_EVEN, D_MODEL, ATTN_IN), D_MODEL ** -0.5),
        "attn_b_f": jax.random.uniform(ks[10], (N_EVEN, N_FOX_HEADS), jnp.float32, 1.0, 5.0),
        "attn_w_out": nrm(ks[11], (N_EVEN, ATTN_OUT, D_MODEL), BETA * ATTN_OUT ** -0.5),
        "s5_w_in": nrm(ks[12], (N_ODD, D_MODEL, S5_WIDTH), D_MODEL ** -0.5),
        "s5_lambda_re": -0.5 + nrm(ks[13], (N_ODD, S5_GROUPS, S5_STATE), 0.01),
        "s5_lambda_im": math.pi * n_idx + nrm(ks[14], (N_ODD, S5_GROUPS, S5_STATE), 0.01),
        "s5_log_dt": jax.random.uniform(ks[15], (N_ODD, S5_GROUPS), jnp.float32, math.log(1e-3), math.log(1e-1)),
        "s5_b_re": nrm(ks[16], (N_ODD, S5_GROUPS, S5_STATE, S5_GROUP), (2 * S5_GROUP) ** -0.5),
        "s5_b_im": nrm(ks[17], (N_ODD, S5_GROUPS, S5_STATE, S5_GROUP), (2 * S5_GROUP) ** -0.5),
        "s5_c_re": nrm(ks[18], (N_ODD, S5_GROUPS, S5_GROUP, S5_STATE), S5_STATE ** -0.5),
        "s5_c_im": nrm(ks[19], (N_ODD, S5_GROUPS, S5_GROUP, S5_STATE), S5_STATE ** -0.5),
        "s5_d": nrm(ks[20], (N_ODD, S5_WIDTH), 1.0),
        "s5_w_glu_out": nrm(ks[21], (N_ODD, S5_WIDTH, D_MODEL), BETA * S5_WIDTH ** -0.5),
        "s5_w_glu_gate": nrm(ks[22], (N_ODD, S5_WIDTH, D_MODEL), S5_WIDTH ** -0.5),
    }


def reference(x, ffn1_w_gate, ffn1_w_up, ffn1_w_down, ffn2_w_gate, ffn2_w_up, ffn2_w_down,
              ln_gain, ln_bias, attn_w_in, attn_b_f, attn_w_out, s5_w_in, s5_lambda_re,
              s5_lambda_im, s5_log_dt, s5_b_re, s5_b_im, s5_c_re, s5_c_im, s5_d,
              s5_w_glu_out, s5_w_glu_gate):
    for i in range(DEPTH):
        x = layer_norm(ALPHA * x + 0.5 * swiglu(x, ffn1_w_gate[i], ffn1_w_up[i], ffn1_w_down[i]),
                       ln_gain[i, 0], ln_bias[i, 0])
        j = i // 2
        if i % 2 == 0:
            mix = attention_mixer(x, attn_w_in[j], attn_b_f[j], attn_w_out[j])
        else:
            mix = s5_mixer(x, s5_w_in[j], s5_lambda_re[j], s5_lambda_im[j], s5_log_dt[j],
                           s5_b_re[j], s5_b_im[j], s5_c_re[j], s5_c_im[j], s5_d[j],
                           s5_w_glu_out[j], s5_w_glu_gate[j])
        x = layer_norm(ALPHA * x + mix, ln_gain[i, 1], ln_bias[i, 1])
        x = layer_norm(ALPHA * x + 0.5 * swiglu(x, ffn2_w_gate[i], ffn2_w_up[i], ffn2_w_down[i]),
                       ln_gain[i, 2], ln_bias[i, 2])
    return x
```

```python
import functools
import math

import jax
import jax.numpy as jnp
from jax import lax
from jax.experimental import pallas as pl
from jax.experimental.pallas import tpu as pltpu

D_MODEL = 2048
DEPTH = 2
HEAD_DIM = 128
N_FOX_HEADS = 8
N_DIL_HEADS = 8
DIL_PAIRS = ((128, 1), (512, 4), (2048, 16))
DIL_BLOCK = 128
D_FF = 5632
S5_GROUP = 16
S5_WIDTH = 1024
S5_GROUPS = S5_WIDTH // S5_GROUP
S5_STATE = 64
ALPHA = (2.0 * DEPTH) ** 0.25
LN_EPS = 1e-5
FOX_W = N_FOX_HEADS * HEAD_DIM
DIL_W = N_DIL_HEADS * HEAD_DIM

F32 = jnp.float32
BF16 = jnp.bfloat16
NEG = -1e30

V7X_LANES = 128
V7X_SUBLANES = 8
V7X_VMEM_LIMIT_BYTES = 56 * 1024 * 1024

FFN_ROWS = 512
FFN_COLS = 512
MM_ROWS = 1024
MM_COLS = 1024
FOX_BLOCK = 512
DIL_ROWS = 1024
OUT_ROWS = 256
GLU_ROWS = 512
CUMSUM_ROWS = 512
S5_ROWS = 256
S5_GB = 16
S5_NB = S5_GROUPS // S5_GB
S5_LANES = S5_GB * S5_STATE


def _params(sem):
    return pltpu.CompilerParams(dimension_semantics=sem, vmem_limit_bytes=V7X_VMEM_LIMIT_BYTES)


def _layer_norm(v, g, b):
    mu = jnp.mean(v, axis=-1, keepdims=True)
    d = v - mu
    var = jnp.mean(d * d, axis=-1, keepdims=True)
    return d * lax.rsqrt(var + LN_EPS) * g + b


def _dot(a, b):
    return jnp.dot(a, b, preferred_element_type=F32)


def _dot_nt(a, b):
    return lax.dot_general(a, b, (((1,), (1,)), ((), ())), preferred_element_type=F32)


def _ffn_kernel(x_ref, xb_ref, wg_ref, wu_ref, wd_ref, g_ref, b_ref, o_ref, ob_ref):
    j = pl.program_id(1)
    xb = xb_ref[...]
    gate = _dot(xb, wg_ref[...])
    up = _dot(xb, wu_ref[...])
    h = (gate * jax.nn.sigmoid(gate) * up).astype(BF16)
    part = _dot(h, wd_ref[...])

    @pl.when(j == 0)
    def _():
        o_ref[...] = part

    @pl.when(j > 0)
    def _():
        o_ref[...] += part

    @pl.when(j == pl.num_programs(1) - 1)
    def _():
        y = _layer_norm(ALPHA * x_ref[...] + 0.5 * o_ref[...], g_ref[...], b_ref[...])
        o_ref[...] = y
        ob_ref[...] = y.astype(BF16)


def _ffn(x, xb, wg, wu, wd, g, b):
    s = x.shape[0]
    grid = (s // FFN_ROWS, D_FF // FFN_COLS)
    row = lambda i, j: (i, 0)
    return pl.pallas_call(
        _ffn_kernel,
        grid=grid,
        in_specs=[
            pl.BlockSpec((FFN_ROWS, D_MODEL), row),
            pl.BlockSpec((FFN_ROWS, D_MODEL), row),
            pl.BlockSpec((D_MODEL, FFN_COLS), lambda i, j: (0, j)),
            pl.BlockSpec((D_MODEL, FFN_COLS), lambda i, j: (0, j)),
            pl.BlockSpec((FFN_COLS, D_MODEL), lambda i, j: (j, 0)),
            pl.BlockSpec((1, D_MODEL), lambda i, j: (0, 0)),
            pl.BlockSpec((1, D_MODEL), lambda i, j: (0, 0)),
        ],
        out_specs=[pl.BlockSpec((FFN_ROWS, D_MODEL), row), pl.BlockSpec((FFN_ROWS, D_MODEL), row)],
        out_shape=[jax.ShapeDtypeStruct((s, D_MODEL), F32), jax.ShapeDtypeStruct((s, D_MODEL), BF16)],
        compiler_params=_params(("parallel", "arbitrary")),
        name="ffn",
    )(x, xb, wg, wu, wd, g, b)


def _attn_in_kernel(x_ref, w_ref, wf_ref, o_ref, f_ref):
    x = x_ref[...]
    o_ref[...] = _dot(x, w_ref[...]).astype(o_ref.dtype)

    @pl.when(pl.program_id(1) == 0)
    def _():
        f_ref[...] = _dot(x, wf_ref[...])


def _attn_in(xb, w, wf):
    s, n = xb.shape[0], w.shape[1]
    return pl.pallas_call(
        _attn_in_kernel,
        grid=(s // MM_ROWS, n // MM_COLS),
        in_specs=[
            pl.BlockSpec((MM_ROWS, D_MODEL), lambda i, j: (i, 0)),
            pl.BlockSpec((D_MODEL, MM_COLS), lambda i, j: (0, j)),
            pl.BlockSpec((D_MODEL, V7X_LANES), lambda i, j: (0, 0)),
        ],
        out_specs=[
            pl.BlockSpec((MM_ROWS, MM_COLS), lambda i, j: (i, j)),
            pl.BlockSpec((MM_ROWS, V7X_LANES), lambda i, j: (i, 0)),
        ],
        out_shape=[jax.ShapeDtypeStruct((s, n), BF16), jax.ShapeDtypeStruct((s, V7X_LANES), F32)],
        compiler_params=_params(("parallel", "arbitrary")),
        name="attn_in",
    )(xb, w, wf)


def _mm_kernel(x_ref, w_ref, o_ref):
    o_ref[...] = _dot(x_ref[...], w_ref[...]).astype(o_ref.dtype)


def _matmul(xb, w, out_dtype):
    s, n = xb.shape[0], w.shape[1]
    return pl.pallas_call(
        _mm_kernel,
        grid=(s // MM_ROWS, n // MM_COLS),
        in_specs=[
            pl.BlockSpec((MM_ROWS, xb.shape[1]), lambda i, j: (i, 0)),
            pl.BlockSpec((xb.shape[1], MM_COLS), lambda i, j: (0, j)),
        ],
        out_specs=pl.BlockSpec((MM_ROWS, MM_COLS), lambda i, j: (i, j)),
        out_shape=jax.ShapeDtypeStruct((s, n), out_dtype),
        compiler_params=_params(("parallel", "parallel")),
        name="matmul",
    )(xb, w)


def _split3(v):
    hi = v.astype(BF16)
    r1 = v - hi.astype(F32)
    mid = r1.astype(BF16)
    lo = (r1 - mid.astype(F32)).astype(BF16)
    return hi, mid, lo


def _cumsum_kernel(f_ref, bf_ref, c_ref, carry_ref):
    @pl.when(pl.program_id(0) == 0)
    def _():
        carry_ref[...] = jnp.zeros_like(carry_ref)

    z = f_ref[...] + bf_ref[...]
    log_f = jnp.minimum(z, 0.0) - jnp.log1p(jnp.exp(-jnp.abs(z)))
    n = z.shape[0]
    tri = (lax.broadcasted_iota(jnp.int32, (n, n), 0) >= lax.broadcasted_iota(jnp.int32, (n, n), 1)).astype(BF16)
    hi, mid, lo = _split3(log_f)
    c = _dot(tri, hi) + _dot(tri, mid) + _dot(tri, lo) + carry_ref[0:1, :]
    c_ref[...] = c
    carry_ref[...] = jnp.broadcast_to(c[n - 1:n, :], carry_ref.shape)


def _forget_cumsum(f_logit, b_f_row):
    s = f_logit.shape[0]
    return pl.pallas_call(
        _cumsum_kernel,
        grid=(s // CUMSUM_ROWS,),
        in_specs=[
            pl.BlockSpec((CUMSUM_ROWS, V7X_LANES), lambda i: (i, 0)),
            pl.BlockSpec((1, V7X_LANES), lambda i: (0, 0)),
        ],
        out_specs=pl.BlockSpec((CUMSUM_ROWS, V7X_LANES), lambda i: (i, 0)),
        out_shape=jax.ShapeDtypeStruct((s, V7X_LANES), F32),
        scratch_shapes=[pltpu.VMEM((V7X_SUBLANES, V7X_LANES), F32)],
        compiler_params=_params(("arbitrary",)),
        name="forget_cumsum",
    )(f_logit, b_f_row)


def _fox_kernel(qi_ref, kj_ref, q_ref, k_ref, v_ref, cq_ref, ck_ref, o_ref, m_ref, l_ref, acc_ref):
    t = pl.program_id(0)
    i = qi_ref[t]
    j = kj_ref[t]
    blk = q_ref.shape[0]

    @pl.when(j == 0)
    def _():
        m_ref[...] = jnp.full_like(m_ref, NEG)
        l_ref[...] = jnp.zeros_like(l_ref)
        acc_ref[...] = jnp.zeros_like(acc_ref)

    def step(diagonal):
        if diagonal:
            keep = lax.broadcasted_iota(jnp.int32, (blk, blk), 1) <= lax.broadcasted_iota(jnp.int32, (blk, blk), 0)
        for h in range(N_FOX_HEADS):
            hs = slice(h * HEAD_DIM, (h + 1) * HEAD_DIM)
            s = _dot_nt(q_ref[:, hs], k_ref[:, hs])
            s = s + cq_ref[:, h:h + 1] - ck_ref[h:h + 1, :]
            if diagonal:
                s = jnp.where(keep, s, NEG)
            m_prev = m_ref[h]
            m_new = jnp.maximum(m_prev, jnp.max(s, axis=1, keepdims=True))
            p = jnp.exp(s - m_new[:, 0:1])
            alpha = jnp.exp(m_prev - m_new)
            l_ref[h] = alpha * l_ref[h] + jnp.sum(p, axis=1, keepdims=True)
            m_ref[h] = m_new
            acc_ref[:, hs] = alpha * acc_ref[:, hs] + _dot(p.astype(BF16), v_ref[:, hs])

    @pl.when(j < i)
    def _():
        step(False)

    @pl.when(j == i)
    def _():
        step(True)
        for h in range(N_FOX_HEADS):
            hs = slice(h * HEAD_DIM, (h + 1) * HEAD_DIM)
            o_ref[:, hs] = (acc_ref[:, hs] / l_ref[h]).astype(o_ref.dtype)


def _fox_attention(qkv, c_rows, c_cols):
    s = qkv.shape[0]
    nb = s // FOX_BLOCK
    qi = jnp.asarray([i for i in range(nb) for _ in range(i + 1)], jnp.int32)
    kj = jnp.asarray([j for i in range(nb) for j in range(i + 1)], jnp.int32)
    grid_spec = pltpu.PrefetchScalarGridSpec(
        num_scalar_prefetch=2,
        grid=(qi.shape[0],),
        in_specs=[
            pl.BlockSpec((FOX_BLOCK, FOX_W), lambda t, qi, kj: (qi[t], 0)),
            pl.BlockSpec((FOX_BLOCK, FOX_W), lambda t, qi, kj: (kj[t], 1)),
            pl.BlockSpec((FOX_BLOCK, FOX_W), lambda t, qi, kj: (kj[t], 2)),
            pl.BlockSpec((FOX_BLOCK, V7X_LANES), lambda t, qi, kj: (qi[t], 0)),
            pl.BlockSpec((N_FOX_HEADS, FOX_BLOCK), lambda t, qi, kj: (0, kj[t])),
        ],
        out_specs=pl.BlockSpec((FOX_BLOCK, FOX_W), lambda t, qi, kj: (qi[t], 0)),
        scratch_shapes=[
            pltpu.VMEM((N_FOX_HEADS, FOX_BLOCK, V7X_LANES), F32),
            pltpu.VMEM((N_FOX_HEADS, FOX_BLOCK, V7X_LANES), F32),
            pltpu.VMEM((FOX_BLOCK, FOX_W), F32),
        ],
    )
    return pl.pallas_call(
        _fox_kernel,
        grid_spec=grid_spec,
        out_shape=jax.ShapeDtypeStruct((s, FOX_W), BF16),
        compiler_params=_params(("arbitrary",)),
        name="fox_attention",
    )(qi, kj, qkv, qkv, qkv, c_rows, c_cols)


def _dil_kernel(q_ref, k_ref, v_ref, kh_ref, vh_ref, o_ref, lse_ref):
    has_prev_block = pl.program_id(1) > 0
    b = DIL_BLOCK
    jr = lax.broadcasted_iota(jnp.int32, (b, 2 * b), 0)
    mc = lax.broadcasted_iota(jnp.int32, (b, 2 * b), 1)
    band = (mc >= jr) & (mc <= jr + b)
    cur_half = mc >= b
    lane = lax.broadcasted_iota(jnp.int32, (b, V7X_LANES), 1)

    def unit(q, k2, v2, keep):
        s = jnp.where(keep, _dot_nt(q, k2), NEG)
        m = jnp.max(s, axis=1, keepdims=True)
        p = jnp.exp(s - m)
        l = jnp.sum(p, axis=1, keepdims=True)
        return _dot(p.astype(BF16), v2) / l, m + jnp.log(l)

    def rows(r0, k_of, v_of, keep):
        lse_tile = jnp.zeros((b, V7X_LANES), F32)
        for h in range(N_DIL_HEADS):
            hs = slice(h * HEAD_DIM, (h + 1) * HEAD_DIM)
            o, lse = unit(q_ref[pl.ds(r0, b), hs], k_of(hs), v_of(hs), keep)
            o_ref[pl.ds(r0, b), hs] = o.astype(o_ref.dtype)
            lse_tile = jnp.where(lane == h, lse, lse_tile)
        lse_ref[pl.ds(r0, b), :] = lse_tile

    rows(0,
         lambda hs: jnp.concatenate([kh_ref[:, hs], k_ref[0:b, hs]], axis=0),
         lambda hs: jnp.concatenate([vh_ref[:, hs], v_ref[0:b, hs]], axis=0),
         band & (cur_half | has_prev_block))

    def body(n, carry):
        r0 = pl.multiple_of(n * b, b)
        p0 = pl.multiple_of((n - 1) * b, b)
        rows(r0, lambda hs: k_ref[pl.ds(p0, 2 * b), hs], lambda hs: v_ref[pl.ds(p0, 2 * b), hs], band)
        return carry

    lax.fori_loop(1, q_ref.shape[0] // b, body, 0)


def _dilated_branch(qkv, dilation):
    s = qkv.shape[0]
    length = s // dilation
    rows = min(DIL_ROWS, length)
    view = qkv.reshape(length, dilation * qkv.shape[1])
    nblk = qkv.shape[1] // DIL_W
    sub = rows // DIL_BLOCK
    main = lambda c: pl.BlockSpec((rows, DIL_W), lambda r, i: (i, r * nblk + c))
    halo = lambda c: pl.BlockSpec((DIL_BLOCK, DIL_W), lambda r, i: (jnp.maximum(i * sub - 1, 0), r * nblk + c))
    o, lse = pl.pallas_call(
        _dil_kernel,
        grid=(dilation, length // rows),
        in_specs=[main(3), main(4), main(5), halo(4), halo(5)],
        out_specs=[
            pl.BlockSpec((rows, DIL_W), lambda r, i: (i, r)),
            pl.BlockSpec((rows, V7X_LANES), lambda r, i: (i, r)),
        ],
        out_shape=[
            jax.ShapeDtypeStruct((length, dilation * DIL_W), BF16),
            jax.ShapeDtypeStruct((length, dilation * V7X_LANES), F32),
        ],
        compiler_params=_params(("parallel", "parallel")),
        name=f"dilated_d{dilation}",
    )(view, view, view, view, view)
    return o.reshape(s, DIL_W), lse.reshape(s, V7X_LANES)


def _attn_out_kernel(x_ref, ya_ref, o1_ref, o2_ref, o3_ref, l1_ref, l2_ref, l3_ref, w_ref, g_ref, b_ref,
                     o_ref, ob_ref):
    l1, l2, l3 = l1_ref[...], l2_ref[...], l3_ref[...]
    mx = jnp.maximum(jnp.maximum(l1, l2), l3)
    e1, e2, e3 = jnp.exp(l1 - mx), jnp.exp(l2 - mx), jnp.exp(l3 - mx)
    den = e1 + e2 + e3
    w1, w2, w3 = e1 / den, e2 / den, e3 / den
    mixed = [ya_ref[...]]
    for h in range(N_DIL_HEADS):
        hs = slice(h * HEAD_DIM, (h + 1) * HEAD_DIM)
        mix = (w1[:, h:h + 1] * o1_ref[:, hs].astype(F32) + w2[:, h:h + 1] * o2_ref[:, hs].astype(F32)
               + w3[:, h:h + 1] * o3_ref[:, hs].astype(F32))
        mixed.append(mix.astype(BF16))
    y = _dot(jnp.concatenate(mixed, axis=1), w_ref[...])
    out = _layer_norm(ALPHA * x_ref[...] + y, g_ref[...], b_ref[...])
    o_ref[...] = out
    ob_ref[...] = out.astype(BF16)


def _attn_out(x, ya, outs, lses, w, g, b):
    s = x.shape[0]
    row = lambda width: pl.BlockSpec((OUT_ROWS, width), lambda i: (i, 0))
    const = lambda shape: pl.BlockSpec(shape, lambda i: (0, 0))
    return pl.pallas_call(
        _attn_out_kernel,
        grid=(s // OUT_ROWS,),
        in_specs=[row(D_MODEL), row(FOX_W), row(DIL_W), row(DIL_W), row(DIL_W),
                  row(V7X_LANES), row(V7X_LANES), row(V7X_LANES),
                  const((FOX_W + DIL_W, D_MODEL)), const((1, D_MODEL)), const((1, D_MODEL))],
        out_specs=[row(D_MODEL), row(D_MODEL)],
        out_shape=[jax.ShapeDtypeStruct((s, D_MODEL), F32), jax.ShapeDtypeStruct((s, D_MODEL), BF16)],
        compiler_params=_params(("parallel",)),
        name="attn_out",
    )(x, ya, *outs, *lses, w, g, b)


def _s5_prep_kernel(lr_ref, li_ref, ldt_ref, br_ref, bi_ref, pr_ref, pi_ref, bbr_ref, bbi_ref):
    lr, li = lr_ref[...], li_ref[...]
    dt = jnp.exp(ldt_ref[...])
    mag = jnp.exp(lr * dt)
    ar, ai = mag * jnp.cos(li * dt), mag * jnp.sin(li * dt)
    den = lr * lr + li * li
    cr = ((ar - 1.0) * lr + ai * li) / den
    ci = (ai * lr - (ar - 1.0) * li) / den
    br, bi = br_ref[...], bi_ref[...]
    bbr_ref[...] = cr * br - ci * bi
    bbi_ref[...] = cr * bi + ci * br
    pr, pi = ar, ai
    pr_ref[0] = pr
    pi_ref[0] = pi
    for k in range(1, V7X_SUBLANES):
        pr, pi = pr * ar - pi * ai, pr * ai + pi * ar
        pr_ref[k] = pr
        pi_ref[k] = pi


def _s5_prep(lam_re, lam_im, log_dt, b_re, b_im):
    rep = lambda v: jnp.repeat(v, S5_GROUP, axis=0)
    rows = S5_GROUPS * S5_GROUP
    bt = lambda v: v.transpose(0, 2, 1).reshape(rows, S5_STATE)
    mat = jax.ShapeDtypeStruct((rows, S5_STATE), F32)
    pw = jax.ShapeDtypeStruct((V7X_SUBLANES, rows, S5_STATE), F32)
    return pl.pallas_call(_s5_prep_kernel, out_shape=[pw, pw, mat, mat], name="s5_prep")(
        rep(lam_re), rep(lam_im), rep(jnp.broadcast_to(log_dt[:, None], lam_re.shape)), bt(b_re), bt(b_im))


def _s5_core_kernel(u_ref, bblk_ref, cblk_ref, pr_ref, pi_ref, d_ref, z_ref, sr_ref, si_ref, cr_ref, ci_ref):
    @pl.when(pl.program_id(0) == 0)
    def _():
        cr_ref[...] = jnp.zeros_like(cr_ref)
        ci_ref[...] = jnp.zeros_like(ci_ref)

    rows = u_ref.shape[0]
    sub = lax.broadcasted_iota(jnp.int32, (V7X_SUBLANES, S5_LANES), 0)
    width = S5_GB * S5_GROUP
    ys = []
    for jb in range(S5_NB):
        ub = u_ref[:, jb * width:(jb + 1) * width].astype(BF16)
        bu = _dot(ub, bblk_ref[jb])
        sr_ref[...] = bu[:, :S5_LANES]
        si_ref[...] = bu[:, S5_LANES:]
        p_re, p_im = pr_ref[jb], pi_ref[jb]
        steps = [(k, p_re[k - 1:k, :], p_im[k - 1:k, :]) for k in (1, 2, 4)]

        def block(n, carry):
            c_re, c_im = carry
            r0 = pl.multiple_of(n * V7X_SUBLANES, V7X_SUBLANES)
            vr, vi = sr_ref[pl.ds(r0, V7X_SUBLANES), :], si_ref[pl.ds(r0, V7X_SUBLANES), :]
            for k, ar, ai in steps:
                tr = jnp.where(sub >= k, pltpu.roll(vr, k, 0), 0.0)
                ti = jnp.where(sub >= k, pltpu.roll(vi, k, 0), 0.0)
                vr, vi = vr + ar * tr - ai * ti, vi + ar * ti + ai * tr
            vr, vi = vr + p_re * c_re - p_im * c_im, vi + p_re * c_im + p_im * c_re
            sr_ref[pl.ds(r0, V7X_SUBLANES), :] = vr
            si_ref[pl.ds(r0, V7X_SUBLANES), :] = vi
            return vr[V7X_SUBLANES - 1:, :], vi[V7X_SUBLANES - 1:, :]

        c_re, c_im = lax.fori_loop(0, rows // V7X_SUBLANES, block,
                                   (cr_ref[jb:jb + 1, :], ci_ref[jb:jb + 1, :]), unroll=2)
        cr_ref[jb:jb + 1, :] = c_re
        ci_ref[jb:jb + 1, :] = c_im
        state = jnp.concatenate([sr_ref[...], si_ref[...]], axis=1).astype(BF16)
        ys.append(_dot(state, cblk_ref[jb]))
    y = jnp.concatenate(ys, axis=1) + d_ref[...] * u_ref[...]
    z = 0.5 * y * (1.0 + jnp.tanh(math.sqrt(2.0 / math.pi) * (y + 0.044715 * (y * y * y))))
    z_ref[...] = z.astype(z_ref.dtype)


def _s5_core(u, bblk, cblk, p_re, p_im, d_row):
    s = u.shape[0]
    const = lambda shape: pl.BlockSpec(shape, lambda t: (0,) * len(shape))
    return pl.pallas_call(
        _s5_core_kernel,
        grid=(s // S5_ROWS,),
        in_specs=[pl.BlockSpec((S5_ROWS, S5_WIDTH), lambda t: (t, 0)),
                  const(bblk.shape), const(cblk.shape), const(p_re.shape), const(p_im.shape),
                  const((1, S5_WIDTH))],
        out_specs=pl.BlockSpec((S5_ROWS, S5_WIDTH), lambda t: (t, 0)),
        out_shape=jax.ShapeDtypeStruct((s, S5_WIDTH), BF16),
        scratch_shapes=[pltpu.VMEM((S5_ROWS, S5_LANES), F32), pltpu.VMEM((S5_ROWS, S5_LANES), F32),
                        pltpu.VMEM((V7X_SUBLANES, S5_LANES), F32), pltpu.VMEM((V7X_SUBLANES, S5_LANES), F32)],
        compiler_params=_params(("arbitrary",)),
        name="s5_core",
    )(u, bblk, cblk, p_re, p_im, d_row)


def _block_diag(blocks):
    nb, gb, r, c = blocks.shape
    eye = jnp.eye(gb, dtype=blocks.dtype)
    return jnp.einsum("ngrc,gk->ngrkc", blocks, eye).reshape(nb, gb * r, gb * c)


def _glu_kernel(x_ref, z_ref, wo_ref, wg_ref, g_ref, b_ref, o_ref, ob_ref):
    z = z_ref[...]
    y = _dot(z, wo_ref[...]) * jax.nn.sigmoid(_dot(z, wg_ref[...]))
    out = _layer_norm(ALPHA * x_ref[...] + y, g_ref[...], b_ref[...])
    o_ref[...] = out
    ob_ref[...] = out.astype(BF16)


def _glu(x, z, wo, wg, g, b):
    s = x.shape[0]
    row = lambda width: pl.BlockSpec((GLU_ROWS, width), lambda i: (i, 0))
    const = lambda shape: pl.BlockSpec(shape, lambda i: (0, 0))
    return pl.pallas_call(
        _glu_kernel,
        grid=(s // GLU_ROWS,),
        in_specs=[row(D_MODEL), row(S5_WIDTH), const((S5_WIDTH, D_MODEL)), const((S5_WIDTH, D_MODEL)),
                  const((1, D_MODEL)), const((1, D_MODEL))],
        out_specs=[row(D_MODEL), row(D_MODEL)],
        out_shape=[jax.ShapeDtypeStruct((s, D_MODEL), F32), jax.ShapeDtypeStruct((s, D_MODEL), BF16)],
        compiler_params=_params(("parallel",)),
        name="glu",
    )(x, z, wo, wg, g, b)


def _attention_mixer(x, xb, w_in, b_f, w_out, g, b):
    scale = HEAD_DIM ** -0.5
    fox_q, fox_k, fox_v = (w_in[:, n * FOX_W:(n + 1) * FOX_W] for n in range(3))
    w_f = w_in[:, 3 * FOX_W:3 * FOX_W + N_FOX_HEADS]
    d0 = 3 * FOX_W + N_FOX_HEADS
    dil_q, dil_k, dil_v = (w_in[:, d0 + n * DIL_W:d0 + (n + 1) * DIL_W] for n in range(3))
    w_cat = jnp.concatenate([fox_q * scale, fox_k, fox_v, dil_q * scale, dil_k, dil_v], axis=1).astype(BF16)
    w_f = jnp.pad(w_f, ((0, 0), (0, V7X_LANES - N_FOX_HEADS))).astype(BF16)
    qkv, f_logit = _attn_in(xb, w_cat, w_f)
    c_rows = _forget_cumsum(f_logit, jnp.pad(b_f, (0, V7X_LANES - N_FOX_HEADS))[None, :])
    c_cols = c_rows[:, :N_FOX_HEADS].T
    ya = _fox_attention(qkv, c_rows, c_cols)
    branches = [_dilated_branch(qkv, d) for (_, d) in DIL_PAIRS]
    return _attn_out(x, ya, [o for o, _ in branches], [l for _, l in branches], w_out.astype(BF16), g, b)


def _s5_mixer(x, xb, w_in, lam_re, lam_im, log_dt, b_re, b_im, c_re, c_im, d_skip, w_glu_out, w_glu_gate, g, b):
    u = _matmul(xb, w_in.astype(BF16), F32)
    p_re, p_im, bb_re, bb_im = _s5_prep(lam_re, lam_im, log_dt, b_re, b_im)
    lanes = lambda p: p[:, ::S5_GROUP, :].reshape(V7X_SUBLANES, S5_NB, S5_LANES).transpose(1, 0, 2)
    blk = lambda v: v.reshape(S5_NB, S5_GB, S5_GROUP, S5_STATE)
    bblk = jnp.concatenate([_block_diag(blk(bb_re)), _block_diag(blk(bb_im))], axis=2).astype(BF16)
    ct = lambda v: v.transpose(0, 2, 1).reshape(S5_NB, S5_GB, S5_STATE, S5_GROUP)
    cblk = jnp.concatenate([_block_diag(ct(c_re)), -_block_diag(ct(c_im))], axis=1).astype(BF16)
    z = _s5_core(u, bblk, cblk, lanes(p_re), lanes(p_im), d_skip[None, :])
    return _glu(x, z, w_glu_out.astype(BF16), w_glu_gate.astype(BF16), g, b)


def kernel(x, ffn1_w_gate, ffn1_w_up, ffn1_w_down, ffn2_w_gate, ffn2_w_up, ffn2_w_down, ln_gain, ln_bias,
           attn_w_in, attn_b_f, attn_w_out, s5_w_in, s5_lambda_re, s5_lambda_im, s5_log_dt, s5_b_re, s5_b_im,
           s5_c_re, s5_c_im, s5_d, s5_w_glu_out, s5_w_glu_gate):
    batch, seq, _ = x.shape
    assert batch == 1
    h = x.reshape(seq, D_MODEL)
    hb = h.astype(BF16)
    ln = lambda i, k: (ln_gain[i, k][None, :], ln_bias[i, k][None, :])
    for i in range(DEPTH):
        h, hb = _ffn(h, hb, ffn1_w_gate[i].astype(BF16), ffn1_w_up[i].astype(BF16),
                     ffn1_w_down[i].astype(BF16), *ln(i, 0))
        j = i // 2
        if i % 2 == 0:
            h, hb = _attention_mixer(h, hb, attn_w_in[j], attn_b_f[j], attn_w_out[j], *ln(i, 1))
        else:
            h, hb = _s5_mixer(h, hb, s5_w_in[j], s5_lambda_re[j], s5_lambda_im[j], s5_log_dt[j], s5_b_re[j],
                              s5_b_im[j], s5_c_re[j], s5_c_im[j], s5_d[j], s5_w_glu_out[j], s5_w_glu_gate[j],
                              *ln(i, 1))
        h, hb = _ffn(h, hb, ffn2_w_gate[i].astype(BF16), ffn2_w_up[i].astype(BF16),
                     ffn2_w_down[i].astype(BF16), *ln(i, 2))
    return h.reshape(batch, seq, D_MODEL)
```

```python
import functools
import math

import jax
import jax.numpy as jnp
from jax import lax
from jax.experimental import pallas as pl
from jax.experimental.pallas import tpu as pltpu

D_MODEL = 2048
DEPTH = 2
HEAD_DIM = 128
N_FOX_HEADS = 8
N_DIL_HEADS = 8
DIL_PAIRS = ((128, 1), (512, 4), (2048, 16))
DIL_BLOCK = 128
D_FF = 5632
S5_GROUP = 16
S5_WIDTH = 1024
S5_GROUPS = S5_WIDTH // S5_GROUP
S5_STATE = 64
ALPHA = (2.0 * DEPTH) ** 0.25
LN_EPS = 1e-5
FOX_W = N_FOX_HEADS * HEAD_DIM
DIL_W = N_DIL_HEADS * HEAD_DIM

F32 = jnp.float32
BF16 = jnp.bfloat16
NEG = -1e30
LOG2E = math.log2(math.e)
FOX_AUG = 128
FOX_VROWS = HEAD_DIM + 16

V7X_LANES = 128
V7X_SUBLANES = 8
V7X_VMEM_LIMIT_BYTES = 56 * 1024 * 1024

FFN_ROWS = 512
FFN_COLS = 512
MM_ROWS = 1024
MM_COLS = 1024
FOX_BLOCK = 512
DIL_ROWS = 1024
OUT_ROWS = 256
GLU_ROWS = 512
CUMSUM_ROWS = 512
S5_ROWS = 256
S5_GB = 16
S5_NB = S5_GROUPS // S5_GB
S5_LANES = S5_GB * S5_STATE


def _params(sem):
    return pltpu.CompilerParams(dimension_semantics=sem, vmem_limit_bytes=V7X_VMEM_LIMIT_BYTES)


def _layer_norm(v, g, b):
    mu = jnp.mean(v, axis=-1, keepdims=True)
    d = v - mu
    var = jnp.mean(d * d, axis=-1, keepdims=True)
    return d * lax.rsqrt(var + LN_EPS) * g + b


def _dot(a, b):
    return jnp.dot(a, b, preferred_element_type=F32)


def _dot_nt(a, b):
    return lax.dot_general(a, b, (((1,), (1,)), ((), ())), preferred_element_type=F32)


def _ffn_kernel(x_ref, xb_ref, wg_ref, wu_ref, wd_ref, g_ref, b_ref, o_ref, ob_ref, h_ref):
    j = pl.program_id(1)
    last = pl.num_programs(1) - 1

    def hidden():
        xb = xb_ref[...]
        gate = _dot(xb, wg_ref[...])
        up = _dot(xb, wu_ref[...])
        return (gate * jax.nn.sigmoid(gate) * up).astype(BF16)

    def down(slot):
        return _dot(h_ref[slot], wd_ref[...])

    @pl.when(j == 0)
    def _():
        h_ref[0] = hidden()

    @pl.when(j == 1)
    def _():
        o_ref[...] = down(0)
        h_ref[1] = hidden()

    @pl.when((j > 1) & (j < last))
    def _():
        o_ref[...] += down((j - 1) % 2)
        h_ref[j % 2] = hidden()

    @pl.when(j == last)
    def _():
        y = _layer_norm(ALPHA * x_ref[...] + 0.5 * (o_ref[...] + down((j - 1) % 2)), g_ref[...], b_ref[...])
        o_ref[...] = y
        ob_ref[...] = y.astype(BF16)


def _ffn(x, xb, wg, wu, wd, g, b):
    s = x.shape[0]
    chunks = D_FF // FFN_COLS
    row = lambda i, j: (i, 0)
    up_map = lambda i, j: (0, jnp.minimum(j, chunks - 1))
    return pl.pallas_call(
        _ffn_kernel,
        grid=(s // FFN_ROWS, chunks + 1),
        in_specs=[
            pl.BlockSpec((FFN_ROWS, D_MODEL), row),
            pl.BlockSpec((FFN_ROWS, D_MODEL), row),
            pl.BlockSpec((D_MODEL, FFN_COLS), up_map),
            pl.BlockSpec((D_MODEL, FFN_COLS), up_map),
            pl.BlockSpec((FFN_COLS, D_MODEL), lambda i, j: (jnp.maximum(j - 1, 0), 0)),
            pl.BlockSpec((1, D_MODEL), lambda i, j: (0, 0)),
            pl.BlockSpec((1, D_MODEL), lambda i, j: (0, 0)),
        ],
        out_specs=[pl.BlockSpec((FFN_ROWS, D_MODEL), row), pl.BlockSpec((FFN_ROWS, D_MODEL), row)],
        out_shape=[jax.ShapeDtypeStruct((s, D_MODEL), F32), jax.ShapeDtypeStruct((s, D_MODEL), BF16)],
        scratch_shapes=[pltpu.VMEM((2, FFN_ROWS, FFN_COLS), BF16)],
        compiler_params=_params(("parallel", "arbitrary")),
        name="ffn",
    )(x, xb, wg, wu, wd, g, b)


def _attn_in_kernel(x_ref, w_ref, wf_ref, o_ref, f_ref):
    x = x_ref[...]
    o_ref[...] = _dot(x, w_ref[...]).astype(o_ref.dtype)

    @pl.when(pl.program_id(1) == 0)
    def _():
        f_ref[...] = _dot(x, wf_ref[...])


def _attn_in(xb, w, wf):
    s, n = xb.shape[0], w.shape[1]
    return pl.pallas_call(
        _attn_in_kernel,
        grid=(s // MM_ROWS, n // MM_COLS),
        in_specs=[
            pl.BlockSpec((MM_ROWS, D_MODEL), lambda i, j: (i, 0)),
            pl.BlockSpec((D_MODEL, MM_COLS), lambda i, j: (0, j)),
            pl.BlockSpec((D_MODEL, V7X_LANES), lambda i, j: (0, 0)),
        ],
        out_specs=[
            pl.BlockSpec((MM_ROWS, MM_COLS), lambda i, j: (i, j)),
            pl.BlockSpec((MM_ROWS, V7X_LANES), lambda i, j: (i, 0)),
        ],
        out_shape=[jax.ShapeDtypeStruct((s, n), BF16), jax.ShapeDtypeStruct((s, V7X_LANES), F32)],
        compiler_params=_params(("parallel", "arbitrary")),
        name="attn_in",
    )(xb, w, wf)


def _mm_kernel(x_ref, w_ref, o_ref):
    o_ref[...] = _dot(x_ref[...], w_ref[...]).astype(o_ref.dtype)


def _matmul(xb, w, out_dtype):
    s, n = xb.shape[0], w.shape[1]
    return pl.pallas_call(
        _mm_kernel,
        grid=(s // MM_ROWS, n // MM_COLS),
        in_specs=[
            pl.BlockSpec((MM_ROWS, xb.shape[1]), lambda i, j: (i, 0)),
            pl.BlockSpec((xb.shape[1], MM_COLS), lambda i, j: (0, j)),
        ],
        out_specs=pl.BlockSpec((MM_ROWS, MM_COLS), lambda i, j: (i, j)),
        out_shape=jax.ShapeDtypeStruct((s, n), out_dtype),
        compiler_params=_params(("parallel", "parallel")),
        name="matmul",
    )(xb, w)


def _split3(v):
    hi = v.astype(BF16)
    r1 = v - hi.astype(F32)
    mid = r1.astype(BF16)
    lo = (r1 - mid.astype(F32)).astype(BF16)
    return hi, mid, lo


def _cumsum_kernel(f_ref, bf_ref, c_ref, carry_ref):
    @pl.when(pl.program_id(0) == 0)
    def _():
        carry_ref[...] = jnp.zeros_like(carry_ref)

    z = f_ref[...] + bf_ref[...]
    log_f = jnp.minimum(z, 0.0) - jnp.log1p(jnp.exp(-jnp.abs(z)))
    n = z.shape[0]
    tri = (lax.broadcasted_iota(jnp.int32, (n, n), 0) >= lax.broadcasted_iota(jnp.int32, (n, n), 1)).astype(BF16)
    hi, mid, lo = _split3(log_f)
    c = _dot(tri, hi) + _dot(tri, mid) + _dot(tri, lo) + carry_ref[0:1, :]
    c_ref[...] = c
    carry_ref[...] = jnp.broadcast_to(c[n - 1:n, :], carry_ref.shape)


def _forget_cumsum(f_logit, b_f_row):
    s = f_logit.shape[0]
    return pl.pallas_call(
        _cumsum_kernel,
        grid=(s // CUMSUM_ROWS,),
        in_specs=[
            pl.BlockSpec((CUMSUM_ROWS, V7X_LANES), lambda i: (i, 0)),
            pl.BlockSpec((1, V7X_LANES), lambda i: (0, 0)),
        ],
        out_specs=pl.BlockSpec((CUMSUM_ROWS, V7X_LANES), lambda i: (i, 0)),
        out_shape=jax.ShapeDtypeStruct((s, V7X_LANES), F32),
        scratch_shapes=[pltpu.VMEM((V7X_SUBLANES, V7X_LANES), F32)],
        compiler_params=_params(("arbitrary",)),
        name="forget_cumsum",
    )(f_logit, b_f_row)


def _fox_prep_kernel(qkv_ref, c_ref, qt_ref, ka_ref, vt_ref):
    rows = qkv_ref.shape[0]
    lane = lax.broadcasted_iota(jnp.int32, (rows, V7X_LANES), 1)
    bias = c_ref[...] * (-LOG2E)
    ones_rows = (lax.broadcasted_iota(jnp.int32, (FOX_AUG, rows), 0) < 3).astype(BF16)
    sum_rows = (lax.broadcasted_iota(jnp.int32, (FOX_VROWS - HEAD_DIM, rows), 0) < 1).astype(BF16)
    for h in range(N_FOX_HEADS):
        hs = slice(h * HEAD_DIM, (h + 1) * HEAD_DIM)
        a0 = h * (HEAD_DIM + FOX_AUG)
        qt_ref[a0:a0 + HEAD_DIM, :] = qkv_ref[:, hs].astype(F32).T.astype(BF16)
        qt_ref[a0 + HEAD_DIM:a0 + HEAD_DIM + FOX_AUG, :] = ones_rows
        v0 = h * FOX_VROWS
        vt_ref[v0:v0 + HEAD_DIM, :] = (
            qkv_ref[:, 2 * FOX_W + h * HEAD_DIM:2 * FOX_W + (h + 1) * HEAD_DIM].astype(F32).T.astype(BF16))
        vt_ref[v0 + HEAD_DIM:v0 + FOX_VROWS, :] = sum_rows
        hi, mid, lo = _split3(jnp.broadcast_to(bias[:, h:h + 1], (rows, V7X_LANES)))
        aug = jnp.where(lane == 0, hi.astype(F32), jnp.where(lane == 1, mid.astype(F32),
                                                              jnp.where(lane == 2, lo.astype(F32), 0.0)))
        ka_ref[:, a0:a0 + HEAD_DIM] = qkv_ref[:, FOX_W + h * HEAD_DIM:FOX_W + (h + 1) * HEAD_DIM]
        ka_ref[:, a0 + HEAD_DIM:a0 + HEAD_DIM + FOX_AUG] = aug.astype(BF16)


def _fox_prep(qkv, c_rows):
    s = qkv.shape[0]
    wide = N_FOX_HEADS * (HEAD_DIM + FOX_AUG)
    vrows = N_FOX_HEADS * FOX_VROWS
    return pl.pallas_call(
        _fox_prep_kernel,
        grid=(s // FOX_BLOCK,),
        in_specs=[pl.BlockSpec((FOX_BLOCK, 3 * FOX_W), lambda i: (i, 0)),
                  pl.BlockSpec((FOX_BLOCK, V7X_LANES), lambda i: (i, 0))],
        out_specs=[pl.BlockSpec((wide, FOX_BLOCK), lambda i: (0, i)),
                   pl.BlockSpec((FOX_BLOCK, wide), lambda i: (i, 0)),
                   pl.BlockSpec((vrows, FOX_BLOCK), lambda i: (0, i))],
        out_shape=[jax.ShapeDtypeStruct((wide, s), BF16), jax.ShapeDtypeStruct((s, wide), BF16),
                   jax.ShapeDtypeStruct((vrows, s), BF16)],
        compiler_params=_params(("parallel",)),
        name="fox_prep",
    )(qkv, c_rows)


def _fox_kernel(qi_ref, kj_ref, qt_ref, ka_ref, vt_ref, o_ref, m_ref, acc_ref, s_ref):
    t = pl.program_id(0)
    i = qi_ref[t]
    j = kj_ref[t]
    blk = ka_ref.shape[0]
    aw = HEAD_DIM + FOX_AUG

    @pl.when(j == 0)
    def _():
        m_ref[...] = jnp.full_like(m_ref, NEG)
        acc_ref[...] = jnp.zeros_like(acc_ref)

    def scores(h):
        return _dot(ka_ref[:, h * aw:(h + 1) * aw], qt_ref[h * aw:(h + 1) * aw, :])

    def step(diagonal):
        if diagonal:
            keep = lax.broadcasted_iota(jnp.int32, (blk, blk), 0) <= lax.broadcasted_iota(jnp.int32, (blk, blk), 1)
        s_ref[0] = scores(0)
        for h in range(N_FOX_HEADS):
            if h + 1 < N_FOX_HEADS:
                s_ref[(h + 1) % 2] = scores(h + 1)
            s = s_ref[h % 2]
            if diagonal:
                s = jnp.where(keep, s, NEG)
            m_prev = m_ref[h:h + 1, :]
            m_new = jnp.maximum(m_prev, jnp.max(s, axis=0, keepdims=True))
            p = jnp.exp2(s - m_new).astype(BF16)
            alpha = jnp.exp2(m_prev - m_new)
            m_ref[h:h + 1, :] = m_new
            vs = slice(h * FOX_VROWS, (h + 1) * FOX_VROWS)
            acc_ref[vs, :] = alpha * acc_ref[vs, :] + _dot(vt_ref[vs, :], p)

    @pl.when(j < i)
    def _():
        step(False)

    @pl.when(j == i)
    def _():
        step(True)
        for h in range(N_FOX_HEADS):
            v0 = h * FOX_VROWS
            out_t = acc_ref[v0:v0 + HEAD_DIM, :] / acc_ref[v0 + HEAD_DIM:v0 + HEAD_DIM + 1, :]
            o_ref[:, h * HEAD_DIM:(h + 1) * HEAD_DIM] = out_t.T.astype(o_ref.dtype)


def _fox_attention(qt, ka, vt):
    s = ka.shape[0]
    nb = s // FOX_BLOCK
    qi = jnp.asarray([i for i in range(nb) for _ in range(i + 1)], jnp.int32)
    kj = jnp.asarray([j for i in range(nb) for j in range(i + 1)], jnp.int32)
    wide = ka.shape[1]
    vrows = vt.shape[0]
    grid_spec = pltpu.PrefetchScalarGridSpec(
        num_scalar_prefetch=2,
        grid=(qi.shape[0],),
        in_specs=[
            pl.BlockSpec((wide, FOX_BLOCK), lambda t, qi, kj: (0, qi[t])),
            pl.BlockSpec((FOX_BLOCK, wide), lambda t, qi, kj: (kj[t], 0)),
            pl.BlockSpec((vrows, FOX_BLOCK), lambda t, qi, kj: (0, kj[t])),
        ],
        out_specs=pl.BlockSpec((FOX_BLOCK, FOX_W), lambda t, qi, kj: (qi[t], 0)),
        scratch_shapes=[
            pltpu.VMEM((N_FOX_HEADS, FOX_BLOCK), F32),
            pltpu.VMEM((vrows, FOX_BLOCK), F32),
            pltpu.VMEM((2, FOX_BLOCK, FOX_BLOCK), F32),
        ],
    )
    return pl.pallas_call(
        _fox_kernel,
        grid_spec=grid_spec,
        out_shape=jax.ShapeDtypeStruct((s, FOX_W), BF16),
        compiler_params=_params(("arbitrary",)),
        name="fox_attention",
    )(qi, kj, qt, ka, vt)


def _dil_kernel(q_ref, k_ref, v_ref, kh_ref, vh_ref, o_ref, lse_ref):
    has_prev_block = pl.program_id(1) > 0
    b = DIL_BLOCK
    jr = lax.broadcasted_iota(jnp.int32, (b, 2 * b), 0)
    mc = lax.broadcasted_iota(jnp.int32, (b, 2 * b), 1)
    band = (mc >= jr) & (mc <= jr + b)
    cur_half = mc >= b
    lane = lax.broadcasted_iota(jnp.int32, (b, V7X_LANES), 1)

    def unit(q, k2, v2, keep):
        s = jnp.where(keep, _dot_nt(q, k2), NEG)
        m = jnp.max(s, axis=1, keepdims=True)
        p = jnp.exp(s - m)
        l = jnp.sum(p, axis=1, keepdims=True)
        return _dot(p.astype(BF16), v2) / l, m + jnp.log(l)

    def rows(r0, k_of, v_of, keep):
        lse_tile = jnp.zeros((b, V7X_LANES), F32)
        for h in range(N_DIL_HEADS):
            hs = slice(h * HEAD_DIM, (h + 1) * HEAD_DIM)
            o, lse = unit(q_ref[pl.ds(r0, b), hs], k_of(hs), v_of(hs), keep)
            o_ref[pl.ds(r0, b), hs] = o.astype(o_ref.dtype)
            lse_tile = jnp.where(lane == h, lse, lse_tile)
        lse_ref[pl.ds(r0, b), :] = lse_tile

    rows(0,
         lambda hs: jnp.concatenate([kh_ref[:, hs], k_ref[0:b, hs]], axis=0),
         lambda hs: jnp.concatenate([vh_ref[:, hs], v_ref[0:b, hs]], axis=0),
         band & (cur_half | has_prev_block))

    def body(n, carry):
        r0 = pl.multiple_of(n * b, b)
        p0 = pl.multiple_of((n - 1) * b, b)
        rows(r0, lambda hs: k_ref[pl.ds(p0, 2 * b), hs], lambda hs: v_ref[pl.ds(p0, 2 * b), hs], band)
        return carry

    lax.fori_loop(1, q_ref.shape[0] // b, body, 0)


def _dilated_branch(qkv, dilation):
    s = qkv.shape[0]
    length = s // dilation
    rows = min(DIL_ROWS, length)
    view = qkv.reshape(length, dilation * qkv.shape[1])
    nblk = qkv.shape[1] // DIL_W
    sub = rows // DIL_BLOCK
    main = lambda c: pl.BlockSpec((rows, DIL_W), lambda r, i: (i, r * nblk + c))
    halo = lambda c: pl.BlockSpec((DIL_BLOCK, DIL_W), lambda r, i: (jnp.maximum(i * sub - 1, 0), r * nblk + c))
    o, lse = pl.pallas_call(
        _dil_kernel,
        grid=(dilation, length // rows),
        in_specs=[main(3), main(4), main(5), halo(4), halo(5)],
        out_specs=[
            pl.BlockSpec((rows, DIL_W), lambda r, i: (i, r)),
            pl.BlockSpec((rows, V7X_LANES), lambda r, i: (i, r)),
        ],
        out_shape=[
            jax.ShapeDtypeStruct((length, dilation * DIL_W), BF16),
            jax.ShapeDtypeStruct((length, dilation * V7X_LANES), F32),
        ],
        compiler_params=_params(("parallel", "parallel")),
        name=f"dilated_d{dilation}",
    )(view, view, view, view, view)
    return o.reshape(s, DIL_W), lse.reshape(s, V7X_LANES)


def _attn_out_kernel(x_ref, ya_ref, o1_ref, o2_ref, o3_ref, l1_ref, l2_ref, l3_ref, w_ref, g_ref, b_ref,
                     o_ref, ob_ref):
    l1, l2, l3 = l1_ref[...], l2_ref[...], l3_ref[...]
    mx = jnp.maximum(jnp.maximum(l1, l2), l3)
    e1, e2, e3 = jnp.exp(l1 - mx), jnp.exp(l2 - mx), jnp.exp(l3 - mx)
    den = e1 + e2 + e3
    w1, w2, w3 = e1 / den, e2 / den, e3 / den
    mixed = [ya_ref[...]]
    for h in range(N_DIL_HEADS):
        hs = slice(h * HEAD_DIM, (h + 1) * HEAD_DIM)
        mix = (w1[:, h:h + 1] * o1_ref[:, hs].astype(F32) + w2[:, h:h + 1] * o2_ref[:, hs].astype(F32)
               + w3[:, h:h + 1] * o3_ref[:, hs].astype(F32))
        mixed.append(mix.astype(BF16))
    y = _dot(jnp.concatenate(mixed, axis=1), w_ref[...])
    out = _layer_norm(ALPHA * x_ref[...] + y, g_ref[...], b_ref[...])
    o_ref[...] = out
    ob_ref[...] = out.astype(BF16)


def _attn_out(x, ya, outs, lses, w, g, b):
    s = x.shape[0]
    row = lambda width: pl.BlockSpec((OUT_ROWS, width), lambda i: (i, 0))
    const = lambda shape: pl.BlockSpec(shape, lambda i: (0, 0))
    return pl.pallas_call(
        _attn_out_kernel,
        grid=(s // OUT_ROWS,),
        in_specs=[row(D_MODEL), row(FOX_W), row(DIL_W), row(DIL_W), row(DIL_W),
                  row(V7X_LANES), row(V7X_LANES), row(V7X_LANES),
                  const((FOX_W + DIL_W, D_MODEL)), const((1, D_MODEL)), const((1, D_MODEL))],
        out_specs=[row(D_MODEL), row(D_MODEL)],
        out_shape=[jax.ShapeDtypeStruct((s, D_MODEL), F32), jax.ShapeDtypeStruct((s, D_MODEL), BF16)],
        compiler_params=_params(("parallel",)),
        name="attn_out",
    )(x, ya, *outs, *lses, w, g, b)


def _s5_prep_kernel(lr_ref, li_ref, ldt_ref, br_ref, bi_ref, pr_ref, pi_ref, bbr_ref, bbi_ref):
    lr, li = lr_ref[...], li_ref[...]
    dt = jnp.exp(ldt_ref[...])
    mag = jnp.exp(lr * dt)
    ar, ai = mag * jnp.cos(li * dt), mag * jnp.sin(li * dt)
    den = lr * lr + li * li
    cr = ((ar - 1.0) * lr + ai * li) / den
    ci = (ai * lr - (ar - 1.0) * li) / den
    br, bi = br_ref[...], bi_ref[...]
    bbr_ref[...] = cr * br - ci * bi
    bbi_ref[...] = cr * bi + ci * br
    pr, pi = ar, ai
    pr_ref[0] = pr
    pi_ref[0] = pi
    for k in range(1, V7X_SUBLANES):
        pr, pi = pr * ar - pi * ai, pr * ai + pi * ar
        pr_ref[k] = pr
        pi_ref[k] = pi


def _s5_prep(lam_re, lam_im, log_dt, b_re, b_im):
    rep = lambda v: jnp.repeat(v, S5_GROUP, axis=0)
    rows = S5_GROUPS * S5_GROUP
    bt = lambda v: v.transpose(0, 2, 1).reshape(rows, S5_STATE)
    mat = jax.ShapeDtypeStruct((rows, S5_STATE), F32)
    pw = jax.ShapeDtypeStruct((V7X_SUBLANES, rows, S5_STATE), F32)
    return pl.pallas_call(_s5_prep_kernel, out_shape=[pw, pw, mat, mat], name="s5_prep")(
        rep(lam_re), rep(lam_im), rep(jnp.broadcast_to(log_dt[:, None], lam_re.shape)), bt(b_re), bt(b_im))


def _s5_core_kernel(u_ref, bblk_ref, cblk_ref, pr_ref, pi_ref, d_ref, z_ref, sr_ref, si_ref, cr_ref, ci_ref):
    @pl.when(pl.program_id(0) == 0)
    def _():
        cr_ref[...] = jnp.zeros_like(cr_ref)
        ci_ref[...] = jnp.zeros_like(ci_ref)

    rows = u_ref.shape[0]
    sub = lax.broadcasted_iota(jnp.int32, (V7X_SUBLANES, S5_LANES), 0)
    width = S5_GB * S5_GROUP
    ys = []
    for jb in range(S5_NB):
        ub = u_ref[:, jb * width:(jb + 1) * width].astype(BF16)
        bu = _dot(ub, bblk_ref[jb])
        sr_ref[...] = bu[:, :S5_LANES]
        si_ref[...] = bu[:, S5_LANES:]
        p_re, p_im = pr_ref[jb], pi_ref[jb]
        steps = [(k, p_re[k - 1:k, :], p_im[k - 1:k, :]) for k in (1, 2, 4)]

        def block(n, carry):
            c_re, c_im = carry
            r0 = pl.multiple_of(n * V7X_SUBLANES, V7X_SUBLANES)
            vr, vi = sr_ref[pl.ds(r0, V7X_SUBLANES), :], si_ref[pl.ds(r0, V7X_SUBLANES), :]
            for k, ar, ai in steps:
                tr = jnp.where(sub >= k, pltpu.roll(vr, k, 0), 0.0)
                ti = jnp.where(sub >= k, pltpu.roll(vi, k, 0), 0.0)
                vr, vi = vr + ar * tr - ai * ti, vi + ar * ti + ai * tr
            vr, vi = vr + p_re * c_re - p_im * c_im, vi + p_re * c_im + p_im * c_re
            sr_ref[pl.ds(r0, V7X_SUBLANES), :] = vr
            si_ref[pl.ds(r0, V7X_SUBLANES), :] = vi
            return vr[V7X_SUBLANES - 1:, :], vi[V7X_SUBLANES - 1:, :]

        c_re, c_im = lax.fori_loop(0, rows // V7X_SUBLANES, block,
                                   (cr_ref[jb:jb + 1, :], ci_ref[jb:jb + 1, :]), unroll=2)
        cr_ref[jb:jb + 1, :] = c_re
        ci_ref[jb:jb + 1, :] = c_im
        state = jnp.concatenate([sr_ref[...], si_ref[...]], axis=1).astype(BF16)
        ys.append(_dot(state, cblk_ref[jb]))
    y = jnp.concatenate(ys, axis=1) + d_ref[...] * u_ref[...]
    z = 0.5 * y * (1.0 + jnp.tanh(math.sqrt(2.0 / math.pi) * (y + 0.044715 * (y * y * y))))
    z_ref[...] = z.astype(z_ref.dtype)


def _s5_core(u, bblk, cblk, p_re, p_im, d_row):
    s = u.shape[0]
    const = lambda shape: pl.BlockSpec(shape, lambda t: (0,) * len(shape))
    return pl.pallas_call(
        _s5_core_kernel,
        grid=(s // S5_ROWS,),
        in_specs=[pl.BlockSpec((S5_ROWS, S5_WIDTH), lambda t: (t, 0)),
                  const(bblk.shape), const(cblk.shape), const(p_re.shape), const(p_im.shape),
                  const((1, S5_WIDTH))],
        out_specs=pl.BlockSpec((S5_ROWS, S5_WIDTH), lambda t: (t, 0)),
        out_shape=jax.ShapeDtypeStruct((s, S5_WIDTH), BF16),
        scratch_shapes=[pltpu.VMEM((S5_ROWS, S5_LANES), F32), pltpu.VMEM((S5_ROWS, S5_LANES), F32),
                        pltpu.VMEM((V7X_SUBLANES, S5_LANES), F32), pltpu.VMEM((V7X_SUBLANES, S5_LANES), F32)],
        compiler_params=_params(("arbitrary",)),
        name="s5_core",
    )(u, bblk, cblk, p_re, p_im, d_row)


def _block_diag(blocks):
    nb, gb, r, c = blocks.shape
    eye = jnp.eye(gb, dtype=blocks.dtype)
    return jnp.einsum("ngrc,gk->ngrkc", blocks, eye).reshape(nb, gb * r, gb * c)


def _glu_kernel(x_ref, z_ref, wo_ref, wg_ref, g_ref, b_ref, o_ref, ob_ref):
    z = z_ref[...]
    y = _dot(z, wo_ref[...]) * jax.nn.sigmoid(_dot(z, wg_ref[...]))
    out = _layer_norm(ALPHA * x_ref[...] + y, g_ref[...], b_ref[...])
    o_ref[...] = out
    ob_ref[...] = out.astype(BF16)


def _glu(x, z, wo, wg, g, b):
    s = x.shape[0]
    row = lambda width: pl.BlockSpec((GLU_ROWS, width), lambda i: (i, 0))
    const = lambda shape: pl.BlockSpec(shape, lambda i: (0, 0))
    return pl.pallas_call(
        _glu_kernel,
        grid=(s // GLU_ROWS,),
        in_specs=[row(D_MODEL), row(S5_WIDTH), const((S5_WIDTH, D_MODEL)), const((S5_WIDTH, D_MODEL)),
                  const((1, D_MODEL)), const((1, D_MODEL))],
        out_specs=[row(D_MODEL), row(D_MODEL)],
        out_shape=[jax.ShapeDtypeStruct((s, D_MODEL), F32), jax.ShapeDtypeStruct((s, D_MODEL), BF16)],
        compiler_params=_params(("parallel",)),
        name="glu",
    )(x, z, wo, wg, g, b)


def _attention_mixer(x, xb, w_in, b_f, w_out, g, b):
    scale = HEAD_DIM ** -0.5
    fox_q, fox_k, fox_v = (w_in[:, n * FOX_W:(n + 1) * FOX_W] for n in range(3))
    w_f = w_in[:, 3 * FOX_W:3 * FOX_W + N_FOX_HEADS]
    d0 = 3 * FOX_W + N_FOX_HEADS
    dil_q, dil_k, dil_v = (w_in[:, d0 + n * DIL_W:d0 + (n + 1) * DIL_W] for n in range(3))
    w_cat = jnp.concatenate([fox_q * (scale * LOG2E), fox_k, fox_v, dil_q * scale, dil_k, dil_v],
                            axis=1).astype(BF16)
    w_f = jnp.pad(w_f, ((0, 0), (0, V7X_LANES - N_FOX_HEADS))).astype(BF16)
    qkv, f_logit = _attn_in(xb, w_cat, w_f)
    c_rows = _forget_cumsum(f_logit, jnp.pad(b_f, (0, V7X_LANES - N_FOX_HEADS))[None, :])
    ya = _fox_attention(*_fox_prep(qkv, c_rows))
    branches = [_dilated_branch(qkv, d) for (_, d) in DIL_PAIRS]
    return _attn_out(x, ya, [o for o, _ in branches], [l for _, l in branches], w_out.astype(BF16), g, b)


def _s5_mixer(x, xb, w_in, lam_re, lam_im, log_dt, b_re, b_im, c_re, c_im, d_skip, w_glu_out, w_glu_gate, g, b):
    u = _matmul(xb, w_in.astype(BF16), F32)
    p_re, p_im, bb_re, bb_im = _s5_prep(lam_re, lam_im, log_dt, b_re, b_im)
    lanes = lambda p: p[:, ::S5_GROUP, :].reshape(V7X_SUBLANES, S5_NB, S5_LANES).transpose(1, 0, 2)
    blk = lambda v: v.reshape(S5_NB, S5_GB, S5_GROUP, S5_STATE)
    bblk = jnp.concatenate([_block_diag(blk(bb_re)), _block_diag(blk(bb_im))], axis=2).astype(BF16)
    ct = lambda v: v.transpose(0, 2, 1).reshape(S5_NB, S5_GB, S5_STATE, S5_GROUP)
    cblk = jnp.concatenate([_block_diag(ct(c_re)), -_block_diag(ct(c_im))], axis=1).astype(BF16)
    z = _s5_core(u, bblk, cblk, lanes(p_re), lanes(p_im), d_skip[None, :])
    return _glu(x, z, w_glu_out.astype(BF16), w_glu_gate.astype(BF16), g, b)


def kernel(x, ffn1_w_gate, ffn1_w_up, ffn1_w_down, ffn2_w_gate, ffn2_w_up, ffn2_w_down, ln_gain, ln_bias,
           attn_w_in, attn_b_f, attn_w_out, s5_w_in, s5_lambda_re, s5_lambda_im, s5_log_dt, s5_b_re, s5_b_im,
           s5_c_re, s5_c_im, s5_d, s5_w_glu_out, s5_w_glu_gate):
    batch, seq, _ = x.shape
    assert batch == 1
    h = x.reshape(seq, D_MODEL)
    hb = h.astype(BF16)
    ln = lambda i, k: (ln_gain[i, k][None, :], ln_bias[i, k][None, :])
    for i in range(DEPTH):
        h, hb = _ffn(h, hb, ffn1_w_gate[i].astype(BF16), ffn1_w_up[i].astype(BF16),
                     ffn1_w_down[i].astype(BF16), *ln(i, 0))
        j = i // 2
        if i % 2 == 0:
            h, hb = _attention_mixer(h, hb, attn_w_in[j], attn_b_f[j], attn_w_out[j], *ln(i, 1))
        else:
            h, hb = _s5_mixer(h, hb, s5_w_in[j], s5_lambda_re[j], s5_lambda_im[j], s5_log_dt[j], s5_b_re[j],
                              s5_b_im[j], s5_c_re[j], s5_c_im[j], s5_d[j], s5_w_glu_out[j], s5_w_glu_gate[j],
                              *ln(i, 1))
        h, hb = _ffn(h, hb, ffn2_w_gate[i].astype(BF16), ffn2_w_up[i].astype(BF16),
                     ffn2_w_down[i].astype(BF16), *ln(i, 2))
    return h.reshape(batch, seq, D_MODEL)
```

```python
import math

import jax
import jax.numpy as jnp
from jax import lax
from jax.experimental import pallas as pl
from jax.experimental.pallas import tpu as pltpu

D_MODEL = 2048
DEPTH = 2
HEAD_DIM = 128
N_FOX_HEADS = 8
N_DIL_HEADS = 8
DIL_PAIRS = ((128, 1), (512, 4), (2048, 16))
DIL_BLOCK = 128
D_FF = 5632
S5_GROUP = 16
S5_WIDTH = 1024
S5_GROUPS = S5_WIDTH // S5_GROUP
S5_STATE = 64
ALPHA = (2.0 * DEPTH) ** 0.25
LN_EPS = 1e-5
FOX_W = N_FOX_HEADS * HEAD_DIM
DIL_W = N_DIL_HEADS * HEAD_DIM

F32 = jnp.float32
BF16 = jnp.bfloat16
NEG = -1e30
LOG2E = math.log2(math.e)
FOX_AUG = 128
FOX_VROWS = HEAD_DIM + 16

V7X_LANES = 128
V7X_SUBLANES = 8
V7X_VMEM_LIMIT_BYTES = 56 * 1024 * 1024

FFN_ROWS = 512
FFN_COLS = 512
MM_ROWS = 1024
MM_COLS = 1024
FOX_BLOCK = 512
DIL_TILE = 2048
OUT_ROWS = 256
GLU_ROWS = 512
CUMSUM_ROWS = 512
S5_ROWS = 256
S5_SEG = S5_ROWS // V7X_SUBLANES
S5_GB = 16
S5_NB = S5_GROUPS // S5_GB
S5_LANES = S5_GB * S5_STATE


def _params(sem):
    return pltpu.CompilerParams(dimension_semantics=sem, vmem_limit_bytes=V7X_VMEM_LIMIT_BYTES)


def _layer_norm(v, g, b):
    mu = jnp.mean(v, axis=-1, keepdims=True)
    d = v - mu
    var = jnp.mean(d * d, axis=-1, keepdims=True)
    return d * lax.rsqrt(var + LN_EPS) * g + b


def _dot(a, b):
    return jnp.dot(a, b, preferred_element_type=F32)


def _dot_nt(a, b):
    return lax.dot_general(a, b, (((1,), (1,)), ((), ())), preferred_element_type=F32)


def _ffn_kernel(x_ref, xb_ref, wg_ref, wu_ref, wd_ref, g_ref, b_ref, o_ref, ob_ref, h_ref):
    j = pl.program_id(1)
    last = pl.num_programs(1) - 1

    def hidden():
        xb = xb_ref[...]
        gate = _dot(xb, wg_ref[...])
        up = _dot(xb, wu_ref[...])
        return (gate * jax.nn.sigmoid(gate) * up).astype(BF16)

    def down(slot):
        return _dot(h_ref[slot], wd_ref[...])

    @pl.when(j == 0)
    def _():
        h_ref[0] = hidden()

    @pl.when(j == 1)
    def _():
        o_ref[...] = down(0)
        h_ref[1] = hidden()

    @pl.when((j > 1) & (j < last))
    def _():
        o_ref[...] += down((j - 1) % 2)
        h_ref[j % 2] = hidden()

    @pl.when(j == last)
    def _():
        y = _layer_norm(ALPHA * x_ref[...] + 0.5 * (o_ref[...] + down((j - 1) % 2)), g_ref[...], b_ref[...])
        o_ref[...] = y
        ob_ref[...] = y.astype(BF16)


def _ffn(x, xb, wg, wu, wd, g, b):
    s = x.shape[0]
    chunks = D_FF // FFN_COLS
    row = lambda i, j: (i, 0)
    up_map = lambda i, j: (0, jnp.minimum(j, chunks - 1))
    return pl.pallas_call(
        _ffn_kernel,
        grid=(s // FFN_ROWS, chunks + 1),
        in_specs=[
            pl.BlockSpec((FFN_ROWS, D_MODEL), row),
            pl.BlockSpec((FFN_ROWS, D_MODEL), row),
            pl.BlockSpec((D_MODEL, FFN_COLS), up_map),
            pl.BlockSpec((D_MODEL, FFN_COLS), up_map),
            pl.BlockSpec((FFN_COLS, D_MODEL), lambda i, j: (jnp.maximum(j - 1, 0), 0)),
            pl.BlockSpec((1, D_MODEL), lambda i, j: (0, 0)),
            pl.BlockSpec((1, D_MODEL), lambda i, j: (0, 0)),
        ],
        out_specs=[pl.BlockSpec((FFN_ROWS, D_MODEL), row), pl.BlockSpec((FFN_ROWS, D_MODEL), row)],
        out_shape=[jax.ShapeDtypeStruct((s, D_MODEL), F32), jax.ShapeDtypeStruct((s, D_MODEL), BF16)],
        scratch_shapes=[pltpu.VMEM((2, FFN_ROWS, FFN_COLS), BF16)],
        compiler_params=_params(("parallel", "arbitrary")),
        name="ffn",
    )(x, xb, wg, wu, wd, g, b)


def _attn_in_kernel(x_ref, w_ref, wf_ref, o_ref, f_ref):
    x = x_ref[...]
    o_ref[...] = _dot(x, w_ref[...]).astype(o_ref.dtype)

    @pl.when(pl.program_id(1) == 0)
    def _():
        f_ref[...] = _dot(x, wf_ref[...])


def _attn_in(xb, w, wf):
    s, n = xb.shape[0], w.shape[1]
    return pl.pallas_call(
        _attn_in_kernel,
        grid=(s // MM_ROWS, n // MM_COLS),
        in_specs=[
            pl.BlockSpec((MM_ROWS, D_MODEL), lambda i, j: (i, 0)),
            pl.BlockSpec((D_MODEL, MM_COLS), lambda i, j: (0, j)),
            pl.BlockSpec((D_MODEL, V7X_LANES), lambda i, j: (0, 0)),
        ],
        out_specs=[
            pl.BlockSpec((MM_ROWS, MM_COLS), lambda i, j: (i, j)),
            pl.BlockSpec((MM_ROWS, V7X_LANES), lambda i, j: (i, 0)),
        ],
        out_shape=[jax.ShapeDtypeStruct((s, n), BF16), jax.ShapeDtypeStruct((s, V7X_LANES), F32)],
        compiler_params=_params(("parallel", "arbitrary")),
        name="attn_in",
    )(xb, w, wf)


def _mm_kernel(x_ref, w_ref, o_ref):
    o_ref[...] = _dot(x_ref[...], w_ref[...]).astype(o_ref.dtype)


def _matmul(xb, w, out_dtype):
    s, n = xb.shape[0], w.shape[1]
    return pl.pallas_call(
        _mm_kernel,
        grid=(s // MM_ROWS, n // MM_COLS),
        in_specs=[
            pl.BlockSpec((MM_ROWS, xb.shape[1]), lambda i, j: (i, 0)),
            pl.BlockSpec((xb.shape[1], MM_COLS), lambda i, j: (0, j)),
        ],
        out_specs=pl.BlockSpec((MM_ROWS, MM_COLS), lambda i, j: (i, j)),
        out_shape=jax.ShapeDtypeStruct((s, n), out_dtype),
        compiler_params=_params(("parallel", "parallel")),
        name="matmul",
    )(xb, w)


def _split3(v):
    hi = v.astype(BF16)
    r1 = v - hi.astype(F32)
    mid = r1.astype(BF16)
    lo = (r1 - mid.astype(F32)).astype(BF16)
    return hi, mid, lo


def _cumsum_kernel(f_ref, bf_ref, c_ref, carry_ref):
    @pl.when(pl.program_id(0) == 0)
    def _():
        carry_ref[...] = jnp.zeros_like(carry_ref)

    z = f_ref[...] + bf_ref[...]
    log_f = jnp.minimum(z, 0.0) - jnp.log1p(jnp.exp(-jnp.abs(z)))
    n = z.shape[0]
    tri = (lax.broadcasted_iota(jnp.int32, (n, n), 0) >= lax.broadcasted_iota(jnp.int32, (n, n), 1)).astype(BF16)
    hi, mid, lo = _split3(log_f)
    c = _dot(tri, hi) + _dot(tri, mid) + _dot(tri, lo) + carry_ref[0:1, :]
    c_ref[...] = c
    carry_ref[...] = jnp.broadcast_to(c[n - 1:n, :], carry_ref.shape)


def _forget_cumsum(f_logit, b_f_row):
    s = f_logit.shape[0]
    return pl.pallas_call(
        _cumsum_kernel,
        grid=(s // CUMSUM_ROWS,),
        in_specs=[
            pl.BlockSpec((CUMSUM_ROWS, V7X_LANES), lambda i: (i, 0)),
            pl.BlockSpec((1, V7X_LANES), lambda i: (0, 0)),
        ],
        out_specs=pl.BlockSpec((CUMSUM_ROWS, V7X_LANES), lambda i: (i, 0)),
        out_shape=jax.ShapeDtypeStruct((s, V7X_LANES), F32),
        scratch_shapes=[pltpu.VMEM((V7X_SUBLANES, V7X_LANES), F32)],
        compiler_params=_params(("arbitrary",)),
        name="forget_cumsum",
    )(f_logit, b_f_row)


def _fox_prep_kernel(qkv_ref, c_ref, qt_ref, ka_ref, vt_ref):
    rows = qkv_ref.shape[0]
    lane = lax.broadcasted_iota(jnp.int32, (rows, V7X_LANES), 1)
    bias = c_ref[...] * (-LOG2E)
    ones_rows = (lax.broadcasted_iota(jnp.int32, (FOX_AUG, rows), 0) < 3).astype(BF16)
    sum_rows = (lax.broadcasted_iota(jnp.int32, (FOX_VROWS - HEAD_DIM, rows), 0) < 1).astype(BF16)
    for h in range(N_FOX_HEADS):
        hs = slice(h * HEAD_DIM, (h + 1) * HEAD_DIM)
        a0 = h * (HEAD_DIM + FOX_AUG)
        qt_ref[a0:a0 + HEAD_DIM, :] = qkv_ref[:, hs].astype(F32).T.astype(BF16)
        qt_ref[a0 + HEAD_DIM:a0 + HEAD_DIM + FOX_AUG, :] = ones_rows
        v0 = h * FOX_VROWS
        vt_ref[v0:v0 + HEAD_DIM, :] = (
            qkv_ref[:, 2 * FOX_W + h * HEAD_DIM:2 * FOX_W + (h + 1) * HEAD_DIM].astype(F32).T.astype(BF16))
        vt_ref[v0 + HEAD_DIM:v0 + FOX_VROWS, :] = sum_rows
        hi, mid, lo = _split3(jnp.broadcast_to(bias[:, h:h + 1], (rows, V7X_LANES)))
        aug = jnp.where(lane == 0, hi.astype(F32), jnp.where(lane == 1, mid.astype(F32),
                                                              jnp.where(lane == 2, lo.astype(F32), 0.0)))
        ka_ref[:, a0:a0 + HEAD_DIM] = qkv_ref[:, FOX_W + h * HEAD_DIM:FOX_W + (h + 1) * HEAD_DIM]
        ka_ref[:, a0 + HEAD_DIM:a0 + HEAD_DIM + FOX_AUG] = aug.astype(BF16)


def _fox_prep(qkv, c_rows):
    s = qkv.shape[0]
    wide = N_FOX_HEADS * (HEAD_DIM + FOX_AUG)
    vrows = N_FOX_HEADS * FOX_VROWS
    return pl.pallas_call(
        _fox_prep_kernel,
        grid=(s // FOX_BLOCK,),
        in_specs=[pl.BlockSpec((FOX_BLOCK, 3 * FOX_W), lambda i: (i, 0)),
                  pl.BlockSpec((FOX_BLOCK, V7X_LANES), lambda i: (i, 0))],
        out_specs=[pl.BlockSpec((wide, FOX_BLOCK), lambda i: (0, i)),
                   pl.BlockSpec((FOX_BLOCK, wide), lambda i: (i, 0)),
                   pl.BlockSpec((vrows, FOX_BLOCK), lambda i: (0, i))],
        out_shape=[jax.ShapeDtypeStruct((wide, s), BF16), jax.ShapeDtypeStruct((s, wide), BF16),
                   jax.ShapeDtypeStruct((vrows, s), BF16)],
        compiler_params=_params(("parallel",)),
        name="fox_prep",
    )(qkv, c_rows)


def _fox_kernel(qi_ref, kj_ref, qt_ref, ka_ref, vt_ref, o_ref, m_ref, acc_ref, s_ref):
    t = pl.program_id(0)
    i = qi_ref[t]
    j = kj_ref[t]
    blk = ka_ref.shape[0]
    aw = HEAD_DIM + FOX_AUG

    @pl.when(j == 0)
    def _():
        m_ref[...] = jnp.full_like(m_ref, NEG)
        acc_ref[...] = jnp.zeros_like(acc_ref)

    def scores(h):
        return _dot(ka_ref[:, h * aw:(h + 1) * aw], qt_ref[h * aw:(h + 1) * aw, :])

    def step(diagonal):
        if diagonal:
            keep = lax.broadcasted_iota(jnp.int32, (blk, blk), 0) <= lax.broadcasted_iota(jnp.int32, (blk, blk), 1)
        s_ref[0] = scores(0)
        for h in range(N_FOX_HEADS):
            if h + 1 < N_FOX_HEADS:
                s_ref[(h + 1) % 2] = scores(h + 1)
            s = s_ref[h % 2]
            if diagonal:
                s = jnp.where(keep, s, NEG)
            m_prev = m_ref[h:h + 1, :]
            m_new = jnp.maximum(m_prev, jnp.max(s, axis=0, keepdims=True))
            p = jnp.exp2(s - m_new).astype(BF16)
            alpha = jnp.exp2(m_prev - m_new)
            m_ref[h:h + 1, :] = m_new
            vs = slice(h * FOX_VROWS, (h + 1) * FOX_VROWS)
            acc_ref[vs, :] = alpha * acc_ref[vs, :] + _dot(vt_ref[vs, :], p)

    @pl.when(j < i)
    def _():
        step(False)

    @pl.when(j == i)
    def _():
        step(True)
        for h in range(N_FOX_HEADS):
            v0 = h * FOX_VROWS
            out_t = acc_ref[v0:v0 + HEAD_DIM, :] / acc_ref[v0 + HEAD_DIM:v0 + HEAD_DIM + 1, :]
            o_ref[:, h * HEAD_DIM:(h + 1) * HEAD_DIM] = out_t.T.astype(o_ref.dtype)


def _fox_attention(qt, ka, vt):
    s = ka.shape[0]
    nb = s // FOX_BLOCK
    qi = jnp.asarray([i for i in range(nb) for _ in range(i + 1)], jnp.int32)
    kj = jnp.asarray([j for i in range(nb) for j in range(i + 1)], jnp.int32)
    wide = ka.shape[1]
    vrows = vt.shape[0]
    grid_spec = pltpu.PrefetchScalarGridSpec(
        num_scalar_prefetch=2,
        grid=(qi.shape[0],),
        in_specs=[
            pl.BlockSpec((wide, FOX_BLOCK), lambda t, qi, kj: (0, qi[t])),
            pl.BlockSpec((FOX_BLOCK, wide), lambda t, qi, kj: (kj[t], 0)),
            pl.BlockSpec((vrows, FOX_BLOCK), lambda t, qi, kj: (0, kj[t])),
        ],
        out_specs=pl.BlockSpec((FOX_BLOCK, FOX_W), lambda t, qi, kj: (qi[t], 0)),
        scratch_shapes=[
            pltpu.VMEM((N_FOX_HEADS, FOX_BLOCK), F32),
            pltpu.VMEM((vrows, FOX_BLOCK), F32),
            pltpu.VMEM((2, FOX_BLOCK, FOX_BLOCK), F32),
        ],
    )
    return pl.pallas_call(
        _fox_kernel,
        grid_spec=grid_spec,
        out_shape=jax.ShapeDtypeStruct((s, FOX_W), BF16),
        compiler_params=_params(("arbitrary",)),
        name="fox_attention",
    )(qi, kj, qt, ka, vt)


def _dil_kernel(q_ref, k_ref, v_ref, kp_ref, vp_ref, y_ref, o_scr, l_scr):
    has_prev_tile = pl.program_id(0) > 0
    b = DIL_BLOCK
    tile = q_ref.shape[0]
    jr = lax.broadcasted_iota(jnp.int32, (b, 2 * b), 0)
    mc = lax.broadcasted_iota(jnp.int32, (b, 2 * b), 1)
    band = (mc >= jr) & (mc <= jr + b)
    band_first = band & ((mc >= b) | has_prev_tile)

    def rows(ref, start, d):
        return ref[pl.ds(start, b), :] if d == 1 else ref[pl.ds(start, b, stride=d), :]

    for g, (_, d) in enumerate(DIL_PAIRS):
        for r in range(d):
            for n in range(tile // (b * d)):
                cur = r + n * b * d
                if n > 0:
                    k_prev, v_prev, keep = rows(k_ref, cur - b * d, d), rows(v_ref, cur - b * d, d), band
                else:
                    prev = tile - b * d + r
                    k_prev, v_prev, keep = rows(kp_ref, prev, d), rows(vp_ref, prev, d), band_first
                k2 = jnp.concatenate([k_prev, rows(k_ref, cur, d)], axis=0).astype(BF16)
                v2 = jnp.concatenate([v_prev, rows(v_ref, cur, d)], axis=0).astype(BF16)
                s = jnp.where(keep, _dot_nt(rows(q_ref, cur, d).astype(BF16), k2), NEG)
                m = jnp.max(s, axis=1, keepdims=True)
                p = jnp.exp(s - m)
                l = jnp.sum(p, axis=1, keepdims=True)
                o = _dot(p.astype(BF16), v2) / l
                lse = jnp.broadcast_to(m + jnp.log(l), (b, HEAD_DIM))
                if d == 1:
                    o_scr[g, pl.ds(cur, b), :] = o
                    l_scr[g, pl.ds(cur, b), :] = lse
                else:
                    o_scr[g, pl.ds(cur, b, stride=d), :] = o
                    l_scr[g, pl.ds(cur, b, stride=d), :] = lse

    l1, l2, l3 = l_scr[0], l_scr[1], l_scr[2]
    mx = jnp.maximum(jnp.maximum(l1, l2), l3)
    e1, e2, e3 = jnp.exp(l1 - mx), jnp.exp(l2 - mx), jnp.exp(l3 - mx)
    den = e1 + e2 + e3
    y_ref[...] = ((e1 / den) * o_scr[0] + (e2 / den) * o_scr[1] + (e3 / den) * o_scr[2]).astype(y_ref.dtype)


def _dilated_attention(qkv):
    s = qkv.shape[0]
    cur = lambda c: pl.BlockSpec((DIL_TILE, HEAD_DIM), lambda t, h: (t, c * N_DIL_HEADS + h))
    prev = lambda c: pl.BlockSpec((DIL_TILE, HEAD_DIM), lambda t, h: (jnp.maximum(t - 1, 0), c * N_DIL_HEADS + h))
    return pl.pallas_call(
        _dil_kernel,
        grid=(s // DIL_TILE, N_DIL_HEADS),
        in_specs=[cur(0), cur(1), cur(2), prev(1), prev(2)],
        out_specs=pl.BlockSpec((DIL_TILE, HEAD_DIM), lambda t, h: (t, h)),
        out_shape=jax.ShapeDtypeStruct((s, DIL_W), BF16),
        scratch_shapes=[pltpu.VMEM((len(DIL_PAIRS), DIL_TILE, HEAD_DIM), F32),
                        pltpu.VMEM((len(DIL_PAIRS), DIL_TILE, HEAD_DIM), F32)],
        compiler_params=_params(("parallel", "parallel")),
        name="dilated_attention",
    )(qkv, qkv, qkv, qkv, qkv)


def _attn_out_kernel(x_ref, ya_ref, yb_ref, w_ref, g_ref, b_ref, o_ref, ob_ref):
    y = _dot(jnp.concatenate([ya_ref[...], yb_ref[...]], axis=1), w_ref[...])
    out = _layer_norm(ALPHA * x_ref[...] + y, g_ref[...], b_ref[...])
    o_ref[...] = out
    ob_ref[...] = out.astype(BF16)


def _attn_out(x, ya, yb, w, g, b):
    s = x.shape[0]
    row = lambda width: pl.BlockSpec((OUT_ROWS, width), lambda i: (i, 0))
    const = lambda shape: pl.BlockSpec(shape, lambda i: (0, 0))
    return pl.pallas_call(
        _attn_out_kernel,
        grid=(s // OUT_ROWS,),
        in_specs=[row(D_MODEL), row(FOX_W), row(DIL_W),
                  const((FOX_W + DIL_W, D_MODEL)), const((1, D_MODEL)), const((1, D_MODEL))],
        out_specs=[row(D_MODEL), row(D_MODEL)],
        out_shape=[jax.ShapeDtypeStruct((s, D_MODEL), F32), jax.ShapeDtypeStruct((s, D_MODEL), BF16)],
        compiler_params=_params(("parallel",)),
        name="attn_out",
    )(x, ya, yb, w, g, b)


def _cmul(ar, ai, br, bi):
    return ar * br - ai * bi, ar * bi + ai * br


def _s5_prep_kernel(lrg_ref, lig_ref, ldtg_ref, lr_ref, li_ref, ldt_ref, br_ref, bi_ref,
                    pr_ref, pi_ref, qr_ref, qi_ref, bbr_ref, bbi_ref):
    def a_bar(lr, li, ldt):
        dt = jnp.exp(ldt)
        mag = jnp.exp(lr * dt)
        return mag * jnp.cos(li * dt), mag * jnp.sin(li * dt)

    lr, li = lr_ref[...], li_ref[...]
    ar, ai = a_bar(lr, li, ldt_ref[...])
    den = lr * lr + li * li
    cr = ((ar - 1.0) * lr + ai * li) / den
    ci = (ai * lr - (ar - 1.0) * li) / den
    bbr_ref[...], bbi_ref[...] = _cmul(cr, ci, br_ref[...], bi_ref[...])

    ar, ai = a_bar(lrg_ref[...], lig_ref[...], ldtg_ref[...])
    pr, pi = ar, ai
    pr_ref[0], pi_ref[0] = pr, pi
    for k in range(1, S5_SEG):
        pr, pi = _cmul(pr, pi, ar, ai)
        pr_ref[k], pi_ref[k] = pr, pi
    sr, si = pr, pi
    qr_ref[0], qi_ref[0] = sr, si
    for k in range(1, V7X_SUBLANES):
        pr, pi = _cmul(pr, pi, sr, si)
        qr_ref[k], qi_ref[k] = pr, pi


def _s5_prep(lam_re, lam_im, log_dt, b_re, b_im):
    ldt = jnp.broadcast_to(log_dt[:, None], lam_re.shape)
    rep = lambda v: jnp.repeat(v, S5_GROUP, axis=0)
    rows = S5_GROUPS * S5_GROUP
    bt = lambda v: v.transpose(0, 2, 1).reshape(rows, S5_STATE)
    mat = jax.ShapeDtypeStruct((rows, S5_STATE), F32)
    pw = jax.ShapeDtypeStruct((S5_SEG, S5_GROUPS, S5_STATE), F32)
    qw = jax.ShapeDtypeStruct((V7X_SUBLANES, S5_GROUPS, S5_STATE), F32)
    return pl.pallas_call(_s5_prep_kernel, out_shape=[pw, pw, qw, qw, mat, mat], name="s5_prep")(
        lam_re, lam_im, ldt, rep(lam_re), rep(lam_im), rep(ldt), bt(b_re), bt(b_im))


def _s5_core_kernel(u_ref, bblk_ref, cblk_ref, pr_ref, pi_ref, qr_ref, qi_ref, d_ref, z_ref,
                    up_ref, sr_ref, si_ref, cr_ref, ci_ref):
    @pl.when(pl.program_id(0) == 0)
    def _():
        cr_ref[...] = jnp.zeros_like(cr_ref)
        ci_ref[...] = jnp.zeros_like(ci_ref)

    sl = V7X_SUBLANES
    rows = u_ref.shape[0]
    ri = lax.broadcasted_iota(jnp.int32, (rows, rows), 0)
    ci = lax.broadcasted_iota(jnp.int32, (rows, rows), 1)
    perm = ((ri // sl == ci % S5_SEG) & (ri % sl == ci // S5_SEG)).astype(BF16)
    unperm = ((ci // sl == ri % S5_SEG) & (ci % sl == ri // S5_SEG)).astype(BF16)
    hi, mid, lo = _split3(u_ref[...])
    up_ref[...] = _dot(perm, hi) + _dot(perm, mid) + _dot(perm, lo)

    sub = lax.broadcasted_iota(jnp.int32, (sl, S5_LANES), 0)
    width = S5_GB * S5_GROUP
    ys = []
    for jb in range(S5_NB):
        bu = _dot(up_ref[:, jb * width:(jb + 1) * width].astype(BF16), bblk_ref[jb])
        a_re, a_im = pr_ref[jb, 0:1, :], pi_ref[jb, 0:1, :]
        xr, xi = bu[0:sl, :S5_LANES], bu[0:sl, S5_LANES:]
        sr_ref[0:sl, :], si_ref[0:sl, :] = xr, xi
        for j in range(1, S5_SEG):
            tr, ti = _cmul(a_re, a_im, xr, xi)
            xr, xi = tr + bu[j * sl:(j + 1) * sl, :S5_LANES], ti + bu[j * sl:(j + 1) * sl, S5_LANES:]
            sr_ref[j * sl:(j + 1) * sl, :], si_ref[j * sl:(j + 1) * sl, :] = xr, xi

        q_re, q_im = qr_ref[jb], qi_ref[jb]
        er, ei = xr, xi
        for k in (1, 2, 4):
            tr = jnp.where(sub >= k, pltpu.roll(er, k, 0), 0.0)
            ti = jnp.where(sub >= k, pltpu.roll(ei, k, 0), 0.0)
            mr, mi = _cmul(q_re[k - 1:k, :], q_im[k - 1:k, :], tr, ti)
            er, ei = er + mr, ei + mi
        in_re, in_im = cr_ref[jb:jb + 1, :], ci_ref[jb:jb + 1, :]
        mr, mi = _cmul(q_re, q_im, in_re, in_im)
        er, ei = er + mr, ei + mi
        cr_ref[jb:jb + 1, :], ci_ref[jb:jb + 1, :] = er[sl - 1:, :], ei[sl - 1:, :]
        cin_re = jnp.where(sub >= 1, pltpu.roll(er, 1, 0), in_re)
        cin_im = jnp.where(sub >= 1, pltpu.roll(ei, 1, 0), in_im)
        fr, fi = _cmul(pr_ref[jb][:, None, :], pi_ref[jb][:, None, :], cin_re[None], cin_im[None])
        shape3 = (S5_SEG, sl, S5_LANES)
        state = jnp.concatenate([(sr_ref[...].reshape(shape3) + fr).reshape(-1, S5_LANES),
                                 (si_ref[...].reshape(shape3) + fi).reshape(-1, S5_LANES)], axis=1)
        ys.append(_dot(state.astype(BF16), cblk_ref[jb]))
    y = jnp.concatenate(ys, axis=1) + d_ref[...] * up_ref[...]
    z = 0.5 * y * (1.0 + jnp.tanh(math.sqrt(2.0 / math.pi) * (y + 0.044715 * (y * y * y))))
    z_ref[...] = _dot(unperm, z.astype(BF16)).astype(z_ref.dtype)


def _s5_core(u, bblk, cblk, p_re, p_im, q_re, q_im, d_row):
    s = u.shape[0]
    const = lambda shape: pl.BlockSpec(shape, lambda t: (0,) * len(shape))
    return pl.pallas_call(
        _s5_core_kernel,
        grid=(s // S5_ROWS,),
        in_specs=[pl.BlockSpec((S5_ROWS, S5_WIDTH), lambda t: (t, 0)),
                  const(bblk.shape), const(cblk.shape), const(p_re.shape), const(p_im.shape),
                  const(q_re.shape), const(q_im.shape), const((1, S5_WIDTH))],
        out_specs=pl.BlockSpec((S5_ROWS, S5_WIDTH), lambda t: (t, 0)),
        out_shape=jax.ShapeDtypeStruct((s, S5_WIDTH), BF16),
        scratch_shapes=[pltpu.VMEM((S5_ROWS, S5_WIDTH), F32),
                        pltpu.VMEM((S5_ROWS, S5_LANES), F32), pltpu.VMEM((S5_ROWS, S5_LANES), F32),
                        pltpu.VMEM((V7X_SUBLANES, S5_LANES), F32), pltpu.VMEM((V7X_SUBLANES, S5_LANES), F32)],
        compiler_params=_params(("arbitrary",)),
        name="s5_core",
    )(u, bblk, cblk, p_re, p_im, q_re, q_im, d_row)


def _block_diag(blocks):
    nb, gb, r, c = blocks.shape
    eye = jnp.eye(gb, dtype=blocks.dtype)
    return jnp.einsum("ngrc,gk->ngrkc", blocks, eye).reshape(nb, gb * r, gb * c)


def _glu_kernel(x_ref, z_ref, wo_ref, wg_ref, g_ref, b_ref, o_ref, ob_ref):
    z = z_ref[...]
    y = _dot(z, wo_ref[...]) * jax.nn.sigmoid(_dot(z, wg_ref[...]))
    out = _layer_norm(ALPHA * x_ref[...] + y, g_ref[...], b_ref[...])
    o_ref[...] = out
    ob_ref[...] = out.astype(BF16)


def _glu(x, z, wo, wg, g, b):
    s = x.shape[0]
    row = lambda width: pl.BlockSpec((GLU_ROWS, width), lambda i: (i, 0))
    const = lambda shape: pl.BlockSpec(shape, lambda i: (0, 0))
    return pl.pallas_call(
        _glu_kernel,
        grid=(s // GLU_ROWS,),
        in_specs=[row(D_MODEL), row(S5_WIDTH), const((S5_WIDTH, D_MODEL)), const((S5_WIDTH, D_MODEL)),
                  const((1, D_MODEL)), const((1, D_MODEL))],
        out_specs=[row(D_MODEL), row(D_MODEL)],
        out_shape=[jax.ShapeDtypeStruct((s, D_MODEL), F32), jax.ShapeDtypeStruct((s, D_MODEL), BF16)],
        compiler_params=_params(("parallel",)),
        name="glu",
    )(x, z, wo, wg, g, b)


def _attention_mixer(x, xb, w_in, b_f, w_out, g, b):
    scale = HEAD_DIM ** -0.5
    fox_q, fox_k, fox_v = (w_in[:, n * FOX_W:(n + 1) * FOX_W] for n in range(3))
    w_f = w_in[:, 3 * FOX_W:3 * FOX_W + N_FOX_HEADS]
    d0 = 3 * FOX_W + N_FOX_HEADS
    dil_q, dil_k, dil_v = (w_in[:, d0 + n * DIL_W:d0 + (n + 1) * DIL_W] for n in range(3))
    w_fox = jnp.concatenate([fox_q * (scale * LOG2E), fox_k, fox_v], axis=1).astype(BF16)
    w_dil = jnp.concatenate([dil_q * scale, dil_k, dil_v], axis=1).astype(BF16)
    w_f = jnp.pad(w_f, ((0, 0), (0, V7X_LANES - N_FOX_HEADS))).astype(BF16)
    qkv_fox, f_logit = _attn_in(xb, w_fox, w_f)
    c_rows = _forget_cumsum(f_logit, jnp.pad(b_f, (0, V7X_LANES - N_FOX_HEADS))[None, :])
    ya = _fox_attention(*_fox_prep(qkv_fox, c_rows))
    yb = _dilated_attention(_matmul(xb, w_dil, F32))
    return _attn_out(x, ya, yb, w_out.astype(BF16), g, b)


def _s5_mixer(x, xb, w_in, lam_re, lam_im, log_dt, b_re, b_im, c_re, c_im, d_skip, w_glu_out, w_glu_gate, g, b):
    u = _matmul(xb, w_in.astype(BF16), F32)
    p_re, p_im, q_re, q_im, bb_re, bb_im = _s5_prep(lam_re, lam_im, log_dt, b_re, b_im)
    lanes = lambda p: p.reshape(p.shape[0], S5_NB, S5_LANES).transpose(1, 0, 2)
    blk = lambda v: v.reshape(S5_NB, S5_GB, S5_GROUP, S5_STATE)
    bblk = jnp.concatenate([_block_diag(blk(bb_re)), _block_diag(blk(bb_im))], axis=2).astype(BF16)
    ct = lambda v: v.transpose(0, 2, 1).reshape(S5_NB, S5_GB, S5_STATE, S5_GROUP)
    cblk = jnp.concatenate([_block_diag(ct(c_re)), -_block_diag(ct(c_im))], axis=1).astype(BF16)
    z = _s5_core(u, bblk, cblk, lanes(p_re), lanes(p_im), lanes(q_re), lanes(q_im), d_skip[None, :])
    return _glu(x, z, w_glu_out.astype(BF16), w_glu_gate.astype(BF16), g, b)


def kernel(x, ffn1_w_gate, ffn1_w_up, ffn1_w_down, ffn2_w_gate, ffn2_w_up, ffn2_w_down, ln_gain, ln_bias,
           attn_w_in, attn_b_f, attn_w_out, s5_w_in, s5_lambda_re, s5_lambda_im, s5_log_dt, s5_b_re, s5_b_im,
           s5_c_re, s5_c_im, s5_d, s5_w_glu_out, s5_w_glu_gate):
    batch, seq, _ = x.shape
    assert batch == 1
    h = x.reshape(seq, D_MODEL)
    hb = h.astype(BF16)
    ln = lambda i, k: (ln_gain[i, k][None, :], ln_bias[i, k][None, :])
    for i in range(DEPTH):
        h, hb = _ffn(h, hb, ffn1_w_gate[i].astype(BF16), ffn1_w_up[i].astype(BF16),
                     ffn1_w_down[i].astype(BF16), *ln(i, 0))
        j = i // 2
        if i % 2 == 0:
            h, hb = _attention_mixer(h, hb, attn_w_in[j], attn_b_f[j], attn_w_out[j], *ln(i, 1))
        else:
            h, hb = _s5_mixer(h, hb, s5_w_in[j], s5_lambda_re[j], s5_lambda_im[j], s5_log_dt[j], s5_b_re[j],
                              s5_b_im[j], s5_c_re[j], s5_c_im[j], s5_d[j], s5_w_glu_out[j], s5_w_glu_gate[j],
                              *ln(i, 1))
        h, hb = _ffn(h, hb, ffn2_w_gate[i].astype(BF16), ffn2_w_up[i].astype(BF16),
                     ffn2_w_down[i].astype(BF16), *ln(i, 2))
    return h.reshape(batch, seq, D_MODEL)
```

```python
import math

import jax
import jax.numpy as jnp
from jax import lax
from jax.experimental import pallas as pl
from jax.experimental.pallas import tpu as pltpu

D_MODEL = 2048
DEPTH = 2
HEAD_DIM = 128
N_FOX_HEADS = 8
N_DIL_HEADS = 8
DIL_PAIRS = ((128, 1), (512, 4), (2048, 16))
DIL_BLOCK = 128
D_FF = 5632
S5_GROUP = 16
S5_WIDTH = 1024
S5_GROUPS = S5_WIDTH // S5_GROUP
S5_STATE = 64
ALPHA = (2.0 * DEPTH) ** 0.25
LN_EPS = 1e-5
FOX_W = N_FOX_HEADS * HEAD_DIM
DIL_W = N_DIL_HEADS * HEAD_DIM

F32 = jnp.float32
BF16 = jnp.bfloat16
NEG = -1e30
LOG2E = math.log2(math.e)
FOX_AUG = 128
FOX_VROWS = HEAD_DIM + 16

V7X_LANES = 128
V7X_SUBLANES = 8
V7X_VMEM_LIMIT_BYTES = 56 * 1024 * 1024

FFN_ROWS = 512
FFN_COLS = 512
MM_ROWS = 1024
MM_COLS = 1024
FOX_BLOCK = 1024
DIL_TILE = 2048
OUT_ROWS = 512
GLU_ROWS = 512
CUMSUM_ROWS = 512
CAST_STEPS = 8
S5_ROWS = 256
S5_SEG = S5_ROWS // V7X_SUBLANES
S5_GB = 16
S5_NB = S5_GROUPS // S5_GB
S5_LANES = S5_GB * S5_STATE


def _params(sem):
    return pltpu.CompilerParams(dimension_semantics=sem, vmem_limit_bytes=V7X_VMEM_LIMIT_BYTES)


def _layer_norm(v, g, b):
    mu = jnp.mean(v, axis=-1, keepdims=True)
    d = v - mu
    var = jnp.mean(d * d, axis=-1, keepdims=True)
    return d * lax.rsqrt(var + LN_EPS) * g + b


def _dot(a, b):
    return jnp.dot(a, b, preferred_element_type=F32)


def _dot_nt(a, b):
    return lax.dot_general(a, b, (((1,), (1,)), ((), ())), preferred_element_type=F32)


def _ffn_kernel(x_ref, xb_ref, wg_ref, wu_ref, wd_ref, g_ref, b_ref, o_ref, ob_ref, h_ref):
    j = pl.program_id(1)
    last = pl.num_programs(1) - 1

    def hidden():
        xb = xb_ref[...]
        gate = _dot(xb, wg_ref[...])
        up = _dot(xb, wu_ref[...])
        return (gate * jax.nn.sigmoid(gate) * up).astype(BF16)

    def down(slot):
        return _dot(h_ref[slot], wd_ref[...])

    @pl.when(j == 0)
    def _():
        h_ref[0] = hidden()

    @pl.when(j == 1)
    def _():
        o_ref[...] = down(0)
        h_ref[1] = hidden()

    @pl.when((j > 1) & (j < last))
    def _():
        o_ref[...] += down((j - 1) % 2)
        h_ref[j % 2] = hidden()

    @pl.when(j == last)
    def _():
        y = _layer_norm(ALPHA * x_ref[...] + 0.5 * (o_ref[...] + down((j - 1) % 2)), g_ref[...], b_ref[...])
        o_ref[...] = y
        ob_ref[...] = y.astype(BF16)


def _ffn(x, xb, wg, wu, wd, g, b):
    s = x.shape[0]
    chunks = D_FF // FFN_COLS
    row = lambda i, j: (i, 0)
    up_map = lambda i, j: (0, jnp.minimum(j, chunks - 1))
    return pl.pallas_call(
        _ffn_kernel,
        grid=(s // FFN_ROWS, chunks + 1),
        in_specs=[
            pl.BlockSpec((FFN_ROWS, D_MODEL), row),
            pl.BlockSpec((FFN_ROWS, D_MODEL), row),
            pl.BlockSpec((D_MODEL, FFN_COLS), up_map),
            pl.BlockSpec((D_MODEL, FFN_COLS), up_map),
            pl.BlockSpec((FFN_COLS, D_MODEL), lambda i, j: (jnp.maximum(j - 1, 0), 0)),
            pl.BlockSpec((1, D_MODEL), lambda i, j: (0, 0)),
            pl.BlockSpec((1, D_MODEL), lambda i, j: (0, 0)),
        ],
        out_specs=[pl.BlockSpec((FFN_ROWS, D_MODEL), row), pl.BlockSpec((FFN_ROWS, D_MODEL), row)],
        out_shape=[jax.ShapeDtypeStruct((s, D_MODEL), F32), jax.ShapeDtypeStruct((s, D_MODEL), BF16)],
        scratch_shapes=[pltpu.VMEM((2, FFN_ROWS, FFN_COLS), BF16)],
        compiler_params=_params(("parallel", "arbitrary")),
        name="ffn",
    )(x, xb, wg, wu, wd, g, b)


def _cast_kernel(wg_ref, wu_ref, wd_ref, og_ref, ou_ref, od_ref):
    og_ref[...] = wg_ref[...].astype(BF16)
    ou_ref[...] = wu_ref[...].astype(BF16)
    od_ref[...] = wd_ref[...].astype(BF16)


def _ffn_weights_bf16(w_gate, w_up, w_down, layer):
    steps = CAST_STEPS
    up_rows, down_rows = D_MODEL // steps, D_FF // steps
    up_in = pl.BlockSpec((None, up_rows, D_FF), lambda r: (layer, r, 0))
    up_out = pl.BlockSpec((up_rows, D_FF), lambda r: (r, 0))
    return pl.pallas_call(
        _cast_kernel,
        grid=(steps,),
        in_specs=[up_in, up_in, pl.BlockSpec((None, down_rows, D_MODEL), lambda r: (layer, r, 0))],
        out_specs=[up_out, up_out, pl.BlockSpec((down_rows, D_MODEL), lambda r: (r, 0))],
        out_shape=[jax.ShapeDtypeStruct((D_MODEL, D_FF), BF16), jax.ShapeDtypeStruct((D_MODEL, D_FF), BF16),
                   jax.ShapeDtypeStruct((D_FF, D_MODEL), BF16)],
        compiler_params=_params(("parallel",)),
        name="ffn_weights_bf16",
    )(w_gate, w_up, w_down)


def _attn_in_kernel(x_ref, w_ref, wf_ref, o_ref, f_ref):
    x = x_ref[...]
    o_ref[...] = _dot(x, w_ref[...]).astype(o_ref.dtype)

    @pl.when(pl.program_id(1) == 0)
    def _():
        f_ref[...] = _dot(x, wf_ref[...])


def _attn_in(xb, w, wf):
    s, n = xb.shape[0], w.shape[1]
    return pl.pallas_call(
        _attn_in_kernel,
        grid=(s // MM_ROWS, n // MM_COLS),
        in_specs=[
            pl.BlockSpec((MM_ROWS, D_MODEL), lambda i, j: (i, 0)),
            pl.BlockSpec((D_MODEL, MM_COLS), lambda i, j: (0, j)),
            pl.BlockSpec((D_MODEL, V7X_LANES), lambda i, j: (0, 0)),
        ],
        out_specs=[
            pl.BlockSpec((MM_ROWS, MM_COLS), lambda i, j: (i, j)),
            pl.BlockSpec((MM_ROWS, V7X_LANES), lambda i, j: (i, 0)),
        ],
        out_shape=[jax.ShapeDtypeStruct((s, n), BF16), jax.ShapeDtypeStruct((s, V7X_LANES), F32)],
        compiler_params=_params(("parallel", "arbitrary")),
        name="attn_in",
    )(xb, w, wf)


def _mm_kernel(x_ref, w_ref, o_ref):
    o_ref[...] = _dot(x_ref[...], w_ref[...]).astype(o_ref.dtype)


def _matmul(xb, w, out_dtype):
    s, n = xb.shape[0], w.shape[1]
    return pl.pallas_call(
        _mm_kernel,
        grid=(s // MM_ROWS, n // MM_COLS),
        in_specs=[
            pl.BlockSpec((MM_ROWS, xb.shape[1]), lambda i, j: (i, 0)),
            pl.BlockSpec((xb.shape[1], MM_COLS), lambda i, j: (0, j)),
        ],
        out_specs=pl.BlockSpec((MM_ROWS, MM_COLS), lambda i, j: (i, j)),
        out_shape=jax.ShapeDtypeStruct((s, n), out_dtype),
        compiler_params=_params(("parallel", "parallel")),
        name="matmul",
    )(xb, w)


def _split3(v):
    hi = v.astype(BF16)
    r1 = v - hi.astype(F32)
    mid = r1.astype(BF16)
    lo = (r1 - mid.astype(F32)).astype(BF16)
    return hi, mid, lo


def _cumsum_kernel(f_ref, bf_ref, c_ref, carry_ref):
    @pl.when(pl.program_id(0) == 0)
    def _():
        carry_ref[...] = jnp.zeros_like(carry_ref)

    z = f_ref[...] + bf_ref[...]
    log_f = jnp.minimum(z, 0.0) - jnp.log1p(jnp.exp(-jnp.abs(z)))
    n = z.shape[0]
    tri = (lax.broadcasted_iota(jnp.int32, (n, n), 0) >= lax.broadcasted_iota(jnp.int32, (n, n), 1)).astype(BF16)
    hi, mid, lo = _split3(log_f)
    c = _dot(tri, hi) + _dot(tri, mid) + _dot(tri, lo) + carry_ref[0:1, :]
    c_ref[...] = c
    carry_ref[...] = jnp.broadcast_to(c[n - 1:n, :], carry_ref.shape)


def _forget_cumsum(f_logit, b_f_row):
    s = f_logit.shape[0]
    return pl.pallas_call(
        _cumsum_kernel,
        grid=(s // CUMSUM_ROWS,),
        in_specs=[
            pl.BlockSpec((CUMSUM_ROWS, V7X_LANES), lambda i: (i, 0)),
            pl.BlockSpec((1, V7X_LANES), lambda i: (0, 0)),
        ],
        out_specs=pl.BlockSpec((CUMSUM_ROWS, V7X_LANES), lambda i: (i, 0)),
        out_shape=jax.ShapeDtypeStruct((s, V7X_LANES), F32),
        scratch_shapes=[pltpu.VMEM((V7X_SUBLANES, V7X_LANES), F32)],
        compiler_params=_params(("arbitrary",)),
        name="forget_cumsum",
    )(f_logit, b_f_row)


def _fox_prep_kernel(qkv_ref, c_ref, qt_ref, ka_ref, vt_ref):
    rows = qkv_ref.shape[0]
    lane = lax.broadcasted_iota(jnp.int32, (rows, V7X_LANES), 1)
    bias = c_ref[...] * (-LOG2E)
    ones_rows = (lax.broadcasted_iota(jnp.int32, (FOX_AUG, rows), 0) < 3).astype(BF16)
    sum_rows = (lax.broadcasted_iota(jnp.int32, (FOX_VROWS - HEAD_DIM, rows), 0) < 1).astype(BF16)
    for h in range(N_FOX_HEADS):
        hs = slice(h * HEAD_DIM, (h + 1) * HEAD_DIM)
        a0 = h * (HEAD_DIM + FOX_AUG)
        qt_ref[a0:a0 + HEAD_DIM, :] = qkv_ref[:, hs].astype(F32).T.astype(BF16)
        qt_ref[a0 + HEAD_DIM:a0 + HEAD_DIM + FOX_AUG, :] = ones_rows
        v0 = h * FOX_VROWS
        vt_ref[v0:v0 + HEAD_DIM, :] = (
            qkv_ref[:, 2 * FOX_W + h * HEAD_DIM:2 * FOX_W + (h + 1) * HEAD_DIM].astype(F32).T.astype(BF16))
        vt_ref[v0 + HEAD_DIM:v0 + FOX_VROWS, :] = sum_rows
        hi, mid, lo = _split3(jnp.broadcast_to(bias[:, h:h + 1], (rows, V7X_LANES)))
        aug = jnp.where(lane == 0, hi.astype(F32), jnp.where(lane == 1, mid.astype(F32),
                                                              jnp.where(lane == 2, lo.astype(F32), 0.0)))
        ka_ref[:, a0:a0 + HEAD_DIM] = qkv_ref[:, FOX_W + h * HEAD_DIM:FOX_W + (h + 1) * HEAD_DIM]
        ka_ref[:, a0 + HEAD_DIM:a0 + HEAD_DIM + FOX_AUG] = aug.astype(BF16)


def _fox_prep(qkv, c_rows):
    s = qkv.shape[0]
    wide = N_FOX_HEADS * (HEAD_DIM + FOX_AUG)
    vrows = N_FOX_HEADS * FOX_VROWS
    return pl.pallas_call(
        _fox_prep_kernel,
        grid=(s // FOX_BLOCK,),
        in_specs=[pl.BlockSpec((FOX_BLOCK, 3 * FOX_W), lambda i: (i, 0)),
                  pl.BlockSpec((FOX_BLOCK, V7X_LANES), lambda i: (i, 0))],
        out_specs=[pl.BlockSpec((wide, FOX_BLOCK), lambda i: (0, i)),
                   pl.BlockSpec((FOX_BLOCK, wide), lambda i: (i, 0)),
                   pl.BlockSpec((vrows, FOX_BLOCK), lambda i: (0, i))],
        out_shape=[jax.ShapeDtypeStruct((wide, s), BF16), jax.ShapeDtypeStruct((s, wide), BF16),
                   jax.ShapeDtypeStruct((vrows, s), BF16)],
        compiler_params=_params(("parallel",)),
        name="fox_prep",
    )(qkv, c_rows)


def _fox_kernel(qi_ref, kj_ref, qt_ref, ka_ref, vt_ref, o_ref, m_ref, acc_ref, s_ref):
    t = pl.program_id(0)
    i = qi_ref[t]
    j = kj_ref[t]
    blk = ka_ref.shape[0]
    aw = HEAD_DIM + FOX_AUG

    @pl.when(j == 0)
    def _():
        m_ref[...] = jnp.full_like(m_ref, NEG)
        acc_ref[...] = jnp.zeros_like(acc_ref)

    def scores(h):
        return _dot(ka_ref[:, h * aw:(h + 1) * aw], qt_ref[h * aw:(h + 1) * aw, :])

    def step(diagonal):
        if diagonal:
            keep = lax.broadcasted_iota(jnp.int32, (blk, blk), 0) <= lax.broadcasted_iota(jnp.int32, (blk, blk), 1)
        s_ref[0] = scores(0)
        for h in range(N_FOX_HEADS):
            if h + 1 < N_FOX_HEADS:
                s_ref[(h + 1) % 2] = scores(h + 1)
            s = s_ref[h % 2]
            if diagonal:
                s = jnp.where(keep, s, NEG)
            m_prev = m_ref[h:h + 1, :]
            m_new = jnp.maximum(m_prev, jnp.max(s, axis=0, keepdims=True))
            p = jnp.exp2(s - m_new).astype(BF16)
            alpha = jnp.exp2(m_prev - m_new)
            m_ref[h:h + 1, :] = m_new
            vs = slice(h * FOX_VROWS, (h + 1) * FOX_VROWS)
            acc_ref[vs, :] = alpha * acc_ref[vs, :] + _dot(vt_ref[vs, :], p)

    @pl.when(j < i)
    def _():
        step(False)

    @pl.when(j == i)
    def _():
        step(True)
        for h in range(N_FOX_HEADS):
            v0 = h * FOX_VROWS
            out_t = acc_ref[v0:v0 + HEAD_DIM, :] / acc_ref[v0 + HEAD_DIM:v0 + HEAD_DIM + 1, :]
            o_ref[:, h * HEAD_DIM:(h + 1) * HEAD_DIM] = out_t.T.astype(o_ref.dtype)


def _fox_attention(qt, ka, vt):
    s = ka.shape[0]
    nb = s // FOX_BLOCK
    qi = jnp.asarray([i for i in range(nb) for _ in range(i + 1)], jnp.int32)
    kj = jnp.asarray([j for i in range(nb) for j in range(i + 1)], jnp.int32)
    wide = ka.shape[1]
    vrows = vt.shape[0]
    grid_spec = pltpu.PrefetchScalarGridSpec(
        num_scalar_prefetch=2,
        grid=(qi.shape[0],),
        in_specs=[
            pl.BlockSpec((wide, FOX_BLOCK), lambda t, qi, kj: (0, qi[t])),
            pl.BlockSpec((FOX_BLOCK, wide), lambda t, qi, kj: (kj[t], 0)),
            pl.BlockSpec((vrows, FOX_BLOCK), lambda t, qi, kj: (0, kj[t])),
        ],
        out_specs=pl.BlockSpec((FOX_BLOCK, FOX_W), lambda t, qi, kj: (qi[t], 0)),
        scratch_shapes=[
            pltpu.VMEM((N_FOX_HEADS, FOX_BLOCK), F32),
            pltpu.VMEM((vrows, FOX_BLOCK), F32),
            pltpu.VMEM((2, FOX_BLOCK, FOX_BLOCK), F32),
        ],
    )
    return pl.pallas_call(
        _fox_kernel,
        grid_spec=grid_spec,
        out_shape=jax.ShapeDtypeStruct((s, FOX_W), BF16),
        compiler_params=_params(("arbitrary",)),
        name="fox_attention",
    )(qi, kj, qt, ka, vt)


def _dil_kernel(q_ref, k_ref, v_ref, kp_ref, vp_ref, y_ref, o_scr, l_scr):
    has_prev_tile = pl.program_id(0) > 0
    b = DIL_BLOCK
    tile = q_ref.shape[0]
    jr = lax.broadcasted_iota(jnp.int32, (b, 2 * b), 0)
    mc = lax.broadcasted_iota(jnp.int32, (b, 2 * b), 1)
    band = (mc >= jr) & (mc <= jr + b)
    band_first = band & ((mc >= b) | has_prev_tile)

    def rows(ref, start, d):
        return ref[pl.ds(start, b), :] if d == 1 else ref[pl.ds(start, b, stride=d), :]

    for g, (_, d) in enumerate(DIL_PAIRS):
        for r in range(d):
            for n in range(tile // (b * d)):
                cur = r + n * b * d
                if n > 0:
                    k_prev, v_prev, keep = rows(k_ref, cur - b * d, d), rows(v_ref, cur - b * d, d), band
                else:
                    prev = tile - b * d + r
                    k_prev, v_prev, keep = rows(kp_ref, prev, d), rows(vp_ref, prev, d), band_first
                k2 = jnp.concatenate([k_prev, rows(k_ref, cur, d)], axis=0).astype(BF16)
                v2 = jnp.concatenate([v_prev, rows(v_ref, cur, d)], axis=0).astype(BF16)
                s = jnp.where(keep, _dot_nt(rows(q_ref, cur, d).astype(BF16), k2), NEG)
                m = jnp.max(s, axis=1, keepdims=True)
                p = jnp.exp(s - m)
                l = jnp.sum(p, axis=1, keepdims=True)
                o = _dot(p.astype(BF16), v2) / l
                lse = jnp.broadcast_to(m + jnp.log(l), (b, HEAD_DIM))
                if d == 1:
                    o_scr[g, pl.ds(cur, b), :] = o
                    l_scr[g, pl.ds(cur, b), :] = lse
                else:
                    o_scr[g, pl.ds(cur, b, stride=d), :] = o
                    l_scr[g, pl.ds(cur, b, stride=d), :] = lse

    l1, l2, l3 = l_scr[0], l_scr[1], l_scr[2]
    mx = jnp.maximum(jnp.maximum(l1, l2), l3)
    e1, e2, e3 = jnp.exp(l1 - mx), jnp.exp(l2 - mx), jnp.exp(l3 - mx)
    den = e1 + e2 + e3
    y_ref[...] = ((e1 / den) * o_scr[0] + (e2 / den) * o_scr[1] + (e3 / den) * o_scr[2]).astype(y_ref.dtype)


def _dilated_attention(qkv):
    s = qkv.shape[0]
    cur = lambda c: pl.BlockSpec((DIL_TILE, HEAD_DIM), lambda t, h: (t, c * N_DIL_HEADS + h))
    prev = lambda c: pl.BlockSpec((DIL_TILE, HEAD_DIM), lambda t, h: (jnp.maximum(t - 1, 0), c * N_DIL_HEADS + h))
    return pl.pallas_call(
        _dil_kernel,
        grid=(s // DIL_TILE, N_DIL_HEADS),
        in_specs=[cur(0), cur(1), cur(2), prev(1), prev(2)],
        out_specs=pl.BlockSpec((DIL_TILE, HEAD_DIM), lambda t, h: (t, h)),
        out_shape=jax.ShapeDtypeStruct((s, DIL_W), BF16),
        scratch_shapes=[pltpu.VMEM((len(DIL_PAIRS), DIL_TILE, HEAD_DIM), F32),
                        pltpu.VMEM((len(DIL_PAIRS), DIL_TILE, HEAD_DIM), F32)],
        compiler_params=_params(("parallel", "parallel")),
        name="dilated_attention",
    )(qkv, qkv, qkv, qkv, qkv)


def _attn_out_kernel(x_ref, ya_ref, yb_ref, w_ref, g_ref, b_ref, o_ref, ob_ref):
    y = _dot(jnp.concatenate([ya_ref[...], yb_ref[...]], axis=1), w_ref[...])
    out = _layer_norm(ALPHA * x_ref[...] + y, g_ref[...], b_ref[...])
    o_ref[...] = out
    ob_ref[...] = out.astype(BF16)


def _attn_out(x, ya, yb, w, g, b):
    s = x.shape[0]
    row = lambda width: pl.BlockSpec((OUT_ROWS, width), lambda i: (i, 0))
    const = lambda shape: pl.BlockSpec(shape, lambda i: (0, 0))
    return pl.pallas_call(
        _attn_out_kernel,
        grid=(s // OUT_ROWS,),
        in_specs=[row(D_MODEL), row(FOX_W), row(DIL_W),
                  const((FOX_W + DIL_W, D_MODEL)), const((1, D_MODEL)), const((1, D_MODEL))],
        out_specs=[row(D_MODEL), row(D_MODEL)],
        out_shape=[jax.ShapeDtypeStruct((s, D_MODEL), F32), jax.ShapeDtypeStruct((s, D_MODEL), BF16)],
        compiler_params=_params(("parallel",)),
        name="attn_out",
    )(x, ya, yb, w, g, b)


def _cmul(ar, ai, br, bi):
    return ar * br - ai * bi, ar * bi + ai * br


def _s5_prep_kernel(lrg_ref, lig_ref, ldtg_ref, lr_ref, li_ref, ldt_ref, br_ref, bi_ref,
                    pr_ref, pi_ref, qr_ref, qi_ref, bbr_ref, bbi_ref):
    def a_bar(lr, li, ldt):
        dt = jnp.exp(ldt)
        mag = jnp.exp(lr * dt)
        return mag * jnp.cos(li * dt), mag * jnp.sin(li * dt)

    lr, li = lr_ref[...], li_ref[...]
    ar, ai = a_bar(lr, li, ldt_ref[...])
    den = lr * lr + li * li
    cr = ((ar - 1.0) * lr + ai * li) / den
    ci = (ai * lr - (ar - 1.0) * li) / den
    bbr_ref[...], bbi_ref[...] = _cmul(cr, ci, br_ref[...], bi_ref[...])

    ar, ai = a_bar(lrg_ref[...], lig_ref[...], ldtg_ref[...])
    pr, pi = ar, ai
    pr_ref[0], pi_ref[0] = pr, pi
    for k in range(1, S5_SEG):
        pr, pi = _cmul(pr, pi, ar, ai)
        pr_ref[k], pi_ref[k] = pr, pi
    sr, si = pr, pi
    qr_ref[0], qi_ref[0] = sr, si
    for k in range(1, V7X_SUBLANES):
        pr, pi = _cmul(pr, pi, sr, si)
        qr_ref[k], qi_ref[k] = pr, pi


def _s5_prep(lam_re, lam_im, log_dt, b_re, b_im):
    ldt = jnp.broadcast_to(log_dt[:, None], lam_re.shape)
    rep = lambda v: jnp.repeat(v, S5_GROUP, axis=0)
    rows = S5_GROUPS * S5_GROUP
    bt = lambda v: v.transpose(0, 2, 1).reshape(rows, S5_STATE)
    mat = jax.ShapeDtypeStruct((rows, S5_STATE), F32)
    pw = jax.ShapeDtypeStruct((S5_SEG, S5_GROUPS, S5_STATE), F32)
    qw = jax.ShapeDtypeStruct((V7X_SUBLANES, S5_GROUPS, S5_STATE), F32)
    return pl.pallas_call(_s5_prep_kernel, out_shape=[pw, pw, qw, qw, mat, mat], name="s5_prep")(
        lam_re, lam_im, ldt, rep(lam_re), rep(lam_im), rep(ldt), bt(b_re), bt(b_im))


def _s5_core_kernel(u_ref, bblk_ref, cblk_ref, pr_ref, pi_ref, qr_ref, qi_ref, d_ref, z_ref,
                    up_ref, sr_ref, si_ref, cr_ref, ci_ref):
    @pl.when(pl.program_id(0) == 0)
    def _():
        cr_ref[...] = jnp.zeros_like(cr_ref)
        ci_ref[...] = jnp.zeros_like(ci_ref)

    sl = V7X_SUBLANES
    rows = u_ref.shape[0]
    ri = lax.broadcasted_iota(jnp.int32, (rows, rows), 0)
    ci = lax.broadcasted_iota(jnp.int32, (rows, rows), 1)
    perm = ((ri // sl == ci % S5_SEG) & (ri % sl == ci // S5_SEG)).astype(BF16)
    unperm = ((ci // sl == ri % S5_SEG) & (ci % sl == ri // S5_SEG)).astype(BF16)
    hi, mid, lo = _split3(u_ref[...])
    up_ref[...] = _dot(perm, hi) + _dot(perm, mid) + _dot(perm, lo)

    sub = lax.broadcasted_iota(jnp.int32, (sl, S5_LANES), 0)
    width = S5_GB * S5_GROUP
    ys = []
    for jb in range(S5_NB):
        bu = _dot(up_ref[:, jb * width:(jb + 1) * width].astype(BF16), bblk_ref[jb])
        a_re, a_im = pr_ref[jb, 0:1, :], pi_ref[jb, 0:1, :]
        xr, xi = bu[0:sl, :S5_LANES], bu[0:sl, S5_LANES:]
        sr_ref[0:sl, :], si_ref[0:sl, :] = xr, xi
        for j in range(1, S5_SEG):
            tr, ti = _cmul(a_re, a_im, xr, xi)
            xr, xi = tr + bu[j * sl:(j + 1) * sl, :S5_LANES], ti + bu[j * sl:(j + 1) * sl, S5_LANES:]
            sr_ref[j * sl:(j + 1) * sl, :], si_ref[j * sl:(j + 1) * sl, :] = xr, xi

        q_re, q_im = qr_ref[jb], qi_ref[jb]
        er, ei = xr, xi
        for k in (1, 2, 4):
            tr = jnp.where(sub >= k, pltpu.roll(er, k, 0), 0.0)
            ti = jnp.where(sub >= k, pltpu.roll(ei, k, 0), 0.0)
            mr, mi = _cmul(q_re[k - 1:k, :], q_im[k - 1:k, :], tr, ti)
            er, ei = er + mr, ei + mi
        in_re, in_im = cr_ref[jb:jb + 1, :], ci_ref[jb:jb + 1, :]
        mr, mi = _cmul(q_re, q_im, in_re, in_im)
        er, ei = er + mr, ei + mi
        cr_ref[jb:jb + 1, :], ci_ref[jb:jb + 1, :] = er[sl - 1:, :], ei[sl - 1:, :]
        cin_re = jnp.where(sub >= 1, pltpu.roll(er, 1, 0), in_re)
        cin_im = jnp.where(sub >= 1, pltpu.roll(ei, 1, 0), in_im)
        fr, fi = _cmul(pr_ref[jb][:, None, :], pi_ref[jb][:, None, :], cin_re[None], cin_im[None])
        shape3 = (S5_SEG, sl, S5_LANES)
        state = jnp.concatenate([(sr_ref[...].reshape(shape3) + fr).reshape(-1, S5_LANES),
                                 (si_ref[...].reshape(shape3) + fi).reshape(-1, S5_LANES)], axis=1)
        ys.append(_dot(state.astype(BF16), cblk_ref[jb]))
    y = jnp.concatenate(ys, axis=1) + d_ref[...] * up_ref[...]
    z = 0.5 * y * (1.0 + jnp.tanh(math.sqrt(2.0 / math.pi) * (y + 0.044715 * (y * y * y))))
    z_ref[...] = _dot(unperm, z.astype(BF16)).astype(z_ref.dtype)


def _s5_core(u, bblk, cblk, p_re, p_im, q_re, q_im, d_row):
    s = u.shape[0]
    const = lambda shape: pl.BlockSpec(shape, lambda t: (0,) * len(shape))
    return pl.pallas_call(
        _s5_core_kernel,
        grid=(s // S5_ROWS,),
        in_specs=[pl.BlockSpec((S5_ROWS, S5_WIDTH), lambda t: (t, 0)),
                  const(bblk.shape), const(cblk.shape), const(p_re.shape), const(p_im.shape),
                  const(q_re.shape), const(q_im.shape), const((1, S5_WIDTH))],
        out_specs=pl.BlockSpec((S5_ROWS, S5_WIDTH), lambda t: (t, 0)),
        out_shape=jax.ShapeDtypeStruct((s, S5_WIDTH), BF16),
        scratch_shapes=[pltpu.VMEM((S5_ROWS, S5_WIDTH), F32),
                        pltpu.VMEM((S5_ROWS, S5_LANES), F32), pltpu.VMEM((S5_ROWS, S5_LANES), F32),
                        pltpu.VMEM((V7X_SUBLANES, S5_LANES), F32), pltpu.VMEM((V7X_SUBLANES, S5_LANES), F32)],
        compiler_params=_params(("arbitrary",)),
        name="s5_core",
    )(u, bblk, cblk, p_re, p_im, q_re, q_im, d_row)


def _block_diag(blocks):
    nb, gb, r, c = blocks.shape
    eye = jnp.eye(gb, dtype=blocks.dtype)
    return jnp.einsum("ngrc,gk->ngrkc", blocks, eye).reshape(nb, gb * r, gb * c)


def _glu_kernel(x_ref, z_ref, wo_ref, wg_ref, g_ref, b_ref, o_ref, ob_ref):
    z = z_ref[...]
    y = _dot(z, wo_ref[...]) * jax.nn.sigmoid(_dot(z, wg_ref[...]))
    out = _layer_norm(ALPHA * x_ref[...] + y, g_ref[...], b_ref[...])
    o_ref[...] = out
    ob_ref[...] = out.astype(BF16)


def _glu(x, z, wo, wg, g, b):
    s = x.shape[0]
    row = lambda width: pl.BlockSpec((GLU_ROWS, width), lambda i: (i, 0))
    const = lambda shape: pl.BlockSpec(shape, lambda i: (0, 0))
    return pl.pallas_call(
        _glu_kernel,
        grid=(s // GLU_ROWS,),
        in_specs=[row(D_MODEL), row(S5_WIDTH), const((S5_WIDTH, D_MODEL)), const((S5_WIDTH, D_MODEL)),
                  const((1, D_MODEL)), const((1, D_MODEL))],
        out_specs=[row(D_MODEL), row(D_MODEL)],
        out_shape=[jax.ShapeDtypeStruct((s, D_MODEL), F32), jax.ShapeDtypeStruct((s, D_MODEL), BF16)],
        compiler_params=_params(("parallel",)),
        name="glu",
    )(x, z, wo, wg, g, b)


def _attention_mixer(x, xb, w_in, b_f, w_out, g, b):
    scale = HEAD_DIM ** -0.5
    fox_q, fox_k, fox_v = (w_in[:, n * FOX_W:(n + 1) * FOX_W] for n in range(3))
    w_f = w_in[:, 3 * FOX_W:3 * FOX_W + N_FOX_HEADS]
    d0 = 3 * FOX_W + N_FOX_HEADS
    dil_q, dil_k, dil_v = (w_in[:, d0 + n * DIL_W:d0 + (n + 1) * DIL_W] for n in range(3))
    w_fox = jnp.concatenate([fox_q * (scale * LOG2E), fox_k, fox_v], axis=1).astype(BF16)
    w_dil = jnp.concatenate([dil_q * scale, dil_k, dil_v], axis=1).astype(BF16)
    w_f = jnp.pad(w_f, ((0, 0), (0, V7X_LANES - N_FOX_HEADS))).astype(BF16)
    qkv_fox, f_logit = _attn_in(xb, w_fox, w_f)
    c_rows = _forget_cumsum(f_logit, jnp.pad(b_f, (0, V7X_LANES - N_FOX_HEADS))[None, :])
    ya = _fox_attention(*_fox_prep(qkv_fox, c_rows))
    yb = _dilated_attention(_matmul(xb, w_dil, F32))
    return _attn_out(x, ya, yb, w_out.astype(BF16), g, b)


def _s5_mixer(x, xb, w_in, lam_re, lam_im, log_dt, b_re, b_im, c_re, c_im, d_skip, w_glu_out, w_glu_gate, g, b):
    u = _matmul(xb, w_in.astype(BF16), F32)
    p_re, p_im, q_re, q_im, bb_re, bb_im = _s5_prep(lam_re, lam_im, log_dt, b_re, b_im)
    lanes = lambda p: p.reshape(p.shape[0], S5_NB, S5_LANES).transpose(1, 0, 2)
    blk = lambda v: v.reshape(S5_NB, S5_GB, S5_GROUP, S5_STATE)
    bblk = jnp.concatenate([_block_diag(blk(bb_re)), _block_diag(blk(bb_im))], axis=2).astype(BF16)
    ct = lambda v: v.transpose(0, 2, 1).reshape(S5_NB, S5_GB, S5_STATE, S5_GROUP)
    cblk = jnp.concatenate([_block_diag(ct(c_re)), -_block_diag(ct(c_im))], axis=1).astype(BF16)
    z = _s5_core(u, bblk, cblk, lanes(p_re), lanes(p_im), lanes(q_re), lanes(q_im), d_skip[None, :])
    return _glu(x, z, w_glu_out.astype(BF16), w_glu_gate.astype(BF16), g, b)


def kernel(x, ffn1_w_gate, ffn1_w_up, ffn1_w_down, ffn2_w_gate, ffn2_w_up, ffn2_w_down, ln_gain, ln_bias,
           attn_w_in, attn_b_f, attn_w_out, s5_w_in, s5_lambda_re, s5_lambda_im, s5_log_dt, s5_b_re, s5_b_im,
           s5_c_re, s5_c_im, s5_d, s5_w_glu_out, s5_w_glu_gate):
    batch, seq, _ = x.shape
    assert batch == 1
    h = x.reshape(seq, D_MODEL)
    hb = h.astype(BF16)
    ln = lambda i, k: (ln_gain[i, k][None, :], ln_bias[i, k][None, :])
    for i in range(DEPTH):
        h, hb = _ffn(h, hb, *_ffn_weights_bf16(ffn1_w_gate, ffn1_w_up, ffn1_w_down, i), *ln(i, 0))
        j = i // 2
        if i % 2 == 0:
            h, hb = _attention_mixer(h, hb, attn_w_in[j], attn_b_f[j], attn_w_out[j], *ln(i, 1))
        else:
            h, hb = _s5_mixer(h, hb, s5_w_in[j], s5_lambda_re[j], s5_lambda_im[j], s5_log_dt[j], s5_b_re[j],
                              s5_b_im[j], s5_c_re[j], s5_c_im[j], s5_d[j], s5_w_glu_out[j], s5_w_glu_gate[j],
                              *ln(i, 1))
        h, hb = _ffn(h, hb, *_ffn_weights_bf16(ffn2_w_gate, ffn2_w_up, ffn2_w_down, i), *ln(i, 2))
    return h.reshape(batch, seq, D_MODEL)
```

```python
import math

import jax
import jax.numpy as jnp
from jax import lax
from jax.experimental import pallas as pl
from jax.experimental.pallas import tpu as pltpu

D_MODEL = 2048
DEPTH = 2
HEAD_DIM = 128
N_FOX_HEADS = 8
N_DIL_HEADS = 8
DIL_PAIRS = ((128, 1), (512, 4), (2048, 16))
DIL_BLOCK = 128
D_FF = 5632
S5_GROUP = 16
S5_WIDTH = 1024
S5_GROUPS = S5_WIDTH // S5_GROUP
S5_STATE = 64
ALPHA = (2.0 * DEPTH) ** 0.25
LN_EPS = 1e-5
FOX_W = N_FOX_HEADS * HEAD_DIM
DIL_W = N_DIL_HEADS * HEAD_DIM

F32 = jnp.float32
BF16 = jnp.bfloat16
NEG = -1e30
LOG2E = math.log2(math.e)
FOX_AUG = 128
FOX_VROWS = HEAD_DIM + 16

V7X_LANES = 128
V7X_SUBLANES = 8
V7X_VMEM_LIMIT_BYTES = 56 * 1024 * 1024

FFN_ROWS = 512
FFN_COLS = 512
FFN_CHUNKS = D_FF // FFN_COLS
FFN_RING_SLOTS = 3
assert (FFN_CHUNKS + 1) % FFN_RING_SLOTS == 0
MM_ROWS = 1024
MM_COLS = 1024
FOX_BLOCK = 1024
DIL_TILE = 2048
OUT_ROWS = 512
GLU_ROWS = 512
CUMSUM_ROWS = 512
CAST_STEPS = 8
S5_ROWS = 256
S5_SEG = S5_ROWS // V7X_SUBLANES
S5_GB = 16
S5_NB = S5_GROUPS // S5_GB
S5_LANES = S5_GB * S5_STATE


def _params(sem):
    return pltpu.CompilerParams(dimension_semantics=sem, vmem_limit_bytes=V7X_VMEM_LIMIT_BYTES)


def _layer_norm(v, g, b):
    mu = jnp.mean(v, axis=-1, keepdims=True)
    d = v - mu
    var = jnp.mean(d * d, axis=-1, keepdims=True)
    return d * lax.rsqrt(var + LN_EPS) * g + b


def _dot(a, b):
    return jnp.dot(a, b, preferred_element_type=F32)


def _dot_nt(a, b):
    return lax.dot_general(a, b, (((1,), (1,)), ((), ())), preferred_element_type=F32)


def _ffn_kernel(x_ref, xb_ref, wg_hbm, wu_hbm, wd_hbm, g_ref, b_ref, o_ref, ob_ref,
                h_ref, wg_buf, wu_buf, wd_buf, sem):
    i = pl.program_id(0)
    last_tile = pl.num_programs(0) - 1
    chunks = FFN_CHUNKS
    nsub = chunks + 1

    def ring(sub, slot, op):
        @pl.when(sub < chunks)
        def _():
            c = jnp.minimum(sub, chunks - 1)
            op(pltpu.make_async_copy(wg_hbm.at[c], wg_buf.at[slot], sem.at[0, slot]))
            op(pltpu.make_async_copy(wu_hbm.at[c], wu_buf.at[slot], sem.at[1, slot]))

        @pl.when(sub >= 1)
        def _():
            c = jnp.maximum(sub - 1, 0)
            op(pltpu.make_async_copy(wd_hbm.at[c], wd_buf.at[slot], sem.at[2, slot]))

    start = lambda cp: cp.start()
    wait = lambda cp: cp.wait()

    @pl.when(i == 0)
    def _():
        ring(jnp.int32(0), 0, start)
        ring(jnp.int32(1), 1, start)

    def fetch_ahead_and_wait(j):
        ahead = j + 2
        wraps = ahead >= nsub

        @pl.when(jnp.logical_or(jnp.logical_not(wraps), i < last_tile))
        def _():
            ring(jnp.where(wraps, ahead - nsub, ahead), ahead % FFN_RING_SLOTS, start)

        ring(j, j % FFN_RING_SLOTS, wait)

    def hidden(slot):
        xb = xb_ref[...]
        gate = _dot(xb, wg_buf[slot])
        up = _dot(xb, wu_buf[slot])
        return (gate * jax.nn.sigmoid(gate) * up).astype(BF16)

    def down(j, slot):
        return _dot(h_ref[(j - 1) % 2], wd_buf[slot])

    fetch_ahead_and_wait(jnp.int32(0))
    h_ref[0] = hidden(0)

    fetch_ahead_and_wait(jnp.int32(1))
    o_ref[...] = down(1, 1)
    h_ref[1] = hidden(1)

    def body(j, carry):
        fetch_ahead_and_wait(j)
        slot = j % FFN_RING_SLOTS
        o_ref[...] += down(j, slot)
        h_ref[j % 2] = hidden(slot)
        return carry

    lax.fori_loop(2, chunks, body, 0)

    fetch_ahead_and_wait(jnp.int32(chunks))
    acc = o_ref[...] + down(chunks, chunks % FFN_RING_SLOTS)
    y = _layer_norm(ALPHA * x_ref[...] + 0.5 * acc, g_ref[...], b_ref[...])
    o_ref[...] = y
    ob_ref[...] = y.astype(BF16)


def _ffn(x, xb, wg, wu, wd, g, b):
    s = x.shape[0]
    row = lambda i: (i, 0)
    hbm = pl.BlockSpec(memory_space=pl.ANY)
    return pl.pallas_call(
        _ffn_kernel,
        grid=(s // FFN_ROWS,),
        in_specs=[
            pl.BlockSpec((FFN_ROWS, D_MODEL), row),
            pl.BlockSpec((FFN_ROWS, D_MODEL), row),
            hbm, hbm, hbm,
            pl.BlockSpec((1, D_MODEL), lambda i: (0, 0)),
            pl.BlockSpec((1, D_MODEL), lambda i: (0, 0)),
        ],
        out_specs=[pl.BlockSpec((FFN_ROWS, D_MODEL), row), pl.BlockSpec((FFN_ROWS, D_MODEL), row)],
        out_shape=[jax.ShapeDtypeStruct((s, D_MODEL), F32), jax.ShapeDtypeStruct((s, D_MODEL), BF16)],
        scratch_shapes=[
            pltpu.VMEM((2, FFN_ROWS, FFN_COLS), BF16),
            pltpu.VMEM((FFN_RING_SLOTS, D_MODEL, FFN_COLS), BF16),
            pltpu.VMEM((FFN_RING_SLOTS, D_MODEL, FFN_COLS), BF16),
            pltpu.VMEM((FFN_RING_SLOTS, FFN_COLS, D_MODEL), BF16),
            pltpu.SemaphoreType.DMA((3, FFN_RING_SLOTS)),
        ],
        compiler_params=_params(("arbitrary",)),
        name="ffn",
    )(x, xb, wg, wu, wd, g, b)


def _cast_kernel(wg_ref, wu_ref, wd_ref, og_ref, ou_ref, od_ref):
    for c in range(FFN_CHUNKS):
        cs = slice(c * FFN_COLS, (c + 1) * FFN_COLS)
        og_ref[c] = wg_ref[:, cs].astype(BF16)
        ou_ref[c] = wu_ref[:, cs].astype(BF16)
    od_ref[...] = wd_ref[...].astype(BF16)


def _ffn_weights_bf16(w_gate, w_up, w_down, layer):
    steps = CAST_STEPS
    up_rows, down_rows = D_MODEL // steps, D_FF // steps
    up_in = pl.BlockSpec((None, up_rows, D_FF), lambda r: (layer, r, 0))
    up_out = pl.BlockSpec((FFN_CHUNKS, up_rows, FFN_COLS), lambda r: (0, r, 0))
    up_shape = jax.ShapeDtypeStruct((FFN_CHUNKS, D_MODEL, FFN_COLS), BF16)
    wg, wu, wd = pl.pallas_call(
        _cast_kernel,
        grid=(steps,),
        in_specs=[up_in, up_in, pl.BlockSpec((None, down_rows, D_MODEL), lambda r: (layer, r, 0))],
        out_specs=[up_out, up_out, pl.BlockSpec((down_rows, D_MODEL), lambda r: (r, 0))],
        out_shape=[up_shape, up_shape, jax.ShapeDtypeStruct((D_FF, D_MODEL), BF16)],
        compiler_params=_params(("parallel",)),
        name="ffn_weights_bf16",
    )(w_gate, w_up, w_down)
    return wg, wu, wd.reshape(FFN_CHUNKS, FFN_COLS, D_MODEL)


def _attn_in_kernel(x_ref, w_ref, wf_ref, o_ref, f_ref):
    x = x_ref[...]
    o_ref[...] = _dot(x, w_ref[...]).astype(o_ref.dtype)

    @pl.when(pl.program_id(1) == 0)
    def _():
        f_ref[...] = _dot(x, wf_ref[...])


def _attn_in(xb, w, wf):
    s, n = xb.shape[0], w.shape[1]
    return pl.pallas_call(
        _attn_in_kernel,
        grid=(s // MM_ROWS, n // MM_COLS),
        in_specs=[
            pl.BlockSpec((MM_ROWS, D_MODEL), lambda i, j: (i, 0)),
            pl.BlockSpec((D_MODEL, MM_COLS), lambda i, j: (0, j)),
            pl.BlockSpec((D_MODEL, V7X_LANES), lambda i, j: (0, 0)),
        ],
        out_specs=[
            pl.BlockSpec((MM_ROWS, MM_COLS), lambda i, j: (i, j)),
            pl.BlockSpec((MM_ROWS, V7X_LANES), lambda i, j: (i, 0)),
        ],
        out_shape=[jax.ShapeDtypeStruct((s, n), BF16), jax.ShapeDtypeStruct((s, V7X_LANES), F32)],
        compiler_params=_params(("parallel", "arbitrary")),
        name="attn_in",
    )(xb, w, wf)


def _mm_kernel(x_ref, w_ref, o_ref):
    o_ref[...] = _dot(x_ref[...], w_ref[...]).astype(o_ref.dtype)


def _matmul(xb, w, out_dtype):
    s, n = xb.shape[0], w.shape[1]
    return pl.pallas_call(
        _mm_kernel,
        grid=(s // MM_ROWS, n // MM_COLS),
        in_specs=[
            pl.BlockSpec((MM_ROWS, xb.shape[1]), lambda i, j: (i, 0)),
            pl.BlockSpec((xb.shape[1], MM_COLS), lambda i, j: (0, j)),
        ],
        out_specs=pl.BlockSpec((MM_ROWS, MM_COLS), lambda i, j: (i, j)),
        out_shape=jax.ShapeDtypeStruct((s, n), out_dtype),
        compiler_params=_params(("parallel", "parallel")),
        name="matmul",
    )(xb, w)


def _split3(v):
    hi = v.astype(BF16)
    r1 = v - hi.astype(F32)
    mid = r1.astype(BF16)
    lo = (r1 - mid.astype(F32)).astype(BF16)
    return hi, mid, lo


def _cumsum_kernel(f_ref, bf_ref, c_ref, carry_ref):
    @pl.when(pl.program_id(0) == 0)
    def _():
        carry_ref[...] = jnp.zeros_like(carry_ref)

    z = f_ref[...] + bf_ref[...]
    log_f = jnp.minimum(z, 0.0) - jnp.log1p(jnp.exp(-jnp.abs(z)))
    n = z.shape[0]
    tri = (lax.broadcasted_iota(jnp.int32, (n, n), 0) >= lax.broadcasted_iota(jnp.int32, (n, n), 1)).astype(BF16)
    hi, mid, lo = _split3(log_f)
    c = _dot(tri, hi) + _dot(tri, mid) + _dot(tri, lo) + carry_ref[0:1, :]
    c_ref[...] = c
    carry_ref[...] = jnp.broadcast_to(c[n - 1:n, :], carry_ref.shape)


def _forget_cumsum(f_logit, b_f_row):
    s = f_logit.shape[0]
    return pl.pallas_call(
        _cumsum_kernel,
        grid=(s // CUMSUM_ROWS,),
        in_specs=[
            pl.BlockSpec((CUMSUM_ROWS, V7X_LANES), lambda i: (i, 0)),
            pl.BlockSpec((1, V7X_LANES), lambda i: (0, 0)),
        ],
        out_specs=pl.BlockSpec((CUMSUM_ROWS, V7X_LANES), lambda i: (i, 0)),
        out_shape=jax.ShapeDtypeStruct((s, V7X_LANES), F32),
        scratch_shapes=[pltpu.VMEM((V7X_SUBLANES, V7X_LANES), F32)],
        compiler_params=_params(("arbitrary",)),
        name="forget_cumsum",
    )(f_logit, b_f_row)


def _fox_prep_kernel(qkv_ref, c_ref, qt_ref, ka_ref, vt_ref):
    rows = qkv_ref.shape[0]
    lane = lax.broadcasted_iota(jnp.int32, (rows, V7X_LANES), 1)
    bias = c_ref[...] * (-LOG2E)
    ones_rows = (lax.broadcasted_iota(jnp.int32, (FOX_AUG, rows), 0) < 3).astype(BF16)
    sum_rows = (lax.broadcasted_iota(jnp.int32, (FOX_VROWS - HEAD_DIM, rows), 0) < 1).astype(BF16)
    for h in range(N_FOX_HEADS):
        hs = slice(h * HEAD_DIM, (h + 1) * HEAD_DIM)
        a0 = h * (HEAD_DIM + FOX_AUG)
        qt_ref[a0:a0 + HEAD_DIM, :] = qkv_ref[:, hs].astype(F32).T.astype(BF16)
        qt_ref[a0 + HEAD_DIM:a0 + HEAD_DIM + FOX_AUG, :] = ones_rows
        v0 = h * FOX_VROWS
        vt_ref[v0:v0 + HEAD_DIM, :] = (
            qkv_ref[:, 2 * FOX_W + h * HEAD_DIM:2 * FOX_W + (h + 1) * HEAD_DIM].astype(F32).T.astype(BF16))
        vt_ref[v0 + HEAD_DIM:v0 + FOX_VROWS, :] = sum_rows
        hi, mid, lo = _split3(jnp.broadcast_to(bias[:, h:h + 1], (rows, V7X_LANES)))
        aug = jnp.where(lane == 0, hi.astype(F32), jnp.where(lane == 1, mid.astype(F32),
                                                              jnp.where(lane == 2, lo.astype(F32), 0.0)))
        ka_ref[:, a0:a0 + HEAD_DIM] = qkv_ref[:, FOX_W + h * HEAD_DIM:FOX_W + (h + 1) * HEAD_DIM]
        ka_ref[:, a0 + HEAD_DIM:a0 + HEAD_DIM + FOX_AUG] = aug.astype(BF16)


def _fox_prep(qkv, c_rows):
    s = qkv.shape[0]
    wide = N_FOX_HEADS * (HEAD_DIM + FOX_AUG)
    vrows = N_FOX_HEADS * FOX_VROWS
    return pl.pallas_call(
        _fox_prep_kernel,
        grid=(s // FOX_BLOCK,),
        in_specs=[pl.BlockSpec((FOX_BLOCK, 3 * FOX_W), lambda i: (i, 0)),
                  pl.BlockSpec((FOX_BLOCK, V7X_LANES), lambda i: (i, 0))],
        out_specs=[pl.BlockSpec((wide, FOX_BLOCK), lambda i: (0, i)),
                   pl.BlockSpec((FOX_BLOCK, wide), lambda i: (i, 0)),
                   pl.BlockSpec((vrows, FOX_BLOCK), lambda i: (0, i))],
        out_shape=[jax.ShapeDtypeStruct((wide, s), BF16), jax.ShapeDtypeStruct((s, wide), BF16),
                   jax.ShapeDtypeStruct((vrows, s), BF16)],
        compiler_params=_params(("parallel",)),
        name="fox_prep",
    )(qkv, c_rows)


def _fox_kernel(qi_ref, kj_ref, qt_ref, ka_ref, vt_ref, o_ref, m_ref, acc_ref, s_ref):
    t = pl.program_id(0)
    i = qi_ref[t]
    j = kj_ref[t]
    blk = ka_ref.shape[0]
    aw = HEAD_DIM + FOX_AUG

    @pl.when(j == 0)
    def _():
        m_ref[...] = jnp.full_like(m_ref, NEG)
        acc_ref[...] = jnp.zeros_like(acc_ref)

    def scores(h):
        return _dot(ka_ref[:, h * aw:(h + 1) * aw], qt_ref[h * aw:(h + 1) * aw, :])

    def step(diagonal):
        if diagonal:
            keep = lax.broadcasted_iota(jnp.int32, (blk, blk), 0) <= lax.broadcasted_iota(jnp.int32, (blk, blk), 1)
        s_ref[0] = scores(0)
        for h in range(N_FOX_HEADS):
            if h + 1 < N_FOX_HEADS:
                s_ref[(h + 1) % 2] = scores(h + 1)
            s = s_ref[h % 2]
            if diagonal:
                s = jnp.where(keep, s, NEG)
            m_prev = m_ref[h:h + 1, :]
            m_new = jnp.maximum(m_prev, jnp.max(s, axis=0, keepdims=True))
            p = jnp.exp2(s - m_new).astype(BF16)
            alpha = jnp.exp2(m_prev - m_new)
            m_ref[h:h + 1, :] = m_new
            vs = slice(h * FOX_VROWS, (h + 1) * FOX_VROWS)
            acc_ref[vs, :] = alpha * acc_ref[vs, :] + _dot(vt_ref[vs, :], p)

    @pl.when(j < i)
    def _():
        step(False)

    @pl.when(j == i)
    def _():
        step(True)
        for h in range(N_FOX_HEADS):
            v0 = h * FOX_VROWS
            out_t = acc_ref[v0:v0 + HEAD_DIM, :] / acc_ref[v0 + HEAD_DIM:v0 + HEAD_DIM + 1, :]
            o_ref[:, h * HEAD_DIM:(h + 1) * HEAD_DIM] = out_t.T.astype(o_ref.dtype)


def _fox_attention(qt, ka, vt):
    s = ka.shape[0]
    nb = s // FOX_BLOCK
    qi = jnp.asarray([i for i in range(nb) for _ in range(i + 1)], jnp.int32)
    kj = jnp.asarray([j for i in range(nb) for j in range(i + 1)], jnp.int32)
    wide = ka.shape[1]
    vrows = vt.shape[0]
    grid_spec = pltpu.PrefetchScalarGridSpec(
        num_scalar_prefetch=2,
        grid=(qi.shape[0],),
        in_specs=[
            pl.BlockSpec((wide, FOX_BLOCK), lambda t, qi, kj: (0, qi[t])),
            pl.BlockSpec((FOX_BLOCK, wide), lambda t, qi, kj: (kj[t], 0)),
            pl.BlockSpec((vrows, FOX_BLOCK), lambda t, qi, kj: (0, kj[t])),
        ],
        out_specs=pl.BlockSpec((FOX_BLOCK, FOX_W), lambda t, qi, kj: (qi[t], 0)),
        scratch_shapes=[
            pltpu.VMEM((N_FOX_HEADS, FOX_BLOCK), F32),
            pltpu.VMEM((vrows, FOX_BLOCK), F32),
            pltpu.VMEM((2, FOX_BLOCK, FOX_BLOCK), F32),
        ],
    )
    return pl.pallas_call(
        _fox_kernel,
        grid_spec=grid_spec,
        out_shape=jax.ShapeDtypeStruct((s, FOX_W), BF16),
        compiler_params=_params(("arbitrary",)),
        name="fox_attention",
    )(qi, kj, qt, ka, vt)


def _dil_kernel(q_ref, k_ref, v_ref, kp_ref, vp_ref, y_ref, o_scr, l_scr):
    has_prev_tile = pl.program_id(0) > 0
    b = DIL_BLOCK
    tile = q_ref.shape[0]
    jr = lax.broadcasted_iota(jnp.int32, (b, 2 * b), 0)
    mc = lax.broadcasted_iota(jnp.int32, (b, 2 * b), 1)
    band = (mc >= jr) & (mc <= jr + b)
    band_first = band & ((mc >= b) | has_prev_tile)

    def rows(ref, start, d):
        return ref[pl.ds(start, b), :] if d == 1 else ref[pl.ds(start, b, stride=d), :]

    for g, (_, d) in enumerate(DIL_PAIRS):
        for r in range(d):
            for n in range(tile // (b * d)):
                cur = r + n * b * d
                if n > 0:
                    k_prev, v_prev, keep = rows(k_ref, cur - b * d, d), rows(v_ref, cur - b * d, d), band
                else:
                    prev = tile - b * d + r
                    k_prev, v_prev, keep = rows(kp_ref, prev, d), rows(vp_ref, prev, d), band_first
                k2 = jnp.concatenate([k_prev, rows(k_ref, cur, d)], axis=0).astype(BF16)
                v2 = jnp.concatenate([v_prev, rows(v_ref, cur, d)], axis=0).astype(BF16)
                s = jnp.where(keep, _dot_nt(rows(q_ref, cur, d).astype(BF16), k2), NEG)
                m = jnp.max(s, axis=1, keepdims=True)
                p = jnp.exp(s - m)
                l = jnp.sum(p, axis=1, keepdims=True)
                o = _dot(p.astype(BF16), v2) / l
                lse = jnp.broadcast_to(m + jnp.log(l), (b, HEAD_DIM))
                if d == 1:
                    o_scr[g, pl.ds(cur, b), :] = o
                    l_scr[g, pl.ds(cur, b), :] = lse
                else:
                    o_scr[g, pl.ds(cur, b, stride=d), :] = o
                    l_scr[g, pl.ds(cur, b, stride=d), :] = lse

    l1, l2, l3 = l_scr[0], l_scr[1], l_scr[2]
    mx = jnp.maximum(jnp.maximum(l1, l2), l3)
    e1, e2, e3 = jnp.exp(l1 - mx), jnp.exp(l2 - mx), jnp.exp(l3 - mx)
    den = e1 + e2 + e3
    y_ref[...] = ((e1 / den) * o_scr[0] + (e2 / den) * o_scr[1] + (e3 / den) * o_scr[2]).astype(y_ref.dtype)


def _dilated_attention(qkv):
    s = qkv.shape[0]
    cur = lambda c: pl.BlockSpec((DIL_TILE, HEAD_DIM), lambda t, h: (t, c * N_DIL_HEADS + h))
    prev = lambda c: pl.BlockSpec((DIL_TILE, HEAD_DIM), lambda t, h: (jnp.maximum(t - 1, 0), c * N_DIL_HEADS + h))
    return pl.pallas_call(
        _dil_kernel,
        grid=(s // DIL_TILE, N_DIL_HEADS),
        in_specs=[cur(0), cur(1), cur(2), prev(1), prev(2)],
        out_specs=pl.BlockSpec((DIL_TILE, HEAD_DIM), lambda t, h: (t, h)),
        out_shape=jax.ShapeDtypeStruct((s, DIL_W), BF16),
        scratch_shapes=[pltpu.VMEM((len(DIL_PAIRS), DIL_TILE, HEAD_DIM), F32),
                        pltpu.VMEM((len(DIL_PAIRS), DIL_TILE, HEAD_DIM), F32)],
        compiler_params=_params(("parallel", "parallel")),
        name="dilated_attention",
    )(qkv, qkv, qkv, qkv, qkv)


def _attn_out_kernel(x_ref, ya_ref, yb_ref, w_ref, g_ref, b_ref, o_ref, ob_ref):
    y = _dot(jnp.concatenate([ya_ref[...], yb_ref[...]], axis=1), w_ref[...])
    out = _layer_norm(ALPHA * x_ref[...] + y, g_ref[...], b_ref[...])
    o_ref[...] = out
    ob_ref[...] = out.astype(BF16)


def _attn_out(x, ya, yb, w, g, b):
    s = x.shape[0]
    row = lambda width: pl.BlockSpec((OUT_ROWS, width), lambda i: (i, 0))
    const = lambda shape: pl.BlockSpec(shape, lambda i: (0, 0))
    return pl.pallas_call(
        _attn_out_kernel,
        grid=(s // OUT_ROWS,),
        in_specs=[row(D_MODEL), row(FOX_W), row(DIL_W),
                  const((FOX_W + DIL_W, D_MODEL)), const((1, D_MODEL)), const((1, D_MODEL))],
        out_specs=[row(D_MODEL), row(D_MODEL)],
        out_shape=[jax.ShapeDtypeStruct((s, D_MODEL), F32), jax.ShapeDtypeStruct((s, D_MODEL), BF16)],
        compiler_params=_params(("parallel",)),
        name="attn_out",
    )(x, ya, yb, w, g, b)


def _cmul(ar, ai, br, bi):
    return ar * br - ai * bi, ar * bi + ai * br


def _s5_prep_kernel(lrg_ref, lig_ref, ldtg_ref, lr_ref, li_ref, ldt_ref, br_ref, bi_ref,
                    pr_ref, pi_ref, qr_ref, qi_ref, bbr_ref, bbi_ref):
    def a_bar(lr, li, ldt):
        dt = jnp.exp(ldt)
        mag = jnp.exp(lr * dt)
        return mag * jnp.cos(li * dt), mag * jnp.sin(li * dt)

    lr, li = lr_ref[...], li_ref[...]
    ar, ai = a_bar(lr, li, ldt_ref[...])
    den = lr * lr + li * li
    cr = ((ar - 1.0) * lr + ai * li) / den
    ci = (ai * lr - (ar - 1.0) * li) / den
    bbr_ref[...], bbi_ref[...] = _cmul(cr, ci, br_ref[...], bi_ref[...])

    ar, ai = a_bar(lrg_ref[...], lig_ref[...], ldtg_ref[...])
    pr, pi = ar, ai
    pr_ref[0], pi_ref[0] = pr, pi
    for k in range(1, S5_SEG):
        pr, pi = _cmul(pr, pi, ar, ai)
        pr_ref[k], pi_ref[k] = pr, pi
    sr, si = pr, pi
    qr_ref[0], qi_ref[0] = sr, si
    for k in range(1, V7X_SUBLANES):
        pr, pi = _cmul(pr, pi, sr, si)
        qr_ref[k], qi_ref[k] = pr, pi


def _s5_prep(lam_re, lam_im, log_dt, b_re, b_im):
    ldt = jnp.broadcast_to(log_dt[:, None], lam_re.shape)
    rep = lambda v: jnp.repeat(v, S5_GROUP, axis=0)
    rows = S5_GROUPS * S5_GROUP
    bt = lambda v: v.transpose(0, 2, 1).reshape(rows, S5_STATE)
    mat = jax.ShapeDtypeStruct((rows, S5_STATE), F32)
    pw = jax.ShapeDtypeStruct((S5_SEG, S5_GROUPS, S5_STATE), F32)
    qw = jax.ShapeDtypeStruct((V7X_SUBLANES, S5_GROUPS, S5_STATE), F32)
    return pl.pallas_call(_s5_prep_kernel, out_shape=[pw, pw, qw, qw, mat, mat], name="s5_prep")(
        lam_re, lam_im, ldt, rep(lam_re), rep(lam_im), rep(ldt), bt(b_re), bt(b_im))


def _s5_core_kernel(u_ref, bblk_ref, cblk_ref, pr_ref, pi_ref, qr_ref, qi_ref, d_ref, z_ref,
                    up_ref, sr_ref, si_ref, cr_ref, ci_ref):
    @pl.when(pl.program_id(0) == 0)
    def _():
        cr_ref[...] = jnp.zeros_like(cr_ref)
        ci_ref[...] = jnp.zeros_like(ci_ref)

    sl = V7X_SUBLANES
    rows = u_ref.shape[0]
    ri = lax.broadcasted_iota(jnp.int32, (rows, rows), 0)
    ci = lax.broadcasted_iota(jnp.int32, (rows, rows), 1)
    perm = ((ri // sl == ci % S5_SEG) & (ri % sl == ci // S5_SEG)).astype(BF16)
    unperm = ((ci // sl == ri % S5_SEG) & (ci % sl == ri // S5_SEG)).astype(BF16)
    hi, mid, lo = _split3(u_ref[...])
    up_ref[...] = _dot(perm, hi) + _dot(perm, mid) + _dot(perm, lo)

    sub = lax.broadcasted_iota(jnp.int32, (sl, S5_LANES), 0)
    width = S5_GB * S5_GROUP
    ys = []
    for jb in range(S5_NB):
        bu = _dot(up_ref[:, jb * width:(jb + 1) * width].astype(BF16), bblk_ref[jb])
        a_re, a_im = pr_ref[jb, 0:1, :], pi_ref[jb, 0:1, :]
        xr, xi = bu[0:sl, :S5_LANES], bu[0:sl, S5_LANES:]
        sr_ref[0:sl, :], si_ref[0:sl, :] = xr, xi
        for j in range(1, S5_SEG):
            tr, ti = _cmul(a_re, a_im, xr, xi)
            xr, xi = tr + bu[j * sl:(j + 1) * sl, :S5_LANES], ti + bu[j * sl:(j + 1) * sl, S5_LANES:]
            sr_ref[j * sl:(j + 1) * sl, :], si_ref[j * sl:(j + 1) * sl, :] = xr, xi

        q_re, q_im = qr_ref[jb], qi_ref[jb]
        er, ei = xr, xi
        for k in (1, 2, 4):
            tr = jnp.where(sub >= k, pltpu.roll(er, k, 0), 0.0)
            ti = jnp.where(sub >= k, pltpu.roll(ei, k, 0), 0.0)
            mr, mi = _cmul(q_re[k - 1:k, :], q_im[k - 1:k, :], tr, ti)
            er, ei = er + mr, ei + mi
        in_re, in_im = cr_ref[jb:jb + 1, :], ci_ref[jb:jb + 1, :]
        mr, mi = _cmul(q_re, q_im, in_re, in_im)
        er, ei = er + mr, ei + mi
        cr_ref[jb:jb + 1, :], ci_ref[jb:jb + 1, :] = er[sl - 1:, :], ei[sl - 1:, :]
        cin_re = jnp.where(sub >= 1, pltpu.roll(er, 1, 0), in_re)
        cin_im = jnp.where(sub >= 1, pltpu.roll(ei, 1, 0), in_im)
        fr, fi = _cmul(pr_ref[jb][:, None, :], pi_ref[jb][:, None, :], cin_re[None], cin_im[None])
        shape3 = (S5_SEG, sl, S5_LANES)
        state = jnp.concatenate([(sr_ref[...].reshape(shape3) + fr).reshape(-1, S5_LANES),
                                 (si_ref[...].reshape(shape3) + fi).reshape(-1, S5_LANES)], axis=1)
        ys.append(_dot(state.astype(BF16), cblk_ref[jb]))
    y = jnp.concatenate(ys, axis=1) + d_ref[...] * up_ref[...]
    z = 0.5 * y * (1.0 + jnp.tanh(math.sqrt(2.0 / math.pi) * (y + 0.044715 * (y * y * y))))
    z_ref[...] = _dot(unperm, z.astype(BF16)).astype(z_ref.dtype)


def _s5_core(u, bblk, cblk, p_re, p_im, q_re, q_im, d_row):
    s = u.shape[0]
    const = lambda shape: pl.BlockSpec(shape, lambda t: (0,) * len(shape))
    return pl.pallas_call(
        _s5_core_kernel,
        grid=(s // S5_ROWS,),
        in_specs=[pl.BlockSpec((S5_ROWS, S5_WIDTH), lambda t: (t, 0)),
                  const(bblk.shape), const(cblk.shape), const(p_re.shape), const(p_im.shape),
                  const(q_re.shape), const(q_im.shape), const((1, S5_WIDTH))],
        out_specs=pl.BlockSpec((S5_ROWS, S5_WIDTH), lambda t: (t, 0)),
        out_shape=jax.ShapeDtypeStruct((s, S5_WIDTH), BF16),
        scratch_shapes=[pltpu.VMEM((S5_ROWS, S5_WIDTH), F32),
                        pltpu.VMEM((S5_ROWS, S5_LANES), F32), pltpu.VMEM((S5_ROWS, S5_LANES), F32),
                        pltpu.VMEM((V7X_SUBLANES, S5_LANES), F32), pltpu.VMEM((V7X_SUBLANES, S5_LANES), F32)],
        compiler_params=_params(("arbitrary",)),
        name="s5_core",
    )(u, bblk, cblk, p_re, p_im, q_re, q_im, d_row)


def _block_diag(blocks):
    nb, gb, r, c = blocks.shape
    eye = jnp.eye(gb, dtype=blocks.dtype)
    return jnp.einsum("ngrc,gk->ngrkc", blocks, eye).reshape(nb, gb * r, gb * c)


def _glu_kernel(x_ref, z_ref, wo_ref, wg_ref, g_ref, b_ref, o_ref, ob_ref):
    z = z_ref[...]
    y = _dot(z, wo_ref[...]) * jax.nn.sigmoid(_dot(z, wg_ref[...]))
    out = _layer_norm(ALPHA * x_ref[...] + y, g_ref[...], b_ref[...])
    o_ref[...] = out
    ob_ref[...] = out.astype(BF16)


def _glu(x, z, wo, wg, g, b):
    s = x.shape[0]
    row = lambda width: pl.BlockSpec((GLU_ROWS, width), lambda i: (i, 0))
    const = lambda shape: pl.BlockSpec(shape, lambda i: (0, 0))
    return pl.pallas_call(
        _glu_kernel,
        grid=(s // GLU_ROWS,),
        in_specs=[row(D_MODEL), row(S5_WIDTH), const((S5_WIDTH, D_MODEL)), const((S5_WIDTH, D_MODEL)),
                  const((1, D_MODEL)), const((1, D_MODEL))],
        out_specs=[row(D_MODEL), row(D_MODEL)],
        out_shape=[jax.ShapeDtypeStruct((s, D_MODEL), F32), jax.ShapeDtypeStruct((s, D_MODEL), BF16)],
        compiler_params=_params(("parallel",)),
        name="glu",
    )(x, z, wo, wg, g, b)


def _attention_mixer(x, xb, w_in, b_f, w_out, g, b):
    scale = HEAD_DIM ** -0.5
    fox_q, fox_k, fox_v = (w_in[:, n * FOX_W:(n + 1) * FOX_W] for n in range(3))
    w_f = w_in[:, 3 * FOX_W:3 * FOX_W + N_FOX_HEADS]
    d0 = 3 * FOX_W + N_FOX_HEADS
    dil_q, dil_k, dil_v = (w_in[:, d0 + n * DIL_W:d0 + (n + 1) * DIL_W] for n in range(3))
    w_fox = jnp.concatenate([fox_q * (scale * LOG2E), fox_k, fox_v], axis=1).astype(BF16)
    w_dil = jnp.concatenate([dil_q * scale, dil_k, dil_v], axis=1).astype(BF16)
    w_f = jnp.pad(w_f, ((0, 0), (0, V7X_LANES - N_FOX_HEADS))).astype(BF16)
    qkv_fox, f_logit = _attn_in(xb, w_fox, w_f)
    c_rows = _forget_cumsum(f_logit, jnp.pad(b_f, (0, V7X_LANES - N_FOX_HEADS))[None, :])
    ya = _fox_attention(*_fox_prep(qkv_fox, c_rows))
    yb = _dilated_attention(_matmul(xb, w_dil, F32))
    return _attn_out(x, ya, yb, w_out.astype(BF16), g, b)


def _s5_mixer(x, xb, w_in, lam_re, lam_im, log_dt, b_re, b_im, c_re, c_im, d_skip, w_glu_out, w_glu_gate, g, b):
    u = _matmul(xb, w_in.astype(BF16), F32)
    p_re, p_im, q_re, q_im, bb_re, bb_im = _s5_prep(lam_re, lam_im, log_dt, b_re, b_im)
    lanes = lambda p: p.reshape(p.shape[0], S5_NB, S5_LANES).transpose(1, 0, 2)
    blk = lambda v: v.reshape(S5_NB, S5_GB, S5_GROUP, S5_STATE)
    bblk = jnp.concatenate([_block_diag(blk(bb_re)), _block_diag(blk(bb_im))], axis=2).astype(BF16)
    ct = lambda v: v.transpose(0, 2, 1).reshape(S5_NB, S5_GB, S5_STATE, S5_GROUP)
    cblk = jnp.concatenate([_block_diag(ct(c_re)), -_block_diag(ct(c_im))], axis=1).astype(BF16)
    z = _s5_core(u, bblk, cblk, lanes(p_re), lanes(p_im), lanes(q_re), lanes(q_im), d_skip[None, :])
    return _glu(x, z, w_glu_out.astype(BF16), w_glu_gate.astype(BF16), g, b)


def kernel(x, ffn1_w_gate, ffn1_w_up, ffn1_w_down, ffn2_w_gate, ffn2_w_up, ffn2_w_down, ln_gain, ln_bias,
           attn_w_in, attn_b_f, attn_w_out, s5_w_in, s5_lambda_re, s5_lambda_im, s5_log_dt, s5_b_re, s5_b_im,
           s5_c_re, s5_c_im, s5_d, s5_w_glu_out, s5_w_glu_gate):
    batch, seq, _ = x.shape
    assert batch == 1
    h = x.reshape(seq, D_MODEL)
    hb = h.astype(BF16)
    ln = lambda i, k: (ln_gain[i, k][None, :], ln_bias[i, k][None, :])
    for i in range(DEPTH):
        h, hb = _ffn(h, hb, *_ffn_weights_bf16(ffn1_w_gate, ffn1_w_up, ffn1_w_down, i), *ln(i, 0))
        j = i // 2
        if i % 2 == 0:
            h, hb = _attention_mixer(h, hb, attn_w_in[j], attn_b_f[j], attn_w_out[j], *ln(i, 1))
        else:
            h, hb = _s5_mixer(h, hb, s5_w_in[j], s5_lambda_re[j], s5_lambda_im[j], s5_log_dt[j], s5_b_re[j],
                              s5_b_im[j], s5_c_re[j], s5_c_im[j], s5_d[j], s5_w_glu_out[j], s5_w_glu_gate[j],
                              *ln(i, 1))
        h, hb = _ffn(h, hb, *_ffn_weights_bf16(ffn2_w_gate, ffn2_w_up, ffn2_w_down, i), *ln(i, 2))
    return h.reshape(batch, seq, D_MODEL)
```

```python
import functools
import math

import jax
import jax.numpy as jnp
from jax import lax
from jax.experimental import pallas as pl
from jax.experimental.pallas import tpu as pltpu

D_MODEL = 2048
DEPTH = 2
HEAD_DIM = 128
N_FOX_HEADS = 8
N_DIL_HEADS = 8
DIL_PAIRS = ((128, 1), (512, 4), (2048, 16))
DIL_BLOCK = 128
D_FF = 5632
S5_GROUP = 16
S5_WIDTH = 1024
S5_GROUPS = S5_WIDTH // S5_GROUP
S5_STATE = 64
ALPHA = (2.0 * DEPTH) ** 0.25
LN_EPS = 1e-5
FOX_W = N_FOX_HEADS * HEAD_DIM
DIL_W = N_DIL_HEADS * HEAD_DIM

F32 = jnp.float32
BF16 = jnp.bfloat16
NEG = -1e30
LOG2E = math.log2(math.e)
FOX_AUG = 128
FOX_VROWS = HEAD_DIM + 16

V7X_LANES = 128
V7X_SUBLANES = 8
V7X_VMEM_LIMIT_BYTES = 56 * 1024 * 1024

FFN_ROWS = 512
FFN_COLS = 512
FFN_CHUNKS = D_FF // FFN_COLS
FFN_RING_SLOTS = 3
assert (FFN_CHUNKS + 1) % FFN_RING_SLOTS == 0
SIDE_UP_ROWS = 64
SIDE_DOWN_ROWS = 16
SIDE_UNITS_PER_CHUNK = D_MODEL // SIDE_UP_ROWS
SIDE_UNITS = FFN_CHUNKS * SIDE_UNITS_PER_CHUNK
assert SIDE_UNITS * SIDE_DOWN_ROWS == D_FF
MM_ROWS = 1024
MM_COLS = 1024
FOX_BLOCK = 1024
DIL_TILE = 2048
OUT_ROWS = 512
GLU_ROWS = 512
CUMSUM_ROWS = 512
CAST_STEPS = 8
S5_ROWS = 256
S5_SEG = S5_ROWS // V7X_SUBLANES
S5_GB = 16
S5_NB = S5_GROUPS // S5_GB
S5_LANES = S5_GB * S5_STATE


def _params(sem):
    return pltpu.CompilerParams(dimension_semantics=sem, vmem_limit_bytes=V7X_VMEM_LIMIT_BYTES)


def _layer_norm(v, g, b):
    mu = jnp.mean(v, axis=-1, keepdims=True)
    d = v - mu
    var = jnp.mean(d * d, axis=-1, keepdims=True)
    return d * lax.rsqrt(var + LN_EPS) * g + b


def _dot(a, b):
    return jnp.dot(a, b, preferred_element_type=F32)


def _dot_nt(a, b):
    return lax.dot_general(a, b, (((1,), (1,)), ((), ())), preferred_element_type=F32)


def _ffn_kernel(*refs, next_layer):
    if next_layer is None:
        (x_ref, xb_ref, wg_hbm, wu_hbm, wd_hbm, g_ref, b_ref, o_ref, ob_ref,
         h_ref, wg_buf, wu_buf, wd_buf, sem) = refs
    else:
        (x_ref, xb_ref, wg_hbm, wu_hbm, wd_hbm, g_ref, b_ref, ng_hbm, nu_hbm, nd_hbm,
         o_ref, ob_ref, og_hbm, ou_hbm, od_hbm,
         h_ref, wg_buf, wu_buf, wd_buf, sem, cin_up, cin_dn, cout_up, cout_dn, csem) = refs
    i = pl.program_id(0)
    last_tile = pl.num_programs(0) - 1
    chunks = FFN_CHUNKS
    nsub = chunks + 1

    def ring(sub, slot, op):
        @pl.when(sub < chunks)
        def _():
            c = jnp.minimum(sub, chunks - 1)
            op(pltpu.make_async_copy(wg_hbm.at[c], wg_buf.at[slot], sem.at[0, slot]))
            op(pltpu.make_async_copy(wu_hbm.at[c], wu_buf.at[slot], sem.at[1, slot]))

        @pl.when(sub >= 1)
        def _():
            c = jnp.maximum(sub - 1, 0)
            op(pltpu.make_async_copy(wd_hbm.at[c], wd_buf.at[slot], sem.at[2, slot]))

    start = lambda cp: cp.start()
    wait = lambda cp: cp.wait()

    def side_unit(u, op, inbound):
        slot = u % 2
        c, rb = u // SIDE_UNITS_PER_CHUNK, u % SIDE_UNITS_PER_CHUNK
        up_rows = pl.ds(pl.multiple_of(rb * SIDE_UP_ROWS, SIDE_UP_ROWS), SIDE_UP_ROWS)
        dn_rows = pl.ds(pl.multiple_of(u * SIDE_DOWN_ROWS, SIDE_DOWN_ROWS), SIDE_DOWN_ROWS)
        cols = pl.ds(pl.multiple_of(c * FFN_COLS, FFN_COLS), FFN_COLS)
        if inbound:
            op(pltpu.make_async_copy(ng_hbm.at[next_layer, up_rows, cols], cin_up.at[slot, 0], csem.at[0, slot, 0]))
            op(pltpu.make_async_copy(nu_hbm.at[next_layer, up_rows, cols], cin_up.at[slot, 1], csem.at[0, slot, 1]))
            op(pltpu.make_async_copy(nd_hbm.at[next_layer, dn_rows, :], cin_dn.at[slot], csem.at[0, slot, 2]))
        else:
            op(pltpu.make_async_copy(cout_up.at[slot, 0], og_hbm.at[c, up_rows, :], csem.at[1, slot, 0]))
            op(pltpu.make_async_copy(cout_up.at[slot, 1], ou_hbm.at[c, up_rows, :], csem.at[1, slot, 1]))
            op(pltpu.make_async_copy(cout_dn.at[slot], od_hbm.at[dn_rows, :], csem.at[1, slot, 2]))

    def side_dma(j):
        count = i * nsub + j
        for lag, op, inbound in ((3, wait, False), (2, start, False), (1, wait, True), (0, start, True)):
            u = count - lag

            @pl.when((u >= 0) & (u < SIDE_UNITS))
            def _():
                side_unit(u, op, inbound)

    def side_cast(j):
        slot = (i * nsub + j + 1) % 2
        cout_up[slot] = cin_up[slot].astype(BF16)
        cout_dn[slot] = cin_dn[slot].astype(BF16)

    @pl.when(i == 0)
    def _():
        ring(jnp.int32(0), 0, start)
        ring(jnp.int32(1), 1, start)
        if next_layer is not None:
            cin_up[...] = jnp.zeros_like(cin_up)
            cin_dn[...] = jnp.zeros_like(cin_dn)

    def fetch_ahead_and_wait(j):
        ahead = j + 2
        wraps = ahead >= nsub

        @pl.when(jnp.logical_or(jnp.logical_not(wraps), i < last_tile))
        def _():
            ring(jnp.where(wraps, ahead - nsub, ahead), ahead % FFN_RING_SLOTS, start)

        ring(j, j % FFN_RING_SLOTS, wait)
        if next_layer is not None:
            side_dma(j)

    def hidden(slot):
        xb = xb_ref[...]
        gate = _dot(xb, wg_buf[slot])
        up = _dot(xb, wu_buf[slot])
        return (gate * jax.nn.sigmoid(gate) * up).astype(BF16)

    def down(j, slot):
        return _dot(h_ref[(j - 1) % 2], wd_buf[slot])

    def side(j):
        if next_layer is not None:
            side_cast(j)

    fetch_ahead_and_wait(jnp.int32(0))
    side(0)
    h_ref[0] = hidden(0)

    fetch_ahead_and_wait(jnp.int32(1))
    side(1)
    o_ref[...] = down(1, 1)
    h_ref[1] = hidden(1)

    def body(j, carry):
        fetch_ahead_and_wait(j)
        side(j)
        slot = j % FFN_RING_SLOTS
        o_ref[...] += down(j, slot)
        h_ref[j % 2] = hidden(slot)
        return carry

    lax.fori_loop(2, chunks, body, 0)

    fetch_ahead_and_wait(jnp.int32(chunks))
    side(chunks)
    acc = o_ref[...] + down(chunks, chunks % FFN_RING_SLOTS)
    y = _layer_norm(ALPHA * x_ref[...] + 0.5 * acc, g_ref[...], b_ref[...])
    o_ref[...] = y
    ob_ref[...] = y.astype(BF16)


def _ffn(x, xb, weights, g, b, next_weights=None, next_layer=None):
    s = x.shape[0]
    row = lambda i: (i, 0)
    hbm = pl.BlockSpec(memory_space=pl.ANY)
    in_specs = [
        pl.BlockSpec((FFN_ROWS, D_MODEL), row),
        pl.BlockSpec((FFN_ROWS, D_MODEL), row),
        hbm, hbm, hbm,
        pl.BlockSpec((1, D_MODEL), lambda i: (0, 0)),
        pl.BlockSpec((1, D_MODEL), lambda i: (0, 0)),
    ]
    out_specs = [pl.BlockSpec((FFN_ROWS, D_MODEL), row), pl.BlockSpec((FFN_ROWS, D_MODEL), row)]
    out_shape = [jax.ShapeDtypeStruct((s, D_MODEL), F32), jax.ShapeDtypeStruct((s, D_MODEL), BF16)]
    scratch = [
        pltpu.VMEM((2, FFN_ROWS, FFN_COLS), BF16),
        pltpu.VMEM((FFN_RING_SLOTS, D_MODEL, FFN_COLS), BF16),
        pltpu.VMEM((FFN_RING_SLOTS, D_MODEL, FFN_COLS), BF16),
        pltpu.VMEM((FFN_RING_SLOTS, FFN_COLS, D_MODEL), BF16),
        pltpu.SemaphoreType.DMA((3, FFN_RING_SLOTS)),
    ]
    operands = [x, xb, *weights, g, b]
    if next_layer is not None:
        assert (s // FFN_ROWS) * (FFN_CHUNKS + 1) >= SIDE_UNITS + 3
        in_specs += [hbm, hbm, hbm]
        out_specs += [hbm, hbm, hbm]
        up_shape = jax.ShapeDtypeStruct((FFN_CHUNKS, D_MODEL, FFN_COLS), BF16)
        out_shape += [up_shape, up_shape, jax.ShapeDtypeStruct((D_FF, D_MODEL), BF16)]
        scratch += [
            pltpu.VMEM((2, 2, SIDE_UP_ROWS, FFN_COLS), F32), pltpu.VMEM((2, SIDE_DOWN_ROWS, D_MODEL), F32),
            pltpu.VMEM((2, 2, SIDE_UP_ROWS, FFN_COLS), BF16), pltpu.VMEM((2, SIDE_DOWN_ROWS, D_MODEL), BF16),
            pltpu.SemaphoreType.DMA((2, 2, 3)),
        ]
        operands += list(next_weights)
    outs = pl.pallas_call(
        functools.partial(_ffn_kernel, next_layer=next_layer),
        grid=(s // FFN_ROWS,),
        in_specs=in_specs,
        out_specs=out_specs,
        out_shape=out_shape,
        scratch_shapes=scratch,
        compiler_params=_params(("arbitrary",)),
        name="ffn",
    )(*operands)
    if next_layer is None:
        return outs[0], outs[1], None
    return outs[0], outs[1], (outs[2], outs[3], outs[4].reshape(FFN_CHUNKS, FFN_COLS, D_MODEL))


def _cast_kernel(wg_ref, wu_ref, wd_ref, og_ref, ou_ref, od_ref):
    for c in range(FFN_CHUNKS):
        cs = slice(c * FFN_COLS, (c + 1) * FFN_COLS)
        og_ref[c] = wg_ref[:, cs].astype(BF16)
        ou_ref[c] = wu_ref[:, cs].astype(BF16)
    od_ref[...] = wd_ref[...].astype(BF16)


def _ffn_weights_bf16(w_gate, w_up, w_down, layer):
    steps = CAST_STEPS
    up_rows, down_rows = D_MODEL // steps, D_FF // steps
    up_in = pl.BlockSpec((None, up_rows, D_FF), lambda r: (layer, r, 0))
    up_out = pl.BlockSpec((FFN_CHUNKS, up_rows, FFN_COLS), lambda r: (0, r, 0))
    up_shape = jax.ShapeDtypeStruct((FFN_CHUNKS, D_MODEL, FFN_COLS), BF16)
    wg, wu, wd = pl.pallas_call(
        _cast_kernel,
        grid=(steps,),
        in_specs=[up_in, up_in, pl.BlockSpec((None, down_rows, D_MODEL), lambda r: (layer, r, 0))],
        out_specs=[up_out, up_out, pl.BlockSpec((down_rows, D_MODEL), lambda r: (r, 0))],
        out_shape=[up_shape, up_shape, jax.ShapeDtypeStruct((D_FF, D_MODEL), BF16)],
        compiler_params=_params(("parallel",)),
        name="ffn_weights_bf16",
    )(w_gate, w_up, w_down)
    return wg, wu, wd.reshape(FFN_CHUNKS, FFN_COLS, D_MODEL)


def _attn_in_kernel(x_ref, w_ref, wf_ref, o_ref, f_ref):
    x = x_ref[...]
    o_ref[...] = _dot(x, w_ref[...]).astype(o_ref.dtype)

    @pl.when(pl.program_id(1) == 0)
    def _():
        f_ref[...] = _dot(x, wf_ref[...])


def _attn_in(xb, w, wf):
    s, n = xb.shape[0], w.shape[1]
    return pl.pallas_call(
        _attn_in_kernel,
        grid=(s // MM_ROWS, n // MM_COLS),
        in_specs=[
            pl.BlockSpec((MM_ROWS, D_MODEL), lambda i, j: (i, 0)),
            pl.BlockSpec((D_MODEL, MM_COLS), lambda i, j: (0, j)),
            pl.BlockSpec((D_MODEL, V7X_LANES), lambda i, j: (0, 0)),
        ],
        out_specs=[
            pl.BlockSpec((MM_ROWS, MM_COLS), lambda i, j: (i, j)),
            pl.BlockSpec((MM_ROWS, V7X_LANES), lambda i, j: (i, 0)),
        ],
        out_shape=[jax.ShapeDtypeStruct((s, n), BF16), jax.ShapeDtypeStruct((s, V7X_LANES), F32)],
        compiler_params=_params(("parallel", "arbitrary")),
        name="attn_in",
    )(xb, w, wf)


def _mm_kernel(x_ref, w_ref, o_ref):
    o_ref[...] = _dot(x_ref[...], w_ref[...]).astype(o_ref.dtype)


def _matmul(xb, w, out_dtype):
    s, n = xb.shape[0], w.shape[1]
    return pl.pallas_call(
        _mm_kernel,
        grid=(s // MM_ROWS, n // MM_COLS),
        in_specs=[
            pl.BlockSpec((MM_ROWS, xb.shape[1]), lambda i, j: (i, 0)),
            pl.BlockSpec((xb.shape[1], MM_COLS), lambda i, j: (0, j)),
        ],
        out_specs=pl.BlockSpec((MM_ROWS, MM_COLS), lambda i, j: (i, j)),
        out_shape=jax.ShapeDtypeStruct((s, n), out_dtype),
        compiler_params=_params(("parallel", "parallel")),
        name="matmul",
    )(xb, w)


def _split3(v):
    hi = v.astype(BF16)
    r1 = v - hi.astype(F32)
    mid = r1.astype(BF16)
    lo = (r1 - mid.astype(F32)).astype(BF16)
    return hi, mid, lo


def _cumsum_kernel(f_ref, bf_ref, c_ref, carry_ref):
    @pl.when(pl.program_id(0) == 0)
    def _():
        carry_ref[...] = jnp.zeros_like(carry_ref)

    z = f_ref[...] + bf_ref[...]
    log_f = jnp.minimum(z, 0.0) - jnp.log1p(jnp.exp(-jnp.abs(z)))
    n = z.shape[0]
    tri = (lax.broadcasted_iota(jnp.int32, (n, n), 0) >= lax.broadcasted_iota(jnp.int32, (n, n), 1)).astype(BF16)
    hi, mid, lo = _split3(log_f)
    c = _dot(tri, hi) + _dot(tri, mid) + _dot(tri, lo) + carry_ref[0:1, :]
    c_ref[...] = c
    carry_ref[...] = jnp.broadcast_to(c[n - 1:n, :], carry_ref.shape)


def _forget_cumsum(f_logit, b_f_row):
    s = f_logit.shape[0]
    return pl.pallas_call(
        _cumsum_kernel,
        grid=(s // CUMSUM_ROWS,),
        in_specs=[
            pl.BlockSpec((CUMSUM_ROWS, V7X_LANES), lambda i: (i, 0)),
            pl.BlockSpec((1, V7X_LANES), lambda i: (0, 0)),
        ],
        out_specs=pl.BlockSpec((CUMSUM_ROWS, V7X_LANES), lambda i: (i, 0)),
        out_shape=jax.ShapeDtypeStruct((s, V7X_LANES), F32),
        scratch_shapes=[pltpu.VMEM((V7X_SUBLANES, V7X_LANES), F32)],
        compiler_params=_params(("arbitrary",)),
        name="forget_cumsum",
    )(f_logit, b_f_row)


def _fox_prep_kernel(qkv_ref, c_ref, qt_ref, ka_ref, vt_ref):
    rows = qkv_ref.shape[0]
    lane = lax.broadcasted_iota(jnp.int32, (rows, V7X_LANES), 1)
    bias = c_ref[...] * (-LOG2E)
    ones_rows = (lax.broadcasted_iota(jnp.int32, (FOX_AUG, rows), 0) < 3).astype(BF16)
    sum_rows = (lax.broadcasted_iota(jnp.int32, (FOX_VROWS - HEAD_DIM, rows), 0) < 1).astype(BF16)
    for h in range(N_FOX_HEADS):
        hs = slice(h * HEAD_DIM, (h + 1) * HEAD_DIM)
        a0 = h * (HEAD_DIM + FOX_AUG)
        qt_ref[a0:a0 + HEAD_DIM, :] = qkv_ref[:, hs].astype(F32).T.astype(BF16)
        qt_ref[a0 + HEAD_DIM:a0 + HEAD_DIM + FOX_AUG, :] = ones_rows
        v0 = h * FOX_VROWS
        vt_ref[v0:v0 + HEAD_DIM, :] = (
            qkv_ref[:, 2 * FOX_W + h * HEAD_DIM:2 * FOX_W + (h + 1) * HEAD_DIM].astype(F32).T.astype(BF16))
        vt_ref[v0 + HEAD_DIM:v0 + FOX_VROWS, :] = sum_rows
        hi, mid, lo = _split3(jnp.broadcast_to(bias[:, h:h + 1], (rows, V7X_LANES)))
        aug = jnp.where(lane == 0, hi.astype(F32), jnp.where(lane == 1, mid.astype(F32),
                                                              jnp.where(lane == 2, lo.astype(F32), 0.0)))
        ka_ref[:, a0:a0 + HEAD_DIM] = qkv_ref[:, FOX_W + h * HEAD_DIM:FOX_W + (h + 1) * HEAD_DIM]
        ka_ref[:, a0 + HEAD_DIM:a0 + HEAD_DIM + FOX_AUG] = aug.astype(BF16)


def _fox_prep(qkv, c_rows):
    s = qkv.shape[0]
    wide = N_FOX_HEADS * (HEAD_DIM + FOX_AUG)
    vrows = N_FOX_HEADS * FOX_VROWS
    return pl.pallas_call(
        _fox_prep_kernel,
        grid=(s // FOX_BLOCK,),
        in_specs=[pl.BlockSpec((FOX_BLOCK, 3 * FOX_W), lambda i: (i, 0)),
                  pl.BlockSpec((FOX_BLOCK, V7X_LANES), lambda i: (i, 0))],
        out_specs=[pl.BlockSpec((wide, FOX_BLOCK), lambda i: (0, i)),
                   pl.BlockSpec((FOX_BLOCK, wide), lambda i: (i, 0)),
                   pl.BlockSpec((vrows, FOX_BLOCK), lambda i: (0, i))],
        out_shape=[jax.ShapeDtypeStruct((wide, s), BF16), jax.ShapeDtypeStruct((s, wide), BF16),
                   jax.ShapeDtypeStruct((vrows, s), BF16)],
        compiler_params=_params(("parallel",)),
        name="fox_prep",
    )(qkv, c_rows)


def _fox_kernel(qi_ref, kj_ref, qt_ref, ka_ref, vt_ref, o_ref, m_ref, acc_ref, s_ref):
    t = pl.program_id(0)
    i = qi_ref[t]
    j = kj_ref[t]
    blk = ka_ref.shape[0]
    aw = HEAD_DIM + FOX_AUG

    @pl.when(j == 0)
    def _():
        m_ref[...] = jnp.full_like(m_ref, NEG)
        acc_ref[...] = jnp.zeros_like(acc_ref)

    def scores(h):
        return _dot(ka_ref[:, h * aw:(h + 1) * aw], qt_ref[h * aw:(h + 1) * aw, :])

    def step(diagonal):
        if diagonal:
            keep = lax.broadcasted_iota(jnp.int32, (blk, blk), 0) <= lax.broadcasted_iota(jnp.int32, (blk, blk), 1)
        s_ref[0] = scores(0)
        for h in range(N_FOX_HEADS):
            if h + 1 < N_FOX_HEADS:
                s_ref[(h + 1) % 2] = scores(h + 1)
            s = s_ref[h % 2]
            if diagonal:
                s = jnp.where(keep, s, NEG)
            m_prev = m_ref[h:h + 1, :]
            m_new = jnp.maximum(m_prev, jnp.max(s, axis=0, keepdims=True))
            p = jnp.exp2(s - m_new).astype(BF16)
            alpha = jnp.exp2(m_prev - m_new)
            m_ref[h:h + 1, :] = m_new
            vs = slice(h * FOX_VROWS, (h + 1) * FOX_VROWS)
            acc_ref[vs, :] = alpha * acc_ref[vs, :] + _dot(vt_ref[vs, :], p)

    @pl.when(j < i)
    def _():
        step(False)

    @pl.when(j == i)
    def _():
        step(True)
        for h in range(N_FOX_HEADS):
            v0 = h * FOX_VROWS
            out_t = acc_ref[v0:v0 + HEAD_DIM, :] / acc_ref[v0 + HEAD_DIM:v0 + HEAD_DIM + 1, :]
            o_ref[:, h * HEAD_DIM:(h + 1) * HEAD_DIM] = out_t.T.astype(o_ref.dtype)


def _fox_attention(qt, ka, vt):
    s = ka.shape[0]
    nb = s // FOX_BLOCK
    qi = jnp.asarray([i for i in range(nb) for _ in range(i + 1)], jnp.int32)
    kj = jnp.asarray([j for i in range(nb) for j in range(i + 1)], jnp.int32)
    wide = ka.shape[1]
    vrows = vt.shape[0]
    grid_spec = pltpu.PrefetchScalarGridSpec(
        num_scalar_prefetch=2,
        grid=(qi.shape[0],),
        in_specs=[
            pl.BlockSpec((wide, FOX_BLOCK), lambda t, qi, kj: (0, qi[t])),
            pl.BlockSpec((FOX_BLOCK, wide), lambda t, qi, kj: (kj[t], 0)),
            pl.BlockSpec((vrows, FOX_BLOCK), lambda t, qi, kj: (0, kj[t])),
        ],
        out_specs=pl.BlockSpec((FOX_BLOCK, FOX_W), lambda t, qi, kj: (qi[t], 0)),
        scratch_shapes=[
            pltpu.VMEM((N_FOX_HEADS, FOX_BLOCK), F32),
            pltpu.VMEM((vrows, FOX_BLOCK), F32),
            pltpu.VMEM((2, FOX_BLOCK, FOX_BLOCK), F32),
        ],
    )
    return pl.pallas_call(
        _fox_kernel,
        grid_spec=grid_spec,
        out_shape=jax.ShapeDtypeStruct((s, FOX_W), BF16),
        compiler_params=_params(("arbitrary",)),
        name="fox_attention",
    )(qi, kj, qt, ka, vt)


def _dil_kernel(q_ref, k_ref, v_ref, kp_ref, vp_ref, y_ref, o_scr, l_scr):
    has_prev_tile = pl.program_id(0) > 0
    b = DIL_BLOCK
    tile = q_ref.shape[0]
    jr = lax.broadcasted_iota(jnp.int32, (b, 2 * b), 0)
    mc = lax.broadcasted_iota(jnp.int32, (b, 2 * b), 1)
    band = (mc >= jr) & (mc <= jr + b)
    band_first = band & ((mc >= b) | has_prev_tile)

    def rows(ref, start, d):
        return ref[pl.ds(start, b), :] if d == 1 else ref[pl.ds(start, b, stride=d), :]

    for g, (_, d) in enumerate(DIL_PAIRS):
        for r in range(d):
            for n in range(tile // (b * d)):
                cur = r + n * b * d
                if n > 0:
                    k_prev, v_prev, keep = rows(k_ref, cur - b * d, d), rows(v_ref, cur - b * d, d), band
                else:
                    prev = tile - b * d + r
                    k_prev, v_prev, keep = rows(kp_ref, prev, d), rows(vp_ref, prev, d), band_first
                k2 = jnp.concatenate([k_prev, rows(k_ref, cur, d)], axis=0).astype(BF16)
                v2 = jnp.concatenate([v_prev, rows(v_ref, cur, d)], axis=0).astype(BF16)
                s = jnp.where(keep, _dot_nt(rows(q_ref, cur, d).astype(BF16), k2), NEG)
                m = jnp.max(s, axis=1, keepdims=True)
                p = jnp.exp(s - m)
                l = jnp.sum(p, axis=1, keepdims=True)
                o = _dot(p.astype(BF16), v2) / l
                lse = jnp.broadcast_to(m + jnp.log(l), (b, HEAD_DIM))
                if d == 1:
                    o_scr[g, pl.ds(cur, b), :] = o
                    l_scr[g, pl.ds(cur, b), :] = lse
                else:
                    o_scr[g, pl.ds(cur, b, stride=d), :] = o
                    l_scr[g, pl.ds(cur, b, stride=d), :] = lse

    l1, l2, l3 = l_scr[0], l_scr[1], l_scr[2]
    mx = jnp.maximum(jnp.maximum(l1, l2), l3)
    e1, e2, e3 = jnp.exp(l1 - mx), jnp.exp(l2 - mx), jnp.exp(l3 - mx)
    den = e1 + e2 + e3
    y_ref[...] = ((e1 / den) * o_scr[0] + (e2 / den) * o_scr[1] + (e3 / den) * o_scr[2]).astype(y_ref.dtype)


def _dilated_attention(qkv):
    s = qkv.shape[0]
    cur = lambda c: pl.BlockSpec((DIL_TILE, HEAD_DIM), lambda t, h: (t, c * N_DIL_HEADS + h))
    prev = lambda c: pl.BlockSpec((DIL_TILE, HEAD_DIM), lambda t, h: (jnp.maximum(t - 1, 0), c * N_DIL_HEADS + h))
    return pl.pallas_call(
        _dil_kernel,
        grid=(s // DIL_TILE, N_DIL_HEADS),
        in_specs=[cur(0), cur(1), cur(2), prev(1), prev(2)],
        out_specs=pl.BlockSpec((DIL_TILE, HEAD_DIM), lambda t, h: (t, h)),
        out_shape=jax.ShapeDtypeStruct((s, DIL_W), BF16),
        scratch_shapes=[pltpu.VMEM((len(DIL_PAIRS), DIL_TILE, HEAD_DIM), F32),
                        pltpu.VMEM((len(DIL_PAIRS), DIL_TILE, HEAD_DIM), F32)],
        compiler_params=_params(("parallel", "parallel")),
        name="dilated_attention",
    )(qkv, qkv, qkv, qkv, qkv)


def _attn_out_kernel(x_ref, ya_ref, yb_ref, w_ref, g_ref, b_ref, o_ref, ob_ref):
    y = _dot(jnp.concatenate([ya_ref[...], yb_ref[...]], axis=1), w_ref[...])
    out = _layer_norm(ALPHA * x_ref[...] + y, g_ref[...], b_ref[...])
    o_ref[...] = out
    ob_ref[...] = out.astype(BF16)


def _attn_out(x, ya, yb, w, g, b):
    s = x.shape[0]
    row = lambda width: pl.BlockSpec((OUT_ROWS, width), lambda i: (i, 0))
    const = lambda shape: pl.BlockSpec(shape, lambda i: (0, 0))
    return pl.pallas_call(
        _attn_out_kernel,
        grid=(s // OUT_ROWS,),
        in_specs=[row(D_MODEL), row(FOX_W), row(DIL_W),
                  const((FOX_W + DIL_W, D_MODEL)), const((1, D_MODEL)), const((1, D_MODEL))],
        out_specs=[row(D_MODEL), row(D_MODEL)],
        out_shape=[jax.ShapeDtypeStruct((s, D_MODEL), F32), jax.ShapeDtypeStruct((s, D_MODEL), BF16)],
        compiler_params=_params(("parallel",)),
        name="attn_out",
    )(x, ya, yb, w, g, b)


def _cmul(ar, ai, br, bi):
    return ar * br - ai * bi, ar * bi + ai * br


def _s5_prep_kernel(lrg_ref, lig_ref, ldtg_ref, lr_ref, li_ref, ldt_ref, br_ref, bi_ref,
                    pr_ref, pi_ref, qr_ref, qi_ref, bbr_ref, bbi_ref):
    def a_bar(lr, li, ldt):
        dt = jnp.exp(ldt)
        mag = jnp.exp(lr * dt)
        return mag * jnp.cos(li * dt), mag * jnp.sin(li * dt)

    lr, li = lr_ref[...], li_ref[...]
    ar, ai = a_bar(lr, li, ldt_ref[...])
    den = lr * lr + li * li
    cr = ((ar - 1.0) * lr + ai * li) / den
    ci = (ai * lr - (ar - 1.0) * li) / den
    bbr_ref[...], bbi_ref[...] = _cmul(cr, ci, br_ref[...], bi_ref[...])

    ar, ai = a_bar(lrg_ref[...], lig_ref[...], ldtg_ref[...])
    pr, pi = ar, ai
    pr_ref[0], pi_ref[0] = pr, pi
    for k in range(1, S5_SEG):
        pr, pi = _cmul(pr, pi, ar, ai)
        pr_ref[k], pi_ref[k] = pr, pi
    sr, si = pr, pi
    qr_ref[0], qi_ref[0] = sr, si
    for k in range(1, V7X_SUBLANES):
        pr, pi = _cmul(pr, pi, sr, si)
        qr_ref[k], qi_ref[k] = pr, pi


def _s5_prep(lam_re, lam_im, log_dt, b_re, b_im):
    ldt = jnp.broadcast_to(log_dt[:, None], lam_re.shape)
    rep = lambda v: jnp.repeat(v, S5_GROUP, axis=0)
    rows = S5_GROUPS * S5_GROUP
    bt = lambda v: v.transpose(0, 2, 1).reshape(rows, S5_STATE)
    mat = jax.ShapeDtypeStruct((rows, S5_STATE), F32)
    pw = jax.ShapeDtypeStruct((S5_SEG, S5_GROUPS, S5_STATE), F32)
    qw = jax.ShapeDtypeStruct((V7X_SUBLANES, S5_GROUPS, S5_STATE), F32)
    return pl.pallas_call(_s5_prep_kernel, out_shape=[pw, pw, qw, qw, mat, mat], name="s5_prep")(
        lam_re, lam_im, ldt, rep(lam_re), rep(lam_im), rep(ldt), bt(b_re), bt(b_im))


def _s5_core_kernel(u_ref, bblk_ref, cblk_ref, pr_ref, pi_ref, qr_ref, qi_ref, d_ref, z_ref,
                    up_ref, sr_ref, si_ref, cr_ref, ci_ref):
    @pl.when(pl.program_id(0) == 0)
    def _():
        cr_ref[...] = jnp.zeros_like(cr_ref)
        ci_ref[...] = jnp.zeros_like(ci_ref)

    sl = V7X_SUBLANES
    rows = u_ref.shape[0]
    ri = lax.broadcasted_iota(jnp.int32, (rows, rows), 0)
    ci = lax.broadcasted_iota(jnp.int32, (rows, rows), 1)
    perm = ((ri // sl == ci % S5_SEG) & (ri % sl == ci // S5_SEG)).astype(BF16)
    unperm = ((ci // sl == ri % S5_SEG) & (ci % sl == ri // S5_SEG)).astype(BF16)
    hi, mid, lo = _split3(u_ref[...])
    up_ref[...] = _dot(perm, hi) + _dot(perm, mid) + _dot(perm, lo)

    sub = lax.broadcasted_iota(jnp.int32, (sl, S5_LANES), 0)
    width = S5_GB * S5_GROUP
    ys = []
    for jb in range(S5_NB):
        bu = _dot(up_ref[:, jb * width:(jb + 1) * width].astype(BF16), bblk_ref[jb])
        a_re, a_im = pr_ref[jb, 0:1, :], pi_ref[jb, 0:1, :]
        xr, xi = bu[0:sl, :S5_LANES], bu[0:sl, S5_LANES:]
        sr_ref[0:sl, :], si_ref[0:sl, :] = xr, xi
        for j in range(1, S5_SEG):
            tr, ti = _cmul(a_re, a_im, xr, xi)
            xr, xi = tr + bu[j * sl:(j + 1) * sl, :S5_LANES], ti + bu[j * sl:(j + 1) * sl, S5_LANES:]
            sr_ref[j * sl:(j + 1) * sl, :], si_ref[j * sl:(j + 1) * sl, :] = xr, xi

        q_re, q_im = qr_ref[jb], qi_ref[jb]
        er, ei = xr, xi
        for k in (1, 2, 4):
            tr = jnp.where(sub >= k, pltpu.roll(er, k, 0), 0.0)
            ti = jnp.where(sub >= k, pltpu.roll(ei, k, 0), 0.0)
            mr, mi = _cmul(q_re[k - 1:k, :], q_im[k - 1:k, :], tr, ti)
            er, ei = er + mr, ei + mi
        in_re, in_im = cr_ref[jb:jb + 1, :], ci_ref[jb:jb + 1, :]
        mr, mi = _cmul(q_re, q_im, in_re, in_im)
        er, ei = er + mr, ei + mi
        cr_ref[jb:jb + 1, :], ci_ref[jb:jb + 1, :] = er[sl - 1:, :], ei[sl - 1:, :]
        cin_re = jnp.where(sub >= 1, pltpu.roll(er, 1, 0), in_re)
        cin_im = jnp.where(sub >= 1, pltpu.roll(ei, 1, 0), in_im)
        fr, fi = _cmul(pr_ref[jb][:, None, :], pi_ref[jb][:, None, :], cin_re[None], cin_im[None])
        shape3 = (S5_SEG, sl, S5_LANES)
        state = jnp.concatenate([(sr_ref[...].reshape(shape3) + fr).reshape(-1, S5_LANES),
                                 (si_ref[...].reshape(shape3) + fi).reshape(-1, S5_LANES)], axis=1)
        ys.append(_dot(state.astype(BF16), cblk_ref[jb]))
    y = jnp.concatenate(ys, axis=1) + d_ref[...] * up_ref[...]
    z = 0.5 * y * (1.0 + jnp.tanh(math.sqrt(2.0 / math.pi) * (y + 0.044715 * (y * y * y))))
    z_ref[...] = _dot(unperm, z.astype(BF16)).astype(z_ref.dtype)


def _s5_core(u, bblk, cblk, p_re, p_im, q_re, q_im, d_row):
    s = u.shape[0]
    const = lambda shape: pl.BlockSpec(shape, lambda t: (0,) * len(shape))
    return pl.pallas_call(
        _s5_core_kernel,
        grid=(s // S5_ROWS,),
        in_specs=[pl.BlockSpec((S5_ROWS, S5_WIDTH), lambda t: (t, 0)),
                  const(bblk.shape), const(cblk.shape), const(p_re.shape), const(p_im.shape),
                  const(q_re.shape), const(q_im.shape), const((1, S5_WIDTH))],
        out_specs=pl.BlockSpec((S5_ROWS, S5_WIDTH), lambda t: (t, 0)),
        out_shape=jax.ShapeDtypeStruct((s, S5_WIDTH), BF16),
        scratch_shapes=[pltpu.VMEM((S5_ROWS, S5_WIDTH), F32),
                        pltpu.VMEM((S5_ROWS, S5_LANES), F32), pltpu.VMEM((S5_ROWS, S5_LANES), F32),
                        pltpu.VMEM((V7X_SUBLANES, S5_LANES), F32), pltpu.VMEM((V7X_SUBLANES, S5_LANES), F32)],
        compiler_params=_params(("arbitrary",)),
        name="s5_core",
    )(u, bblk, cblk, p_re, p_im, q_re, q_im, d_row)


def _block_diag(blocks):
    nb, gb, r, c = blocks.shape
    eye = jnp.eye(gb, dtype=blocks.dtype)
    return jnp.einsum("ngrc,gk->ngrkc", blocks, eye).reshape(nb, gb * r, gb * c)


def _glu_kernel(x_ref, z_ref, wo_ref, wg_ref, g_ref, b_ref, o_ref, ob_ref):
    z = z_ref[...]
    y = _dot(z, wo_ref[...]) * jax.nn.sigmoid(_dot(z, wg_ref[...]))
    out = _layer_norm(ALPHA * x_ref[...] + y, g_ref[...], b_ref[...])
    o_ref[...] = out
    ob_ref[...] = out.astype(BF16)


def _glu(x, z, wo, wg, g, b):
    s = x.shape[0]
    row = lambda width: pl.BlockSpec((GLU_ROWS, width), lambda i: (i, 0))
    const = lambda shape: pl.BlockSpec(shape, lambda i: (0, 0))
    return pl.pallas_call(
        _glu_kernel,
        grid=(s // GLU_ROWS,),
        in_specs=[row(D_MODEL), row(S5_WIDTH), const((S5_WIDTH, D_MODEL)), const((S5_WIDTH, D_MODEL)),
                  const((1, D_MODEL)), const((1, D_MODEL))],
        out_specs=[row(D_MODEL), row(D_MODEL)],
        out_shape=[jax.ShapeDtypeStruct((s, D_MODEL), F32), jax.ShapeDtypeStruct((s, D_MODEL), BF16)],
        compiler_params=_params(("parallel",)),
        name="glu",
    )(x, z, wo, wg, g, b)


def _attention_mixer(x, xb, w_in, b_f, w_out, g, b):
    scale = HEAD_DIM ** -0.5
    fox_q, fox_k, fox_v = (w_in[:, n * FOX_W:(n + 1) * FOX_W] for n in range(3))
    w_f = w_in[:, 3 * FOX_W:3 * FOX_W + N_FOX_HEADS]
    d0 = 3 * FOX_W + N_FOX_HEADS
    dil_q, dil_k, dil_v = (w_in[:, d0 + n * DIL_W:d0 + (n + 1) * DIL_W] for n in range(3))
    w_fox = jnp.concatenate([fox_q * (scale * LOG2E), fox_k, fox_v], axis=1).astype(BF16)
    w_dil = jnp.concatenate([dil_q * scale, dil_k, dil_v], axis=1).astype(BF16)
    w_f = jnp.pad(w_f, ((0, 0), (0, V7X_LANES - N_FOX_HEADS))).astype(BF16)
    qkv_fox, f_logit = _attn_in(xb, w_fox, w_f)
    c_rows = _forget_cumsum(f_logit, jnp.pad(b_f, (0, V7X_LANES - N_FOX_HEADS))[None, :])
    ya = _fox_attention(*_fox_prep(qkv_fox, c_rows))
    yb = _dilated_attention(_matmul(xb, w_dil, F32))
    return _attn_out(x, ya, yb, w_out.astype(BF16), g, b)


def _s5_mixer(x, xb, w_in, lam_re, lam_im, log_dt, b_re, b_im, c_re, c_im, d_skip, w_glu_out, w_glu_gate, g, b):
    u = _matmul(xb, w_in.astype(BF16), F32)
    p_re, p_im, q_re, q_im, bb_re, bb_im = _s5_prep(lam_re, lam_im, log_dt, b_re, b_im)
    lanes = lambda p: p.reshape(p.shape[0], S5_NB, S5_LANES).transpose(1, 0, 2)
    blk = lambda v: v.reshape(S5_NB, S5_GB, S5_GROUP, S5_STATE)
    bblk = jnp.concatenate([_block_diag(blk(bb_re)), _block_diag(blk(bb_im))], axis=2).astype(BF16)
    ct = lambda v: v.transpose(0, 2, 1).reshape(S5_NB, S5_GB, S5_STATE, S5_GROUP)
    cblk = jnp.concatenate([_block_diag(ct(c_re)), -_block_diag(ct(c_im))], axis=1).astype(BF16)
    z = _s5_core(u, bblk, cblk, lanes(p_re), lanes(p_im), lanes(q_re), lanes(q_im), d_skip[None, :])
    return _glu(x, z, w_glu_out.astype(BF16), w_glu_gate.astype(BF16), g, b)


def kernel(x, ffn1_w_gate, ffn1_w_up, ffn1_w_down, ffn2_w_gate, ffn2_w_up, ffn2_w_down, ln_gain, ln_bias,
           attn_w_in, attn_b_f, attn_w_out, s5_w_in, s5_lambda_re, s5_lambda_im, s5_log_dt, s5_b_re, s5_b_im,
           s5_c_re, s5_c_im, s5_d, s5_w_glu_out, s5_w_glu_gate):
    batch, seq, _ = x.shape
    assert batch == 1
    h = x.reshape(seq, D_MODEL)
    hb = h.astype(BF16)
    ln = lambda i, k: (ln_gain[i, k][None, :], ln_bias[i, k][None, :])
    ffn_w = ((ffn1_w_gate, ffn1_w_up, ffn1_w_down), (ffn2_w_gate, ffn2_w_up, ffn2_w_down))
    order = [(which, i) for i in range(DEPTH) for which in (0, 1)]
    weights = _ffn_weights_bf16(*ffn_w[0], 0)

    def ffn(h, hb, n, ln_params):
        nonlocal weights
        nxt = order[n + 1] if n + 1 < len(order) else None
        h, hb, weights = _ffn(h, hb, weights, *ln_params,
                              next_weights=ffn_w[nxt[0]] if nxt else None, next_layer=nxt[1] if nxt else None)
        return h, hb

    for i in range(DEPTH):
        h, hb = ffn(h, hb, 2 * i, ln(i, 0))
        j = i // 2
        if i % 2 == 0:
            h, hb = _attention_mixer(h, hb, attn_w_in[j], attn_b_f[j], attn_w_out[j], *ln(i, 1))
        else:
            h, hb = _s5_mixer(h, hb, s5_w_in[j], s5_lambda_re[j], s5_lambda_im[j], s5_log_dt[j], s5_b_re[j],
                              s5_b_im[j], s5_c_re[j], s5_c_im[j], s5_d[j], s5_w_glu_out[j], s5_w_glu_gate[j],
                              *ln(i, 1))
        h, hb = ffn(h, hb, 2 * i + 1, ln(i, 2))
    return h.reshape(batch, seq, D_MODEL)
```

```python
import math

import jax
import jax.numpy as jnp
from jax import lax
from jax.experimental import pallas as pl
from jax.experimental.pallas import tpu as pltpu

D_MODEL = 2048
DEPTH = 2
HEAD_DIM = 128
N_FOX_HEADS = 8
N_DIL_HEADS = 8
DIL_PAIRS = ((128, 1), (512, 4), (2048, 16))
DIL_BLOCK = 128
D_FF = 5632
S5_GROUP = 16
S5_WIDTH = 1024
S5_GROUPS = S5_WIDTH // S5_GROUP
S5_STATE = 64
ALPHA = (2.0 * DEPTH) ** 0.25
LN_EPS = 1e-5
FOX_W = N_FOX_HEADS * HEAD_DIM
DIL_W = N_DIL_HEADS * HEAD_DIM

F32 = jnp.float32
BF16 = jnp.bfloat16
NEG = -1e30
LOG2E = math.log2(math.e)
FOX_AUG = 128
FOX_VROWS = HEAD_DIM + 16

V7X_LANES = 128
V7X_SUBLANES = 8
V7X_VMEM_LIMIT_BYTES = 56 * 1024 * 1024

FFN_ROWS = 512
FFN_COLS = 512
FFN_CHUNKS = D_FF // FFN_COLS
FFN_GROUP = 256
FFN_RING_SLOTS = 3
assert (FFN_CHUNKS + 1) % FFN_RING_SLOTS == 0
MM_ROWS = 1024
MM_COLS = 1024
FOX_BLOCK = 1024
DIL_TILE = 2048
OUT_ROWS = 512
GLU_ROWS = 512
CUMSUM_ROWS = 512
CAST_STEPS = 8
S5_ROWS = 256
S5_SEG = S5_ROWS // V7X_SUBLANES
S5_GB = 16
S5_NB = S5_GROUPS // S5_GB
S5_LANES = S5_GB * S5_STATE


def _params(sem):
    return pltpu.CompilerParams(dimension_semantics=sem, vmem_limit_bytes=V7X_VMEM_LIMIT_BYTES)


def _layer_norm(v, g, b):
    mu = jnp.mean(v, axis=-1, keepdims=True)
    d = v - mu
    var = jnp.mean(d * d, axis=-1, keepdims=True)
    return d * lax.rsqrt(var + LN_EPS) * g + b


def _dot(a, b):
    return jnp.dot(a, b, preferred_element_type=F32)


def _dot_nt(a, b):
    return lax.dot_general(a, b, (((1,), (1,)), ((), ())), preferred_element_type=F32)


def _ffn_kernel(x_ref, xb_ref, wgu_hbm, wd_hbm, g_ref, b_ref, o_ref, ob_ref,
                h_ref, wgu_buf, wd_buf, sem):
    i = pl.program_id(0)
    last_tile = pl.num_programs(0) - 1
    chunks = FFN_CHUNKS
    nsub = chunks + 1

    def ring(sub, slot, op):
        @pl.when(sub < chunks)
        def _():
            c = jnp.minimum(sub, chunks - 1)
            op(pltpu.make_async_copy(wgu_hbm.at[c], wgu_buf.at[slot], sem.at[0, slot]))

        @pl.when(sub >= 1)
        def _():
            c = jnp.maximum(sub - 1, 0)
            op(pltpu.make_async_copy(wd_hbm.at[c], wd_buf.at[slot], sem.at[1, slot]))

    start = lambda cp: cp.start()
    wait = lambda cp: cp.wait()

    @pl.when(i == 0)
    def _():
        ring(jnp.int32(0), 0, start)
        ring(jnp.int32(1), 1, start)

    def fetch_ahead_and_wait(j):
        ahead = j + 2
        wraps = ahead >= nsub

        @pl.when(jnp.logical_or(jnp.logical_not(wraps), i < last_tile))
        def _():
            ring(jnp.where(wraps, ahead - nsub, ahead), ahead % FFN_RING_SLOTS, start)

        ring(j, j % FFN_RING_SLOTS, wait)

    def hidden(slot):
        gu = _dot(xb_ref[...], wgu_buf[slot])
        parts = []
        for n in range(FFN_COLS // FFN_GROUP):
            gate = gu[:, 2 * n * FFN_GROUP:(2 * n + 1) * FFN_GROUP]
            up = gu[:, (2 * n + 1) * FFN_GROUP:(2 * n + 2) * FFN_GROUP]
            parts.append((gate * jax.nn.sigmoid(gate) * up).astype(BF16))
        return jnp.concatenate(parts, axis=1)

    def down(j, slot):
        return _dot(h_ref[(j - 1) % 2], wd_buf[slot])

    fetch_ahead_and_wait(jnp.int32(0))
    h_ref[0] = hidden(0)

    fetch_ahead_and_wait(jnp.int32(1))
    o_ref[...] = down(1, 1)
    h_ref[1] = hidden(1)

    def body(j, carry):
        fetch_ahead_and_wait(j)
        slot = j % FFN_RING_SLOTS
        o_ref[...] += down(j, slot)
        h_ref[j % 2] = hidden(slot)
        return carry

    lax.fori_loop(2, chunks, body, 0)

    fetch_ahead_and_wait(jnp.int32(chunks))
    acc = o_ref[...] + down(chunks, chunks % FFN_RING_SLOTS)
    y = _layer_norm(ALPHA * x_ref[...] + 0.5 * acc, g_ref[...], b_ref[...])
    o_ref[...] = y
    ob_ref[...] = y.astype(BF16)


def _ffn(x, xb, wgu, wd, g, b):
    s = x.shape[0]
    row = lambda i: (i, 0)
    hbm = pl.BlockSpec(memory_space=pl.ANY)
    return pl.pallas_call(
        _ffn_kernel,
        grid=(s // FFN_ROWS,),
        in_specs=[
            pl.BlockSpec((FFN_ROWS, D_MODEL), row),
            pl.BlockSpec((FFN_ROWS, D_MODEL), row),
            hbm, hbm,
            pl.BlockSpec((1, D_MODEL), lambda i: (0, 0)),
            pl.BlockSpec((1, D_MODEL), lambda i: (0, 0)),
        ],
        out_specs=[pl.BlockSpec((FFN_ROWS, D_MODEL), row), pl.BlockSpec((FFN_ROWS, D_MODEL), row)],
        out_shape=[jax.ShapeDtypeStruct((s, D_MODEL), F32), jax.ShapeDtypeStruct((s, D_MODEL), BF16)],
        scratch_shapes=[
            pltpu.VMEM((2, FFN_ROWS, FFN_COLS), BF16),
            pltpu.VMEM((FFN_RING_SLOTS, D_MODEL, 2 * FFN_COLS), BF16),
            pltpu.VMEM((FFN_RING_SLOTS, FFN_COLS, D_MODEL), BF16),
            pltpu.SemaphoreType.DMA((2, FFN_RING_SLOTS)),
        ],
        compiler_params=_params(("arbitrary",)),
        name="ffn",
    )(x, xb, wgu, wd, g, b)


def _cast_kernel(wg_ref, wu_ref, wd_ref, ogu_ref, od_ref):
    for n in range(D_FF // FFN_GROUP):
        c, k = divmod(n, FFN_COLS // FFN_GROUP)
        src = slice(n * FFN_GROUP, (n + 1) * FFN_GROUP)
        ogu_ref[c, :, 2 * k * FFN_GROUP:(2 * k + 1) * FFN_GROUP] = wg_ref[:, src].astype(BF16)
        ogu_ref[c, :, (2 * k + 1) * FFN_GROUP:(2 * k + 2) * FFN_GROUP] = wu_ref[:, src].astype(BF16)
    od_ref[...] = wd_ref[...].astype(BF16)


def _ffn_weights_bf16(w_gate, w_up, w_down, layer):
    steps = CAST_STEPS
    up_rows, down_rows = D_MODEL // steps, D_FF // steps
    up_in = pl.BlockSpec((None, up_rows, D_FF), lambda r: (layer, r, 0))
    up_out = pl.BlockSpec((FFN_CHUNKS, up_rows, 2 * FFN_COLS), lambda r: (0, r, 0))
    up_shape = jax.ShapeDtypeStruct((FFN_CHUNKS, D_MODEL, 2 * FFN_COLS), BF16)
    wgu, wd = pl.pallas_call(
        _cast_kernel,
        grid=(steps,),
        in_specs=[up_in, up_in, pl.BlockSpec((None, down_rows, D_MODEL), lambda r: (layer, r, 0))],
        out_specs=[up_out, pl.BlockSpec((down_rows, D_MODEL), lambda r: (r, 0))],
        out_shape=[up_shape, jax.ShapeDtypeStruct((D_FF, D_MODEL), BF16)],
        compiler_params=_params(("parallel",)),
        name="ffn_weights_bf16",
    )(w_gate, w_up, w_down)
    return wgu, wd.reshape(FFN_CHUNKS, FFN_COLS, D_MODEL)


def _attn_in_kernel(x_ref, w_ref, wf_ref, o_ref, f_ref):
    x = x_ref[...]
    o_ref[...] = _dot(x, w_ref[...]).astype(o_ref.dtype)

    @pl.when(pl.program_id(1) == 0)
    def _():
        f_ref[...] = _dot(x, wf_ref[...])


def _attn_in(xb, w, wf):
    s, n = xb.shape[0], w.shape[1]
    return pl.pallas_call(
        _attn_in_kernel,
        grid=(s // MM_ROWS, n // MM_COLS),
        in_specs=[
            pl.BlockSpec((MM_ROWS, D_MODEL), lambda i, j: (i, 0)),
            pl.BlockSpec((D_MODEL, MM_COLS), lambda i, j: (0, j)),
            pl.BlockSpec((D_MODEL, V7X_LANES), lambda i, j: (0, 0)),
        ],
        out_specs=[
            pl.BlockSpec((MM_ROWS, MM_COLS), lambda i, j: (i, j)),
            pl.BlockSpec((MM_ROWS, V7X_LANES), lambda i, j: (i, 0)),
        ],
        out_shape=[jax.ShapeDtypeStruct((s, n), BF16), jax.ShapeDtypeStruct((s, V7X_LANES), F32)],
        compiler_params=_params(("parallel", "arbitrary")),
        name="attn_in",
    )(xb, w, wf)


def _mm_kernel(x_ref, w_ref, o_ref):
    o_ref[...] = _dot(x_ref[...], w_ref[...]).astype(o_ref.dtype)


def _matmul(xb, w, out_dtype):
    s, n = xb.shape[0], w.shape[1]
    return pl.pallas_call(
        _mm_kernel,
        grid=(s // MM_ROWS, n // MM_COLS),
        in_specs=[
            pl.BlockSpec((MM_ROWS, xb.shape[1]), lambda i, j: (i, 0)),
            pl.BlockSpec((xb.shape[1], MM_COLS), lambda i, j: (0, j)),
        ],
        out_specs=pl.BlockSpec((MM_ROWS, MM_COLS), lambda i, j: (i, j)),
        out_shape=jax.ShapeDtypeStruct((s, n), out_dtype),
        compiler_params=_params(("parallel", "parallel")),
        name="matmul",
    )(xb, w)


def _split3(v):
    hi = v.astype(BF16)
    r1 = v - hi.astype(F32)
    mid = r1.astype(BF16)
    lo = (r1 - mid.astype(F32)).astype(BF16)
    return hi, mid, lo


def _cumsum_kernel(f_ref, bf_ref, c_ref, carry_ref):
    @pl.when(pl.program_id(0) == 0)
    def _():
        carry_ref[...] = jnp.zeros_like(carry_ref)

    z = f_ref[...] + bf_ref[...]
    log_f = jnp.minimum(z, 0.0) - jnp.log1p(jnp.exp(-jnp.abs(z)))
    n = z.shape[0]
    tri = (lax.broadcasted_iota(jnp.int32, (n, n), 0) >= lax.broadcasted_iota(jnp.int32, (n, n), 1)).astype(BF16)
    hi, mid, lo = _split3(log_f)
    c = _dot(tri, hi) + _dot(tri, mid) + _dot(tri, lo) + carry_ref[0:1, :]
    c_ref[...] = c
    carry_ref[...] = jnp.broadcast_to(c[n - 1:n, :], carry_ref.shape)


def _forget_cumsum(f_logit, b_f_row):
    s = f_logit.shape[0]
    return pl.pallas_call(
        _cumsum_kernel,
        grid=(s // CUMSUM_ROWS,),
        in_specs=[
            pl.BlockSpec((CUMSUM_ROWS, V7X_LANES), lambda i: (i, 0)),
            pl.BlockSpec((1, V7X_LANES), lambda i: (0, 0)),
        ],
        out_specs=pl.BlockSpec((CUMSUM_ROWS, V7X_LANES), lambda i: (i, 0)),
        out_shape=jax.ShapeDtypeStruct((s, V7X_LANES), F32),
        scratch_shapes=[pltpu.VMEM((V7X_SUBLANES, V7X_LANES), F32)],
        compiler_params=_params(("arbitrary",)),
        name="forget_cumsum",
    )(f_logit, b_f_row)


def _fox_prep_kernel(qkv_ref, c_ref, qt_ref, ka_ref, vt_ref):
    rows = qkv_ref.shape[0]
    lane = lax.broadcasted_iota(jnp.int32, (rows, V7X_LANES), 1)
    bias = c_ref[...] * (-LOG2E)
    ones_rows = (lax.broadcasted_iota(jnp.int32, (FOX_AUG, rows), 0) < 3).astype(BF16)
    sum_rows = (lax.broadcasted_iota(jnp.int32, (FOX_VROWS - HEAD_DIM, rows), 0) < 1).astype(BF16)
    for h in range(N_FOX_HEADS):
        hs = slice(h * HEAD_DIM, (h + 1) * HEAD_DIM)
        a0 = h * (HEAD_DIM + FOX_AUG)
        qt_ref[a0:a0 + HEAD_DIM, :] = qkv_ref[:, hs].astype(F32).T.astype(BF16)
        qt_ref[a0 + HEAD_DIM:a0 + HEAD_DIM + FOX_AUG, :] = ones_rows
        v0 = h * FOX_VROWS
        vt_ref[v0:v0 + HEAD_DIM, :] = (
            qkv_ref[:, 2 * FOX_W + h * HEAD_DIM:2 * FOX_W + (h + 1) * HEAD_DIM].astype(F32).T.astype(BF16))
        vt_ref[v0 + HEAD_DIM:v0 + FOX_VROWS, :] = sum_rows
        hi, mid, lo = _split3(jnp.broadcast_to(bias[:, h:h + 1], (rows, V7X_LANES)))
        aug = jnp.where(lane == 0, hi.astype(F32), jnp.where(lane == 1, mid.astype(F32),
                                                              jnp.where(lane == 2, lo.astype(F32), 0.0)))
        ka_ref[:, a0:a0 + HEAD_DIM] = qkv_ref[:, FOX_W + h * HEAD_DIM:FOX_W + (h + 1) * HEAD_DIM]
        ka_ref[:, a0 + HEAD_DIM:a0 + HEAD_DIM + FOX_AUG] = aug.astype(BF16)


def _fox_prep(qkv, c_rows):
    s = qkv.shape[0]
    wide = N_FOX_HEADS * (HEAD_DIM + FOX_AUG)
    vrows = N_FOX_HEADS * FOX_VROWS
    return pl.pallas_call(
        _fox_prep_kernel,
        grid=(s // FOX_BLOCK,),
        in_specs=[pl.BlockSpec((FOX_BLOCK, 3 * FOX_W), lambda i: (i, 0)),
                  pl.BlockSpec((FOX_BLOCK, V7X_LANES), lambda i: (i, 0))],
        out_specs=[pl.BlockSpec((wide, FOX_BLOCK), lambda i: (0, i)),
                   pl.BlockSpec((FOX_BLOCK, wide), lambda i: (i, 0)),
                   pl.BlockSpec((vrows, FOX_BLOCK), lambda i: (0, i))],
        out_shape=[jax.ShapeDtypeStruct((wide, s), BF16), jax.ShapeDtypeStruct((s, wide), BF16),
                   jax.ShapeDtypeStruct((vrows, s), BF16)],
        compiler_params=_params(("parallel",)),
        name="fox_prep",
    )(qkv, c_rows)


def _fox_kernel(qi_ref, kj_ref, qt_ref, ka_ref, vt_ref, o_ref, m_ref, acc_ref, s_ref):
    t = pl.program_id(0)
    i = qi_ref[t]
    j = kj_ref[t]
    blk = ka_ref.shape[0]
    aw = HEAD_DIM + FOX_AUG

    @pl.when(j == 0)
    def _():
        m_ref[...] = jnp.full_like(m_ref, NEG)
        acc_ref[...] = jnp.zeros_like(acc_ref)

    def scores(h):
        return _dot(ka_ref[:, h * aw:(h + 1) * aw], qt_ref[h * aw:(h + 1) * aw, :])

    def step(diagonal):
        if diagonal:
            keep = lax.broadcasted_iota(jnp.int32, (blk, blk), 0) <= lax.broadcasted_iota(jnp.int32, (blk, blk), 1)
        s_ref[0] = scores(0)
        for h in range(N_FOX_HEADS):
            if h + 1 < N_FOX_HEADS:
                s_ref[(h + 1) % 2] = scores(h + 1)
            s = s_ref[h % 2]
            if diagonal:
                s = jnp.where(keep, s, NEG)
            m_prev = m_ref[h:h + 1, :]
            m_new = jnp.maximum(m_prev, jnp.max(s, axis=0, keepdims=True))
            p = jnp.exp2(s - m_new).astype(BF16)
            alpha = jnp.exp2(m_prev - m_new)
            m_ref[h:h + 1, :] = m_new
            vs = slice(h * FOX_VROWS, (h + 1) * FOX_VROWS)
            acc_ref[vs, :] = alpha * acc_ref[vs, :] + _dot(vt_ref[vs, :], p)

    @pl.when(j < i)
    def _():
        step(False)

    @pl.when(j == i)
    def _():
        step(True)
        for h in range(N_FOX_HEADS):
            v0 = h * FOX_VROWS
            out_t = acc_ref[v0:v0 + HEAD_DIM, :] / acc_ref[v0 + HEAD_DIM:v0 + HEAD_DIM + 1, :]
            o_ref[:, h * HEAD_DIM:(h + 1) * HEAD_DIM] = out_t.T.astype(o_ref.dtype)


def _fox_attention(qt, ka, vt):
    s = ka.shape[0]
    nb = s // FOX_BLOCK
    qi = jnp.asarray([i for i in range(nb) for _ in range(i + 1)], jnp.int32)
    kj = jnp.asarray([j for i in range(nb) for j in range(i + 1)], jnp.int32)
    wide = ka.shape[1]
    vrows = vt.shape[0]
    grid_spec = pltpu.PrefetchScalarGridSpec(
        num_scalar_prefetch=2,
        grid=(qi.shape[0],),
        in_specs=[
            pl.BlockSpec((wide, FOX_BLOCK), lambda t, qi, kj: (0, qi[t])),
            pl.BlockSpec((FOX_BLOCK, wide), lambda t, qi, kj: (kj[t], 0)),
            pl.BlockSpec((vrows, FOX_BLOCK), lambda t, qi, kj: (0, kj[t])),
        ],
        out_specs=pl.BlockSpec((FOX_BLOCK, FOX_W), lambda t, qi, kj: (qi[t], 0)),
        scratch_shapes=[
            pltpu.VMEM((N_FOX_HEADS, FOX_BLOCK), F32),
            pltpu.VMEM((vrows, FOX_BLOCK), F32),
            pltpu.VMEM((2, FOX_BLOCK, FOX_BLOCK), F32),
        ],
    )
    return pl.pallas_call(
        _fox_kernel,
        grid_spec=grid_spec,
        out_shape=jax.ShapeDtypeStruct((s, FOX_W), BF16),
        compiler_params=_params(("arbitrary",)),
        name="fox_attention",
    )(qi, kj, qt, ka, vt)


def _dil_kernel(q_ref, k_ref, v_ref, kp_ref, vp_ref, y_ref, o_scr, l_scr):
    has_prev_tile = pl.program_id(0) > 0
    b = DIL_BLOCK
    tile = q_ref.shape[0]
    jr = lax.broadcasted_iota(jnp.int32, (b, 2 * b), 0)
    mc = lax.broadcasted_iota(jnp.int32, (b, 2 * b), 1)
    band = (mc >= jr) & (mc <= jr + b)
    band_first = band & ((mc >= b) | has_prev_tile)

    def rows(ref, start, d):
        return ref[pl.ds(start, b), :] if d == 1 else ref[pl.ds(start, b, stride=d), :]

    for g, (_, d) in enumerate(DIL_PAIRS):
        for r in range(d):
            first = tile - b * d + r
            k_blk = [rows(kp_ref, first, d).astype(BF16)]
            v_blk = [rows(vp_ref, first, d).astype(BF16)]
            for n in range(tile // (b * d)):
                cur = r + n * b * d
                k_blk.append(rows(k_ref, cur, d).astype(BF16))
                v_blk.append(rows(v_ref, cur, d).astype(BF16))
                k2 = jnp.concatenate(k_blk[n:n + 2], axis=0)
                v2 = jnp.concatenate(v_blk[n:n + 2], axis=0)
                keep = band if n > 0 else band_first
                s = jnp.where(keep, _dot_nt(rows(q_ref, cur, d).astype(BF16), k2), NEG)
                m = jnp.max(s, axis=1, keepdims=True)
                p = jnp.exp(s - m)
                l = jnp.sum(p, axis=1, keepdims=True)
                o = _dot(p.astype(BF16), v2) / l
                lse = jnp.broadcast_to(m + jnp.log(l), (b, HEAD_DIM))
                if d == 1:
                    o_scr[g, pl.ds(cur, b), :] = o
                    l_scr[g, pl.ds(cur, b), :] = lse
                else:
                    o_scr[g, pl.ds(cur, b, stride=d), :] = o
                    l_scr[g, pl.ds(cur, b, stride=d), :] = lse

    l1, l2, l3 = l_scr[0], l_scr[1], l_scr[2]
    mx = jnp.maximum(jnp.maximum(l1, l2), l3)
    e1, e2, e3 = jnp.exp(l1 - mx), jnp.exp(l2 - mx), jnp.exp(l3 - mx)
    den = e1 + e2 + e3
    y_ref[...] = ((e1 / den) * o_scr[0] + (e2 / den) * o_scr[1] + (e3 / den) * o_scr[2]).astype(y_ref.dtype)


def _dilated_attention(qkv):
    s = qkv.shape[0]
    cur = lambda c: pl.BlockSpec((DIL_TILE, HEAD_DIM), lambda t, h: (t, c * N_DIL_HEADS + h))
    prev = lambda c: pl.BlockSpec((DIL_TILE, HEAD_DIM), lambda t, h: (jnp.maximum(t - 1, 0), c * N_DIL_HEADS + h))
    return pl.pallas_call(
        _dil_kernel,
        grid=(s // DIL_TILE, N_DIL_HEADS),
        in_specs=[cur(0), cur(1), cur(2), prev(1), prev(2)],
        out_specs=pl.BlockSpec((DIL_TILE, HEAD_DIM), lambda t, h: (t, h)),
        out_shape=jax.ShapeDtypeStruct((s, DIL_W), BF16),
        scratch_shapes=[pltpu.VMEM((len(DIL_PAIRS), DIL_TILE, HEAD_DIM), F32),
                        pltpu.VMEM((len(DIL_PAIRS), DIL_TILE, HEAD_DIM), F32)],
        compiler_params=_params(("parallel", "parallel")),
        name="dilated_attention",
    )(qkv, qkv, qkv, qkv, qkv)


def _attn_out_kernel(x_ref, ya_ref, yb_ref, w_ref, g_ref, b_ref, o_ref, ob_ref):
    half = x_ref.shape[0] // 2
    for r in range(2):
        rows = slice(r * half, (r + 1) * half)
        y = _dot(jnp.concatenate([ya_ref[rows, :], yb_ref[rows, :]], axis=1), w_ref[...])
        out = _layer_norm(ALPHA * x_ref[rows, :] + y, g_ref[...], b_ref[...])
        o_ref[rows, :] = out
        ob_ref[rows, :] = out.astype(BF16)


def _attn_out(x, ya, yb, w, g, b):
    s = x.shape[0]
    row = lambda width: pl.BlockSpec((OUT_ROWS, width), lambda i: (i, 0))
    const = lambda shape: pl.BlockSpec(shape, lambda i: (0, 0))
    return pl.pallas_call(
        _attn_out_kernel,
        grid=(s // OUT_ROWS,),
        in_specs=[row(D_MODEL), row(FOX_W), row(DIL_W),
                  const((FOX_W + DIL_W, D_MODEL)), const((1, D_MODEL)), const((1, D_MODEL))],
        out_specs=[row(D_MODEL), row(D_MODEL)],
        out_shape=[jax.ShapeDtypeStruct((s, D_MODEL), F32), jax.ShapeDtypeStruct((s, D_MODEL), BF16)],
        compiler_params=_params(("parallel",)),
        name="attn_out",
    )(x, ya, yb, w, g, b)


def _cmul(ar, ai, br, bi):
    return ar * br - ai * bi, ar * bi + ai * br


def _s5_prep_kernel(lrg_ref, lig_ref, ldtg_ref, lr_ref, li_ref, ldt_ref, br_ref, bi_ref,
                    pr_ref, pi_ref, qr_ref, qi_ref, bbr_ref, bbi_ref):
    def a_bar(lr, li, ldt):
        dt = jnp.exp(ldt)
        mag = jnp.exp(lr * dt)
        return mag * jnp.cos(li * dt), mag * jnp.sin(li * dt)

    lr, li = lr_ref[...], li_ref[...]
    ar, ai = a_bar(lr, li, ldt_ref[...])
    den = lr * lr + li * li
    cr = ((ar - 1.0) * lr + ai * li) / den
    ci = (ai * lr - (ar - 1.0) * li) / den
    bbr_ref[...], bbi_ref[...] = _cmul(cr, ci, br_ref[...], bi_ref[...])

    ar, ai = a_bar(lrg_ref[...], lig_ref[...], ldtg_ref[...])
    pr, pi = ar, ai
    pr_ref[0], pi_ref[0] = pr, pi
    for k in range(1, S5_SEG):
        pr, pi = _cmul(pr, pi, ar, ai)
        pr_ref[k], pi_ref[k] = pr, pi
    sr, si = pr, pi
    qr_ref[0], qi_ref[0] = sr, si
    for k in range(1, V7X_SUBLANES):
        pr, pi = _cmul(pr, pi, sr, si)
        qr_ref[k], qi_ref[k] = pr, pi


def _s5_prep(lam_re, lam_im, log_dt, b_re, b_im):
    ldt = jnp.broadcast_to(log_dt[:, None], lam_re.shape)
    rep = lambda v: jnp.repeat(v, S5_GROUP, axis=0)
    rows = S5_GROUPS * S5_GROUP
    bt = lambda v: v.transpose(0, 2, 1).reshape(rows, S5_STATE)
    mat = jax.ShapeDtypeStruct((rows, S5_STATE), F32)
    pw = jax.ShapeDtypeStruct((S5_SEG, S5_GROUPS, S5_STATE), F32)
    qw = jax.ShapeDtypeStruct((V7X_SUBLANES, S5_GROUPS, S5_STATE), F32)
    return pl.pallas_call(_s5_prep_kernel, out_shape=[pw, pw, qw, qw, mat, mat], name="s5_prep")(
        lam_re, lam_im, ldt, rep(lam_re), rep(lam_im), rep(ldt), bt(b_re), bt(b_im))


def _s5_core_kernel(u_ref, bblk_ref, cblk_ref, pr_ref, pi_ref, qr_ref, qi_ref, d_ref, z_ref,
                    up_ref, sr_ref, si_ref, cr_ref, ci_ref):
    @pl.when(pl.program_id(0) == 0)
    def _():
        cr_ref[...] = jnp.zeros_like(cr_ref)
        ci_ref[...] = jnp.zeros_like(ci_ref)

    sl = V7X_SUBLANES
    rows = u_ref.shape[0]
    ri = lax.broadcasted_iota(jnp.int32, (rows, rows), 0)
    ci = lax.broadcasted_iota(jnp.int32, (rows, rows), 1)
    perm = ((ri // sl == ci % S5_SEG) & (ri % sl == ci // S5_SEG)).astype(BF16)
    unperm = ((ci // sl == ri % S5_SEG) & (ci % sl == ri // S5_SEG)).astype(BF16)
    hi, mid, lo = _split3(u_ref[...])
    up_ref[...] = _dot(perm, hi) + _dot(perm, mid) + _dot(perm, lo)

    sub = lax.broadcasted_iota(jnp.int32, (sl, S5_LANES), 0)
    width = S5_GB * S5_GROUP
    ys = []
    for jb in range(S5_NB):
        bu = _dot(up_ref[:, jb * width:(jb + 1) * width].astype(BF16), bblk_ref[jb])
        a_re, a_im = pr_ref[jb, 0:1, :], pi_ref[jb, 0:1, :]
        xr, xi = bu[0:sl, :S5_LANES], bu[0:sl, S5_LANES:]
        sr_ref[0:sl, :], si_ref[0:sl, :] = xr, xi
        for j in range(1, S5_SEG):
            tr, ti = _cmul(a_re, a_im, xr, xi)
            xr, xi = tr + bu[j * sl:(j + 1) * sl, :S5_LANES], ti + bu[j * sl:(j + 1) * sl, S5_LANES:]
            sr_ref[j * sl:(j + 1) * sl, :], si_ref[j * sl:(j + 1) * sl, :] = xr, xi

        q_re, q_im = qr_ref[jb], qi_ref[jb]
        er, ei = xr, xi
        for k in (1, 2, 4):
            tr = jnp.where(sub >= k, pltpu.roll(er, k, 0), 0.0)
            ti = jnp.where(sub >= k, pltpu.roll(ei, k, 0), 0.0)
            mr, mi = _cmul(q_re[k - 1:k, :], q_im[k - 1:k, :], tr, ti)
            er, ei = er + mr, ei + mi
        in_re, in_im = cr_ref[jb:jb + 1, :], ci_ref[jb:jb + 1, :]
        mr, mi = _cmul(q_re, q_im, in_re, in_im)
        er, ei = er + mr, ei + mi
        cr_ref[jb:jb + 1, :], ci_ref[jb:jb + 1, :] = er[sl - 1:, :], ei[sl - 1:, :]
        cin_re = jnp.where(sub >= 1, pltpu.roll(er, 1, 0), in_re)
        cin_im = jnp.where(sub >= 1, pltpu.roll(ei, 1, 0), in_im)
        fr, fi = _cmul(pr_ref[jb][:, None, :], pi_ref[jb][:, None, :], cin_re[None], cin_im[None])
        shape3 = (S5_SEG, sl, S5_LANES)
        state = jnp.concatenate([(sr_ref[...].reshape(shape3) + fr).reshape(-1, S5_LANES),
                                 (si_ref[...].reshape(shape3) + fi).reshape(-1, S5_LANES)], axis=1)
        ys.append(_dot(state.astype(BF16), cblk_ref[jb]))
    y = jnp.concatenate(ys, axis=1) + d_ref[...] * up_ref[...]
    z = 0.5 * y * (1.0 + jnp.tanh(math.sqrt(2.0 / math.pi) * (y + 0.044715 * (y * y * y))))
    z_ref[...] = _dot(unperm, z.astype(BF16)).astype(z_ref.dtype)


def _s5_core(u, bblk, cblk, p_re, p_im, q_re, q_im, d_row):
    s = u.shape[0]
    const = lambda shape: pl.BlockSpec(shape, lambda t: (0,) * len(shape))
    return pl.pallas_call(
        _s5_core_kernel,
        grid=(s // S5_ROWS,),
        in_specs=[pl.BlockSpec((S5_ROWS, S5_WIDTH), lambda t: (t, 0)),
                  const(bblk.shape), const(cblk.shape), const(p_re.shape), const(p_im.shape),
                  const(q_re.shape), const(q_im.shape), const((1, S5_WIDTH))],
        out_specs=pl.BlockSpec((S5_ROWS, S5_WIDTH), lambda t: (t, 0)),
        out_shape=jax.ShapeDtypeStruct((s, S5_WIDTH), BF16),
        scratch_shapes=[pltpu.VMEM((S5_ROWS, S5_WIDTH), F32),
                        pltpu.VMEM((S5_ROWS, S5_LANES), F32), pltpu.VMEM((S5_ROWS, S5_LANES), F32),
                        pltpu.VMEM((V7X_SUBLANES, S5_LANES), F32), pltpu.VMEM((V7X_SUBLANES, S5_LANES), F32)],
        compiler_params=_params(("arbitrary",)),
        name="s5_core",
    )(u, bblk, cblk, p_re, p_im, q_re, q_im, d_row)


def _block_diag(blocks):
    nb, gb, r, c = blocks.shape
    eye = jnp.eye(gb, dtype=blocks.dtype)
    return jnp.einsum("ngrc,gk->ngrkc", blocks, eye).reshape(nb, gb * r, gb * c)


def _glu_kernel(x_ref, z_ref, wo_ref, wg_ref, g_ref, b_ref, o_ref, ob_ref):
    half = z_ref.shape[0] // 2
    for r in range(2):
        rows = slice(r * half, (r + 1) * half)
        z = z_ref[rows, :]
        y = _dot(z, wo_ref[...]) * jax.nn.sigmoid(_dot(z, wg_ref[...]))
        out = _layer_norm(ALPHA * x_ref[rows, :] + y, g_ref[...], b_ref[...])
        o_ref[rows, :] = out
        ob_ref[rows, :] = out.astype(BF16)


def _glu(x, z, wo, wg, g, b):
    s = x.shape[0]
    row = lambda width: pl.BlockSpec((GLU_ROWS, width), lambda i: (i, 0))
    const = lambda shape: pl.BlockSpec(shape, lambda i: (0, 0))
    return pl.pallas_call(
        _glu_kernel,
        grid=(s // GLU_ROWS,),
        in_specs=[row(D_MODEL), row(S5_WIDTH), const((S5_WIDTH, D_MODEL)), const((S5_WIDTH, D_MODEL)),
                  const((1, D_MODEL)), const((1, D_MODEL))],
        out_specs=[row(D_MODEL), row(D_MODEL)],
        out_shape=[jax.ShapeDtypeStruct((s, D_MODEL), F32), jax.ShapeDtypeStruct((s, D_MODEL), BF16)],
        compiler_params=_params(("parallel",)),
        name="glu",
    )(x, z, wo, wg, g, b)


def _attention_mixer(x, xb, w_in, b_f, w_out, g, b):
    scale = HEAD_DIM ** -0.5
    fox_q, fox_k, fox_v = (w_in[:, n * FOX_W:(n + 1) * FOX_W] for n in range(3))
    w_f = w_in[:, 3 * FOX_W:3 * FOX_W + N_FOX_HEADS]
    d0 = 3 * FOX_W + N_FOX_HEADS
    dil_q, dil_k, dil_v = (w_in[:, d0 + n * DIL_W:d0 + (n + 1) * DIL_W] for n in range(3))
    w_fox = jnp.concatenate([fox_q * (scale * LOG2E), fox_k, fox_v], axis=1).astype(BF16)
    w_dil = jnp.concatenate([dil_q * scale, dil_k, dil_v], axis=1).astype(BF16)
    w_f = jnp.pad(w_f, ((0, 0), (0, V7X_LANES - N_FOX_HEADS))).astype(BF16)
    qkv_fox, f_logit = _attn_in(xb, w_fox, w_f)
    c_rows = _forget_cumsum(f_logit, jnp.pad(b_f, (0, V7X_LANES - N_FOX_HEADS))[None, :])
    ya = _fox_attention(*_fox_prep(qkv_fox, c_rows))
    yb = _dilated_attention(_matmul(xb, w_dil, F32))
    return _attn_out(x, ya, yb, w_out.astype(BF16), g, b)


def _s5_mixer(x, xb, w_in, lam_re, lam_im, log_dt, b_re, b_im, c_re, c_im, d_skip, w_glu_out, w_glu_gate, g, b):
    u = _matmul(xb, w_in.astype(BF16), F32)
    p_re, p_im, q_re, q_im, bb_re, bb_im = _s5_prep(lam_re, lam_im, log_dt, b_re, b_im)
    lanes = lambda p: p.reshape(p.shape[0], S5_NB, S5_LANES).transpose(1, 0, 2)
    blk = lambda v: v.reshape(S5_NB, S5_GB, S5_GROUP, S5_STATE)
    bblk = jnp.concatenate([_block_diag(blk(bb_re)), _block_diag(blk(bb_im))], axis=2).astype(BF16)
    ct = lambda v: v.transpose(0, 2, 1).reshape(S5_NB, S5_GB, S5_STATE, S5_GROUP)
    cblk = jnp.concatenate([_block_diag(ct(c_re)), -_block_diag(ct(c_im))], axis=1).astype(BF16)
    z = _s5_core(u, bblk, cblk, lanes(p_re), lanes(p_im), lanes(q_re), lanes(q_im), d_skip[None, :])
    return _glu(x, z, w_glu_out.astype(BF16), w_glu_gate.astype(BF16), g, b)


def kernel(x, ffn1_w_gate, ffn1_w_up, ffn1_w_down, ffn2_w_gate, ffn2_w_up, ffn2_w_down, ln_gain, ln_bias,
           attn_w_in, attn_b_f, attn_w_out, s5_w_in, s5_lambda_re, s5_lambda_im, s5_log_dt, s5_b_re, s5_b_im,
           s5_c_re, s5_c_im, s5_d, s5_w_glu_out, s5_w_glu_gate):
    batch, seq, _ = x.shape
    assert batch == 1
    h = x.reshape(seq, D_MODEL)
    hb = h.astype(BF16)
    ln = lambda i, k: (ln_gain[i, k][None, :], ln_bias[i, k][None, :])
    for i in range(DEPTH):
        h, hb = _ffn(h, hb, *_ffn_weights_bf16(ffn1_w_gate, ffn1_w_up, ffn1_w_down, i), *ln(i, 0))
        j = i // 2
        if i % 2 == 0:
            h, hb = _attention_mixer(h, hb, attn_w_in[j], attn_b_f[j], attn_w_out[j], *ln(i, 1))
        else:
            h, hb = _s5_mixer(h, hb, s5_w_in[j], s5_lambda_re[j], s5_lambda_im[j], s5_log_dt[j], s5_b_re[j],
                              s5_b_im[j], s5_c_re[j], s5_c_im[j], s5_d[j], s5_w_glu_out[j], s5_w_glu_gate[j],
                              *ln(i, 1))
        h, hb = _ffn(h, hb, *_ffn_weights_bf16(ffn2_w_gate, ffn2_w_up, ffn2_w_down, i), *ln(i, 2))
    return h.reshape(batch, seq, D_MODEL)
```

```python
import functools
import math

import jax
import jax.numpy as jnp
from jax import lax
from jax.experimental import pallas as pl
from jax.experimental.pallas import tpu as pltpu

D_MODEL = 2048
DEPTH = 2
HEAD_DIM = 128
N_FOX_HEADS = 8
N_DIL_HEADS = 8
DIL_PAIRS = ((128, 1), (512, 4), (2048, 16))
DIL_BLOCK = 128
D_FF = 5632
S5_GROUP = 16
S5_WIDTH = 1024
S5_GROUPS = S5_WIDTH // S5_GROUP
S5_STATE = 64
ALPHA = (2.0 * DEPTH) ** 0.25
LN_EPS = 1e-5
FOX_W = N_FOX_HEADS * HEAD_DIM
DIL_W = N_DIL_HEADS * HEAD_DIM

F32 = jnp.float32
BF16 = jnp.bfloat16
NEG = -1e30
LOG2E = math.log2(math.e)
FOX_AUG = 128
FOX_VROWS = HEAD_DIM + 16

V7X_LANES = 128
V7X_SUBLANES = 8
V7X_VMEM_LIMIT_BYTES = 56 * 1024 * 1024

FFN_ROWS = 512
FFN_COLS = 512
FFN_CHUNKS = D_FF // FFN_COLS
FFN_GROUP = 256
FFN_RING_SLOTS = 3
assert (FFN_CHUNKS + 1) % FFN_RING_SLOTS == 0
MM_ROWS = 1024
MM_COLS = 1024
FOX_BLOCK = 1024
DIL_TILE = 2048
OUT_ROWS = 512
GLU_ROWS = 512
CUMSUM_ROWS = 512
CAST_STEPS = 8
S5_ROWS = 256
S5_SEG = S5_ROWS // V7X_SUBLANES
S5_GB = 16
S5_NB = S5_GROUPS // S5_GB
S5_LANES = S5_GB * S5_STATE


def _params(sem):
    return pltpu.CompilerParams(dimension_semantics=sem, vmem_limit_bytes=V7X_VMEM_LIMIT_BYTES)


def _layer_norm(v, g, b):
    mu = jnp.mean(v, axis=-1, keepdims=True)
    d = v - mu
    var = jnp.mean(d * d, axis=-1, keepdims=True)
    return d * lax.rsqrt(var + LN_EPS) * g + b


def _dot(a, b):
    return jnp.dot(a, b, preferred_element_type=F32)


def _dot_nt(a, b):
    return lax.dot_general(a, b, (((1,), (1,)), ((), ())), preferred_element_type=F32)


def _ffn_kernel(*refs, cast_input):
    if cast_input:
        x_ref, wgu_hbm, wd_hbm, g_ref, b_ref, o_ref, ob_ref, h_ref, wgu_buf, wd_buf, sem, xb_ref = refs
        xb_ref[...] = x_ref[...].astype(BF16)
    else:
        x_ref, xb_ref, wgu_hbm, wd_hbm, g_ref, b_ref, o_ref, ob_ref, h_ref, wgu_buf, wd_buf, sem = refs
    i = pl.program_id(0)
    last_tile = pl.num_programs(0) - 1
    chunks = FFN_CHUNKS
    nsub = chunks + 1

    def ring(sub, slot, op):
        @pl.when(sub < chunks)
        def _():
            c = jnp.minimum(sub, chunks - 1)
            op(pltpu.make_async_copy(wgu_hbm.at[c], wgu_buf.at[slot], sem.at[0, slot]))

        @pl.when(sub >= 1)
        def _():
            c = jnp.maximum(sub - 1, 0)
            op(pltpu.make_async_copy(wd_hbm.at[c], wd_buf.at[slot], sem.at[1, slot]))

    start = lambda cp: cp.start()
    wait = lambda cp: cp.wait()

    @pl.when(i == 0)
    def _():
        ring(jnp.int32(0), 0, start)
        ring(jnp.int32(1), 1, start)

    def fetch_ahead_and_wait(j):
        ahead = j + 2
        wraps = ahead >= nsub

        @pl.when(jnp.logical_or(jnp.logical_not(wraps), i < last_tile))
        def _():
            ring(jnp.where(wraps, ahead - nsub, ahead), ahead % FFN_RING_SLOTS, start)

        ring(j, j % FFN_RING_SLOTS, wait)

    def hidden(slot):
        gu = _dot(xb_ref[...], wgu_buf[slot])
        parts = []
        for n in range(FFN_COLS // FFN_GROUP):
            gate = gu[:, 2 * n * FFN_GROUP:(2 * n + 1) * FFN_GROUP]
            up = gu[:, (2 * n + 1) * FFN_GROUP:(2 * n + 2) * FFN_GROUP]
            parts.append((gate * jax.nn.sigmoid(gate) * up).astype(BF16))
        return jnp.concatenate(parts, axis=1)

    def down(j, slot):
        return _dot(h_ref[(j - 1) % 2], wd_buf[slot])

    fetch_ahead_and_wait(jnp.int32(0))
    h_ref[0] = hidden(0)

    fetch_ahead_and_wait(jnp.int32(1))
    o_ref[...] = down(1, 1)
    h_ref[1] = hidden(1)

    def body(j, carry):
        fetch_ahead_and_wait(j)
        slot = j % FFN_RING_SLOTS
        o_ref[...] += down(j, slot)
        h_ref[j % 2] = hidden(slot)
        return carry

    lax.fori_loop(2, chunks, body, 0)

    fetch_ahead_and_wait(jnp.int32(chunks))
    acc = o_ref[...] + down(chunks, chunks % FFN_RING_SLOTS)
    y = _layer_norm(ALPHA * x_ref[...] + 0.5 * acc, g_ref[...], b_ref[...])
    o_ref[...] = y
    ob_ref[...] = y.astype(BF16)


def _ffn(x, xb, wgu, wd, g, b):
    s = x.shape[0]
    row = pl.BlockSpec((FFN_ROWS, D_MODEL), lambda i: (i, 0))
    hbm = pl.BlockSpec(memory_space=pl.ANY)
    vec = pl.BlockSpec((1, D_MODEL), lambda i: (0, 0))
    scratch = [
        pltpu.VMEM((2, FFN_ROWS, FFN_COLS), BF16),
        pltpu.VMEM((FFN_RING_SLOTS, D_MODEL, 2 * FFN_COLS), BF16),
        pltpu.VMEM((FFN_RING_SLOTS, FFN_COLS, D_MODEL), BF16),
        pltpu.SemaphoreType.DMA((2, FFN_RING_SLOTS)),
    ]
    cast_input = xb is None
    return pl.pallas_call(
        functools.partial(_ffn_kernel, cast_input=cast_input),
        grid=(s // FFN_ROWS,),
        in_specs=[row] + ([] if cast_input else [row]) + [hbm, hbm, vec, vec],
        out_specs=[row, row],
        out_shape=[jax.ShapeDtypeStruct((s, D_MODEL), F32), jax.ShapeDtypeStruct((s, D_MODEL), BF16)],
        scratch_shapes=scratch + ([pltpu.VMEM((FFN_ROWS, D_MODEL), BF16)] if cast_input else []),
        compiler_params=_params(("arbitrary",)),
        name="ffn",
    )(*([x] if cast_input else [x, xb]), wgu, wd, g, b)


def _cast_kernel(wg_ref, wu_ref, wd_ref, ogu_ref, od_ref):
    for n in range(D_FF // FFN_GROUP):
        c, k = divmod(n, FFN_COLS // FFN_GROUP)
        src = slice(n * FFN_GROUP, (n + 1) * FFN_GROUP)
        ogu_ref[c, :, 2 * k * FFN_GROUP:(2 * k + 1) * FFN_GROUP] = wg_ref[:, src].astype(BF16)
        ogu_ref[c, :, (2 * k + 1) * FFN_GROUP:(2 * k + 2) * FFN_GROUP] = wu_ref[:, src].astype(BF16)
    od_ref[...] = wd_ref[...].astype(BF16)


def _ffn_weights_bf16(w_gate, w_up, w_down, layer):
    steps = CAST_STEPS
    up_rows, down_rows = D_MODEL // steps, D_FF // steps
    up_in = pl.BlockSpec((None, up_rows, D_FF), lambda r: (layer, r, 0))
    up_out = pl.BlockSpec((FFN_CHUNKS, up_rows, 2 * FFN_COLS), lambda r: (0, r, 0))
    up_shape = jax.ShapeDtypeStruct((FFN_CHUNKS, D_MODEL, 2 * FFN_COLS), BF16)
    wgu, wd = pl.pallas_call(
        _cast_kernel,
        grid=(steps,),
        in_specs=[up_in, up_in, pl.BlockSpec((None, down_rows, D_MODEL), lambda r: (layer, r, 0))],
        out_specs=[up_out, pl.BlockSpec((down_rows, D_MODEL), lambda r: (r, 0))],
        out_shape=[up_shape, jax.ShapeDtypeStruct((D_FF, D_MODEL), BF16)],
        compiler_params=_params(("parallel",)),
        name="ffn_weights_bf16",
    )(w_gate, w_up, w_down)
    return wgu, wd.reshape(FFN_CHUNKS, FFN_COLS, D_MODEL)


def _attn_in_kernel(x_ref, w_ref, wf_ref, o_ref, f_ref):
    x = x_ref[...]
    o_ref[...] = _dot(x, w_ref[...]).astype(o_ref.dtype)

    @pl.when(pl.program_id(1) == 0)
    def _():
        f_ref[...] = _dot(x, wf_ref[...])


def _attn_in(xb, w, wf):
    s, n = xb.shape[0], w.shape[1]
    return pl.pallas_call(
        _attn_in_kernel,
        grid=(s // MM_ROWS, n // MM_COLS),
        in_specs=[
            pl.BlockSpec((MM_ROWS, D_MODEL), lambda i, j: (i, 0)),
            pl.BlockSpec((D_MODEL, MM_COLS), lambda i, j: (0, j)),
            pl.BlockSpec((D_MODEL, V7X_LANES), lambda i, j: (0, 0)),
        ],
        out_specs=[
            pl.BlockSpec((MM_ROWS, MM_COLS), lambda i, j: (i, j)),
            pl.BlockSpec((MM_ROWS, V7X_LANES), lambda i, j: (i, 0)),
        ],
        out_shape=[jax.ShapeDtypeStruct((s, n), BF16), jax.ShapeDtypeStruct((s, V7X_LANES), F32)],
        compiler_params=_params(("parallel", "arbitrary")),
        name="attn_in",
    )(xb, w, wf)


def _mm_kernel(x_ref, w_ref, o_ref):
    o_ref[...] = _dot(x_ref[...], w_ref[...]).astype(o_ref.dtype)


def _matmul(xb, w, out_dtype):
    s, n = xb.shape[0], w.shape[1]
    return pl.pallas_call(
        _mm_kernel,
        grid=(s // MM_ROWS, n // MM_COLS),
        in_specs=[
            pl.BlockSpec((MM_ROWS, xb.shape[1]), lambda i, j: (i, 0)),
            pl.BlockSpec((xb.shape[1], MM_COLS), lambda i, j: (0, j)),
        ],
        out_specs=pl.BlockSpec((MM_ROWS, MM_COLS), lambda i, j: (i, j)),
        out_shape=jax.ShapeDtypeStruct((s, n), out_dtype),
        compiler_params=_params(("parallel", "parallel")),
        name="matmul",
    )(xb, w)


def _split3(v):
    hi = v.astype(BF16)
    r1 = v - hi.astype(F32)
    mid = r1.astype(BF16)
    lo = (r1 - mid.astype(F32)).astype(BF16)
    return hi, mid, lo


def _cumsum_kernel(f_ref, bf_ref, c_ref, carry_ref):
    @pl.when(pl.program_id(0) == 0)
    def _():
        carry_ref[...] = jnp.zeros_like(carry_ref)

    z = f_ref[...] + bf_ref[...]
    log_f = jnp.minimum(z, 0.0) - jnp.log1p(jnp.exp(-jnp.abs(z)))
    n = z.shape[0]
    tri = (lax.broadcasted_iota(jnp.int32, (n, n), 0) >= lax.broadcasted_iota(jnp.int32, (n, n), 1)).astype(BF16)
    hi, mid, lo = _split3(log_f)
    c = _dot(tri, hi) + _dot(tri, mid) + _dot(tri, lo) + carry_ref[0:1, :]
    c_ref[...] = c
    carry_ref[...] = jnp.broadcast_to(c[n - 1:n, :], carry_ref.shape)


def _forget_cumsum(f_logit, b_f_row):
    s = f_logit.shape[0]
    return pl.pallas_call(
        _cumsum_kernel,
        grid=(s // CUMSUM_ROWS,),
        in_specs=[
            pl.BlockSpec((CUMSUM_ROWS, V7X_LANES), lambda i: (i, 0)),
            pl.BlockSpec((1, V7X_LANES), lambda i: (0, 0)),
        ],
        out_specs=pl.BlockSpec((CUMSUM_ROWS, V7X_LANES), lambda i: (i, 0)),
        out_shape=jax.ShapeDtypeStruct((s, V7X_LANES), F32),
        scratch_shapes=[pltpu.VMEM((V7X_SUBLANES, V7X_LANES), F32)],
        compiler_params=_params(("arbitrary",)),
        name="forget_cumsum",
    )(f_logit, b_f_row)


def _fox_prep_kernel(qkv_ref, c_ref, qt_ref, ka_ref, vt_ref):
    rows = qkv_ref.shape[0]
    lane = lax.broadcasted_iota(jnp.int32, (rows, V7X_LANES), 1)
    bias = c_ref[...] * (-LOG2E)
    ones_rows = (lax.broadcasted_iota(jnp.int32, (FOX_AUG, rows), 0) < 3).astype(BF16)
    sum_rows = (lax.broadcasted_iota(jnp.int32, (FOX_VROWS - HEAD_DIM, rows), 0) < 1).astype(BF16)
    for h in range(N_FOX_HEADS):
        hs = slice(h * HEAD_DIM, (h + 1) * HEAD_DIM)
        a0 = h * (HEAD_DIM + FOX_AUG)
        qt_ref[a0:a0 + HEAD_DIM, :] = qkv_ref[:, hs].astype(F32).T.astype(BF16)
        qt_ref[a0 + HEAD_DIM:a0 + HEAD_DIM + FOX_AUG, :] = ones_rows
        v0 = h * FOX_VROWS
        vt_ref[v0:v0 + HEAD_DIM, :] = (
            qkv_ref[:, 2 * FOX_W + h * HEAD_DIM:2 * FOX_W + (h + 1) * HEAD_DIM].astype(F32).T.astype(BF16))
        vt_ref[v0 + HEAD_DIM:v0 + FOX_VROWS, :] = sum_rows
        hi, mid, lo = _split3(jnp.broadcast_to(bias[:, h:h + 1], (rows, V7X_LANES)))
        aug = jnp.where(lane == 0, hi.astype(F32), jnp.where(lane == 1, mid.astype(F32),
                                                              jnp.where(lane == 2, lo.astype(F32), 0.0)))
        ka_ref[:, a0:a0 + HEAD_DIM] = qkv_ref[:, FOX_W + h * HEAD_DIM:FOX_W + (h + 1) * HEAD_DIM]
        ka_ref[:, a0 + HEAD_DIM:a0 + HEAD_DIM + FOX_AUG] = aug.astype(BF16)


def _fox_prep(qkv, c_rows):
    s = qkv.shape[0]
    wide = N_FOX_HEADS * (HEAD_DIM + FOX_AUG)
    vrows = N_FOX_HEADS * FOX_VROWS
    return pl.pallas_call(
        _fox_prep_kernel,
        grid=(s // FOX_BLOCK,),
        in_specs=[pl.BlockSpec((FOX_BLOCK, 3 * FOX_W), lambda i: (i, 0)),
                  pl.BlockSpec((FOX_BLOCK, V7X_LANES), lambda i: (i, 0))],
        out_specs=[pl.BlockSpec((wide, FOX_BLOCK), lambda i: (0, i)),
                   pl.BlockSpec((FOX_BLOCK, wide), lambda i: (i, 0)),
                   pl.BlockSpec((vrows, FOX_BLOCK), lambda i: (0, i))],
        out_shape=[jax.ShapeDtypeStruct((wide, s), BF16), jax.ShapeDtypeStruct((s, wide), BF16),
                   jax.ShapeDtypeStruct((vrows, s), BF16)],
        compiler_params=_params(("parallel",)),
        name="fox_prep",
    )(qkv, c_rows)


def _fox_kernel(qi_ref, kj_ref, qt_ref, ka_ref, vt_ref, o_ref, m_ref, acc_ref, s_ref):
    t = pl.program_id(0)
    i = qi_ref[t]
    j = kj_ref[t]
    blk = ka_ref.shape[0]
    aw = HEAD_DIM + FOX_AUG

    @pl.when(j == 0)
    def _():
        m_ref[...] = jnp.full_like(m_ref, NEG)
        acc_ref[...] = jnp.zeros_like(acc_ref)

    def scores(h):
        return _dot(ka_ref[:, h * aw:(h + 1) * aw], qt_ref[h * aw:(h + 1) * aw, :])

    def step(diagonal):
        if diagonal:
            keep = lax.broadcasted_iota(jnp.int32, (blk, blk), 0) <= lax.broadcasted_iota(jnp.int32, (blk, blk), 1)
        s_ref[0] = scores(0)
        for h in range(N_FOX_HEADS):
            if h + 1 < N_FOX_HEADS:
                s_ref[(h + 1) % 2] = scores(h + 1)
            s = s_ref[h % 2]
            if diagonal:
                s = jnp.where(keep, s, NEG)
            m_prev = m_ref[h:h + 1, :]
            m_new = jnp.maximum(m_prev, jnp.max(s, axis=0, keepdims=True))
            p = jnp.exp2(s - m_new).astype(BF16)
            alpha = jnp.exp2(m_prev - m_new)
            m_ref[h:h + 1, :] = m_new
            vs = slice(h * FOX_VROWS, (h + 1) * FOX_VROWS)
            acc_ref[vs, :] = alpha * acc_ref[vs, :] + _dot(vt_ref[vs, :], p)

    @pl.when(j < i)
    def _():
        step(False)

    @pl.when(j == i)
    def _():
        step(True)
        for h in range(N_FOX_HEADS):
            v0 = h * FOX_VROWS
            out_t = acc_ref[v0:v0 + HEAD_DIM, :] / acc_ref[v0 + HEAD_DIM:v0 + HEAD_DIM + 1, :]
            o_ref[:, h * HEAD_DIM:(h + 1) * HEAD_DIM] = out_t.T.astype(o_ref.dtype)


def _fox_attention(qt, ka, vt):
    s = ka.shape[0]
    nb = s // FOX_BLOCK
    qi = jnp.asarray([i for i in range(nb) for _ in range(i + 1)], jnp.int32)
    kj = jnp.asarray([j for i in range(nb) for j in range(i + 1)], jnp.int32)
    wide = ka.shape[1]
    vrows = vt.shape[0]
    grid_spec = pltpu.PrefetchScalarGridSpec(
        num_scalar_prefetch=2,
        grid=(qi.shape[0],),
        in_specs=[
            pl.BlockSpec((wide, FOX_BLOCK), lambda t, qi, kj: (0, qi[t])),
            pl.BlockSpec((FOX_BLOCK, wide), lambda t, qi, kj: (kj[t], 0)),
            pl.BlockSpec((vrows, FOX_BLOCK), lambda t, qi, kj: (0, kj[t])),
        ],
        out_specs=pl.BlockSpec((FOX_BLOCK, FOX_W), lambda t, qi, kj: (qi[t], 0)),
        scratch_shapes=[
            pltpu.VMEM((N_FOX_HEADS, FOX_BLOCK), F32),
            pltpu.VMEM((vrows, FOX_BLOCK), F32),
            pltpu.VMEM((2, FOX_BLOCK, FOX_BLOCK), F32),
        ],
    )
    return pl.pallas_call(
        _fox_kernel,
        grid_spec=grid_spec,
        out_shape=jax.ShapeDtypeStruct((s, FOX_W), BF16),
        compiler_params=_params(("arbitrary",)),
        name="fox_attention",
    )(qi, kj, qt, ka, vt)


def _dil_kernel(q_ref, k_ref, v_ref, kp_ref, vp_ref, y_ref, o_scr, l_scr):
    has_prev_tile = pl.program_id(0) > 0
    b = DIL_BLOCK
    tile = q_ref.shape[0]
    jr = lax.broadcasted_iota(jnp.int32, (b, 2 * b), 0)
    mc = lax.broadcasted_iota(jnp.int32, (b, 2 * b), 1)
    band = (mc >= jr) & (mc <= jr + b)
    band_first = band & ((mc >= b) | has_prev_tile)

    def rows(ref, start, d):
        return ref[pl.ds(start, b), :] if d == 1 else ref[pl.ds(start, b, stride=d), :]

    for g, (_, d) in enumerate(DIL_PAIRS):
        for r in range(d):
            first = tile - b * d + r
            k_blk = [rows(kp_ref, first, d).astype(BF16)]
            v_blk = [rows(vp_ref, first, d).astype(BF16)]
            for n in range(tile // (b * d)):
                cur = r + n * b * d
                k_blk.append(rows(k_ref, cur, d).astype(BF16))
                v_blk.append(rows(v_ref, cur, d).astype(BF16))
                k2 = jnp.concatenate(k_blk[n:n + 2], axis=0)
                v2 = jnp.concatenate(v_blk[n:n + 2], axis=0)
                keep = band if n > 0 else band_first
                s = jnp.where(keep, _dot_nt(rows(q_ref, cur, d).astype(BF16), k2), NEG)
                m = jnp.max(s, axis=1, keepdims=True)
                p = jnp.exp(s - m)
                l = jnp.sum(p, axis=1, keepdims=True)
                o = _dot(p.astype(BF16), v2) / l
                lse = jnp.broadcast_to(m + jnp.log(l), (b, HEAD_DIM))
                if d == 1:
                    o_scr[g, pl.ds(cur, b), :] = o
                    l_scr[g, pl.ds(cur, b), :] = lse
                else:
                    o_scr[g, pl.ds(cur, b, stride=d), :] = o
                    l_scr[g, pl.ds(cur, b, stride=d), :] = lse

    l1, l2, l3 = l_scr[0], l_scr[1], l_scr[2]
    mx = jnp.maximum(jnp.maximum(l1, l2), l3)
    e1, e2, e3 = jnp.exp(l1 - mx), jnp.exp(l2 - mx), jnp.exp(l3 - mx)
    den = e1 + e2 + e3
    y_ref[...] = ((e1 / den) * o_scr[0] + (e2 / den) * o_scr[1] + (e3 / den) * o_scr[2]).astype(y_ref.dtype)


def _dilated_attention(qkv):
    s = qkv.shape[0]
    cur = lambda c: pl.BlockSpec((DIL_TILE, HEAD_DIM), lambda t, h: (t, c * N_DIL_HEADS + h))
    prev = lambda c: pl.BlockSpec((DIL_TILE, HEAD_DIM), lambda t, h: (jnp.maximum(t - 1, 0), c * N_DIL_HEADS + h))
    return pl.pallas_call(
        _dil_kernel,
        grid=(s // DIL_TILE, N_DIL_HEADS),
        in_specs=[cur(0), cur(1), cur(2), prev(1), prev(2)],
        out_specs=pl.BlockSpec((DIL_TILE, HEAD_DIM), lambda t, h: (t, h)),
        out_shape=jax.ShapeDtypeStruct((s, DIL_W), BF16),
        scratch_shapes=[pltpu.VMEM((len(DIL_PAIRS), DIL_TILE, HEAD_DIM), F32),
                        pltpu.VMEM((len(DIL_PAIRS), DIL_TILE, HEAD_DIM), F32)],
        compiler_params=_params(("parallel", "parallel")),
        name="dilated_attention",
    )(qkv, qkv, qkv, qkv, qkv)


def _attn_out_kernel(x_ref, ya_ref, yb_ref, w_ref, g_ref, b_ref, o_ref, ob_ref):
    half = x_ref.shape[0] // 2
    for r in range(2):
        rows = slice(r * half, (r + 1) * half)
        y = _dot(jnp.concatenate([ya_ref[rows, :], yb_ref[rows, :]], axis=1), w_ref[...])
        out = _layer_norm(ALPHA * x_ref[rows, :] + y, g_ref[...], b_ref[...])
        o_ref[rows, :] = out
        ob_ref[rows, :] = out.astype(BF16)


def _attn_out(x, ya, yb, w, g, b):
    s = x.shape[0]
    row = lambda width: pl.BlockSpec((OUT_ROWS, width), lambda i: (i, 0))
    const = lambda shape: pl.BlockSpec(shape, lambda i: (0, 0))
    return pl.pallas_call(
        _attn_out_kernel,
        grid=(s // OUT_ROWS,),
        in_specs=[row(D_MODEL), row(FOX_W), row(DIL_W),
                  const((FOX_W + DIL_W, D_MODEL)), const((1, D_MODEL)), const((1, D_MODEL))],
        out_specs=[row(D_MODEL), row(D_MODEL)],
        out_shape=[jax.ShapeDtypeStruct((s, D_MODEL), F32), jax.ShapeDtypeStruct((s, D_MODEL), BF16)],
        compiler_params=_params(("parallel",)),
        name="attn_out",
    )(x, ya, yb, w, g, b)


def _cmul(ar, ai, br, bi):
    return ar * br - ai * bi, ar * bi + ai * br


def _s5_prep_kernel(lrg_ref, lig_ref, ldtg_ref, lr_ref, li_ref, ldt_ref, br_ref, bi_ref,
                    pr_ref, pi_ref, qr_ref, qi_ref, bbr_ref, bbi_ref):
    def a_bar(lr, li, ldt):
        dt = jnp.exp(ldt)
        mag = jnp.exp(lr * dt)
        return mag * jnp.cos(li * dt), mag * jnp.sin(li * dt)

    lr, li = lr_ref[...], li_ref[...]
    ar, ai = a_bar(lr, li, ldt_ref[...])
    den = lr * lr + li * li
    cr = ((ar - 1.0) * lr + ai * li) / den
    ci = (ai * lr - (ar - 1.0) * li) / den
    bbr_ref[...], bbi_ref[...] = _cmul(cr, ci, br_ref[...], bi_ref[...])

    ar, ai = a_bar(lrg_ref[...], lig_ref[...], ldtg_ref[...])
    pr, pi = ar, ai
    pr_ref[0], pi_ref[0] = pr, pi
    for k in range(1, S5_SEG):
        pr, pi = _cmul(pr, pi, ar, ai)
        pr_ref[k], pi_ref[k] = pr, pi
    sr, si = pr, pi
    qr_ref[0], qi_ref[0] = sr, si
    for k in range(1, V7X_SUBLANES):
        pr, pi = _cmul(pr, pi, sr, si)
        qr_ref[k], qi_ref[k] = pr, pi


def _s5_prep(lam_re, lam_im, log_dt, b_re, b_im):
    ldt = jnp.broadcast_to(log_dt[:, None], lam_re.shape)
    rep = lambda v: jnp.repeat(v, S5_GROUP, axis=0)
    rows = S5_GROUPS * S5_GROUP
    bt = lambda v: v.transpose(0, 2, 1).reshape(rows, S5_STATE)
    mat = jax.ShapeDtypeStruct((rows, S5_STATE), F32)
    pw = jax.ShapeDtypeStruct((S5_SEG, S5_GROUPS, S5_STATE), F32)
    qw = jax.ShapeDtypeStruct((V7X_SUBLANES, S5_GROUPS, S5_STATE), F32)
    return pl.pallas_call(_s5_prep_kernel, out_shape=[pw, pw, qw, qw, mat, mat], name="s5_prep")(
        lam_re, lam_im, ldt, rep(lam_re), rep(lam_im), rep(ldt), bt(b_re), bt(b_im))


def _s5_core_kernel(u_ref, bblk_ref, cblk_ref, pr_ref, pi_ref, qr_ref, qi_ref, d_ref, z_ref,
                    up_ref, sr_ref, si_ref, cr_ref, ci_ref):
    @pl.when(pl.program_id(0) == 0)
    def _():
        cr_ref[...] = jnp.zeros_like(cr_ref)
        ci_ref[...] = jnp.zeros_like(ci_ref)

    sl = V7X_SUBLANES
    rows = u_ref.shape[0]
    ri = lax.broadcasted_iota(jnp.int32, (rows, rows), 0)
    ci = lax.broadcasted_iota(jnp.int32, (rows, rows), 1)
    perm = ((ri // sl == ci % S5_SEG) & (ri % sl == ci // S5_SEG)).astype(BF16)
    unperm = ((ci // sl == ri % S5_SEG) & (ci % sl == ri // S5_SEG)).astype(BF16)
    hi, mid, lo = _split3(u_ref[...])
    up_ref[...] = _dot(perm, hi) + _dot(perm, mid) + _dot(perm, lo)

    sub = lax.broadcasted_iota(jnp.int32, (sl, S5_LANES), 0)
    width = S5_GB * S5_GROUP
    ys = []
    for jb in range(S5_NB):
        bu = _dot(up_ref[:, jb * width:(jb + 1) * width].astype(BF16), bblk_ref[jb])
        a_re, a_im = pr_ref[jb, 0:1, :], pi_ref[jb, 0:1, :]
        xr, xi = bu[0:sl, :S5_LANES], bu[0:sl, S5_LANES:]
        sr_ref[0:sl, :], si_ref[0:sl, :] = xr, xi
        for j in range(1, S5_SEG):
            tr, ti = _cmul(a_re, a_im, xr, xi)
            xr, xi = tr + bu[j * sl:(j + 1) * sl, :S5_LANES], ti + bu[j * sl:(j + 1) * sl, S5_LANES:]
            sr_ref[j * sl:(j + 1) * sl, :], si_ref[j * sl:(j + 1) * sl, :] = xr, xi

        q_re, q_im = qr_ref[jb], qi_ref[jb]
        er, ei = xr, xi
        for k in (1, 2, 4):
            tr = jnp.where(sub >= k, pltpu.roll(er, k, 0), 0.0)
            ti = jnp.where(sub >= k, pltpu.roll(ei, k, 0), 0.0)
            mr, mi = _cmul(q_re[k - 1:k, :], q_im[k - 1:k, :], tr, ti)
            er, ei = er + mr, ei + mi
        in_re, in_im = cr_ref[jb:jb + 1, :], ci_ref[jb:jb + 1, :]
        mr, mi = _cmul(q_re, q_im, in_re, in_im)
        er, ei = er + mr, ei + mi
        cr_ref[jb:jb + 1, :], ci_ref[jb:jb + 1, :] = er[sl - 1:, :], ei[sl - 1:, :]
        cin_re = jnp.where(sub >= 1, pltpu.roll(er, 1, 0), in_re)
        cin_im = jnp.where(sub >= 1, pltpu.roll(ei, 1, 0), in_im)
        fr, fi = _cmul(pr_ref[jb][:, None, :], pi_ref[jb][:, None, :], cin_re[None], cin_im[None])
        shape3 = (S5_SEG, sl, S5_LANES)
        state = jnp.concatenate([(sr_ref[...].reshape(shape3) + fr).reshape(-1, S5_LANES),
                                 (si_ref[...].reshape(shape3) + fi).reshape(-1, S5_LANES)], axis=1)
        ys.append(_dot(state.astype(BF16), cblk_ref[jb]))
    y = jnp.concatenate(ys, axis=1) + d_ref[...] * up_ref[...]
    z = 0.5 * y * (1.0 + jnp.tanh(math.sqrt(2.0 / math.pi) * (y + 0.044715 * (y * y * y))))
    z_ref[...] = _dot(unperm, z.astype(BF16)).astype(z_ref.dtype)


def _s5_core(u, bblk, cblk, p_re, p_im, q_re, q_im, d_row):
    s = u.shape[0]
    const = lambda shape: pl.BlockSpec(shape, lambda t: (0,) * len(shape))
    return pl.pallas_call(
        _s5_core_kernel,
        grid=(s // S5_ROWS,),
        in_specs=[pl.BlockSpec((S5_ROWS, S5_WIDTH), lambda t: (t, 0)),
                  const(bblk.shape), const(cblk.shape), const(p_re.shape), const(p_im.shape),
                  const(q_re.shape), const(q_im.shape), const((1, S5_WIDTH))],
        out_specs=pl.BlockSpec((S5_ROWS, S5_WIDTH), lambda t: (t, 0)),
        out_shape=jax.ShapeDtypeStruct((s, S5_WIDTH), BF16),
        scratch_shapes=[pltpu.VMEM((S5_ROWS, S5_WIDTH), F32),
                        pltpu.VMEM((S5_ROWS, S5_LANES), F32), pltpu.VMEM((S5_ROWS, S5_LANES), F32),
                        pltpu.VMEM((V7X_SUBLANES, S5_LANES), F32), pltpu.VMEM((V7X_SUBLANES, S5_LANES), F32)],
        compiler_params=_params(("arbitrary",)),
        name="s5_core",
    )(u, bblk, cblk, p_re, p_im, q_re, q_im, d_row)


def _block_diag(blocks):
    nb, gb, r, c = blocks.shape
    eye = jnp.eye(gb, dtype=blocks.dtype)
    return jnp.einsum("ngrc,gk->ngrkc", blocks, eye).reshape(nb, gb * r, gb * c)


def _glu_kernel(x_ref, z_ref, wo_ref, wg_ref, g_ref, b_ref, o_ref, ob_ref):
    half = z_ref.shape[0] // 2
    for r in range(2):
        rows = slice(r * half, (r + 1) * half)
        z = z_ref[rows, :]
        y = _dot(z, wo_ref[...]) * jax.nn.sigmoid(_dot(z, wg_ref[...]))
        out = _layer_norm(ALPHA * x_ref[rows, :] + y, g_ref[...], b_ref[...])
        o_ref[rows, :] = out
        ob_ref[rows, :] = out.astype(BF16)


def _glu(x, z, wo, wg, g, b):
    s = x.shape[0]
    row = lambda width: pl.BlockSpec((GLU_ROWS, width), lambda i: (i, 0))
    const = lambda shape: pl.BlockSpec(shape, lambda i: (0, 0))
    return pl.pallas_call(
        _glu_kernel,
        grid=(s // GLU_ROWS,),
        in_specs=[row(D_MODEL), row(S5_WIDTH), const((S5_WIDTH, D_MODEL)), const((S5_WIDTH, D_MODEL)),
                  const((1, D_MODEL)), const((1, D_MODEL))],
        out_specs=[row(D_MODEL), row(D_MODEL)],
        out_shape=[jax.ShapeDtypeStruct((s, D_MODEL), F32), jax.ShapeDtypeStruct((s, D_MODEL), BF16)],
        compiler_params=_params(("parallel",)),
        name="glu",
    )(x, z, wo, wg, g, b)


def _attention_mixer(x, xb, w_in, b_f, w_out, g, b):
    scale = HEAD_DIM ** -0.5
    fox_q, fox_k, fox_v = (w_in[:, n * FOX_W:(n + 1) * FOX_W] for n in range(3))
    w_f = w_in[:, 3 * FOX_W:3 * FOX_W + N_FOX_HEADS]
    d0 = 3 * FOX_W + N_FOX_HEADS
    dil_q, dil_k, dil_v = (w_in[:, d0 + n * DIL_W:d0 + (n + 1) * DIL_W] for n in range(3))
    w_fox = jnp.concatenate([fox_q * (scale * LOG2E), fox_k, fox_v], axis=1).astype(BF16)
    w_dil = jnp.concatenate([dil_q * scale, dil_k, dil_v], axis=1).astype(BF16)
    w_f = jnp.pad(w_f, ((0, 0), (0, V7X_LANES - N_FOX_HEADS))).astype(BF16)
    qkv_fox, f_logit = _attn_in(xb, w_fox, w_f)
    c_rows = _forget_cumsum(f_logit, jnp.pad(b_f, (0, V7X_LANES - N_FOX_HEADS))[None, :])
    ya = _fox_attention(*_fox_prep(qkv_fox, c_rows))
    yb = _dilated_attention(_matmul(xb, w_dil, F32))
    return _attn_out(x, ya, yb, w_out.astype(BF16), g, b)


def _s5_mixer(x, xb, w_in, lam_re, lam_im, log_dt, b_re, b_im, c_re, c_im, d_skip, w_glu_out, w_glu_gate, g, b):
    u = _matmul(xb, w_in.astype(BF16), F32)
    p_re, p_im, q_re, q_im, bb_re, bb_im = _s5_prep(lam_re, lam_im, log_dt, b_re, b_im)
    lanes = lambda p: p.reshape(p.shape[0], S5_NB, S5_LANES).transpose(1, 0, 2)
    blk = lambda v: v.reshape(S5_NB, S5_GB, S5_GROUP, S5_STATE)
    bblk = jnp.concatenate([_block_diag(blk(bb_re)), _block_diag(blk(bb_im))], axis=2).astype(BF16)
    ct = lambda v: v.transpose(0, 2, 1).reshape(S5_NB, S5_GB, S5_STATE, S5_GROUP)
    cblk = jnp.concatenate([_block_diag(ct(c_re)), -_block_diag(ct(c_im))], axis=1).astype(BF16)
    z = _s5_core(u, bblk, cblk, lanes(p_re), lanes(p_im), lanes(q_re), lanes(q_im), d_skip[None, :])
    return _glu(x, z, w_glu_out.astype(BF16), w_glu_gate.astype(BF16), g, b)


def kernel(x, ffn1_w_gate, ffn1_w_up, ffn1_w_down, ffn2_w_gate, ffn2_w_up, ffn2_w_down, ln_gain, ln_bias,
           attn_w_in, attn_b_f, attn_w_out, s5_w_in, s5_lambda_re, s5_lambda_im, s5_log_dt, s5_b_re, s5_b_im,
           s5_c_re, s5_c_im, s5_d, s5_w_glu_out, s5_w_glu_gate):
    batch, seq, _ = x.shape
    assert batch == 1
    h = x.reshape(seq, D_MODEL)
    hb = None
    ln = lambda i, k: (ln_gain[i, k][None, :], ln_bias[i, k][None, :])
    for i in range(DEPTH):
        h, hb = _ffn(h, hb, *_ffn_weights_bf16(ffn1_w_gate, ffn1_w_up, ffn1_w_down, i), *ln(i, 0))
        j = i // 2
        if i % 2 == 0:
            h, hb = _attention_mixer(h, hb, attn_w_in[j], attn_b_f[j], attn_w_out[j], *ln(i, 1))
        else:
            h, hb = _s5_mixer(h, hb, s5_w_in[j], s5_lambda_re[j], s5_lambda_im[j], s5_log_dt[j], s5_b_re[j],
                              s5_b_im[j], s5_c_re[j], s5_c_im[j], s5_d[j], s5_w_glu_out[j], s5_w_glu_gate[j],
                              *ln(i, 1))
        h, hb = _ffn(h, hb, *_ffn_weights_bf16(ffn2_w_gate, ffn2_w_up, ffn2_w_down, i), *ln(i, 2))
    return h.reshape(batch, seq, D_MODEL)
```

```python
import functools
import math

import jax
import jax.numpy as jnp
from jax import lax
from jax.experimental import pallas as pl
from jax.experimental.pallas import tpu as pltpu

D_MODEL = 2048
DEPTH = 2
HEAD_DIM = 128
N_FOX_HEADS = 8
N_DIL_HEADS = 8
DIL_PAIRS = ((128, 1), (512, 4), (2048, 16))
DIL_BLOCK = 128
D_FF = 5632
S5_GROUP = 16
S5_WIDTH = 1024
S5_GROUPS = S5_WIDTH // S5_GROUP
S5_STATE = 64
ALPHA = (2.0 * DEPTH) ** 0.25
LN_EPS = 1e-5
FOX_W = N_FOX_HEADS * HEAD_DIM
DIL_W = N_DIL_HEADS * HEAD_DIM

F32 = jnp.float32
BF16 = jnp.bfloat16
NEG = -1e30
LOG2E = math.log2(math.e)
FOX_AUG = 128
FOX_VROWS = HEAD_DIM + 16

V7X_LANES = 128
V7X_SUBLANES = 8
V7X_VMEM_LIMIT_BYTES = 56 * 1024 * 1024

FFN_ROWS = 512
FFN_COLS = 512
FFN_CHUNKS = D_FF // FFN_COLS
FFN_GROUP = 256
FFN_RING_SLOTS = 3
assert (FFN_CHUNKS + 1) % FFN_RING_SLOTS == 0
MM_ROWS = 2048
MM_COLS = 1024
FOX_BLOCK = 1024
DIL_TILE = 2048
OUT_ROWS = 512
GLU_ROWS = 512
CUMSUM_ROWS = 512
CAST_STEPS = 8
S5_ROWS = 256
S5_SEG = S5_ROWS // V7X_SUBLANES
S5_GB = 16
S5_NB = S5_GROUPS // S5_GB
S5_LANES = S5_GB * S5_STATE


def _params(sem):
    return pltpu.CompilerParams(dimension_semantics=sem, vmem_limit_bytes=V7X_VMEM_LIMIT_BYTES)


def _layer_norm(v, g, b):
    mu = jnp.mean(v, axis=-1, keepdims=True)
    d = v - mu
    var = jnp.mean(d * d, axis=-1, keepdims=True)
    return d * lax.rsqrt(var + LN_EPS) * g + b


def _dot(a, b):
    return jnp.dot(a, b, preferred_element_type=F32)


def _dot_nt(a, b):
    return lax.dot_general(a, b, (((1,), (1,)), ((), ())), preferred_element_type=F32)


def _ffn_kernel(*refs, cast_input):
    if cast_input:
        x_ref, wgu_hbm, wd_hbm, g_ref, b_ref, o_ref, ob_ref, h_ref, wgu_buf, wd_buf, sem, xb_ref = refs
        xb_ref[...] = x_ref[...].astype(BF16)
    else:
        x_ref, xb_ref, wgu_hbm, wd_hbm, g_ref, b_ref, o_ref, ob_ref, h_ref, wgu_buf, wd_buf, sem = refs
    i = pl.program_id(0)
    last_tile = pl.num_programs(0) - 1
    chunks = FFN_CHUNKS
    nsub = chunks + 1

    def ring(sub, slot, op):
        @pl.when(sub < chunks)
        def _():
            c = jnp.minimum(sub, chunks - 1)
            op(pltpu.make_async_copy(wgu_hbm.at[c], wgu_buf.at[slot], sem.at[0, slot]))

        @pl.when(sub >= 1)
        def _():
            c = jnp.maximum(sub - 1, 0)
            op(pltpu.make_async_copy(wd_hbm.at[c], wd_buf.at[slot], sem.at[1, slot]))

    start = lambda cp: cp.start()
    wait = lambda cp: cp.wait()

    @pl.when(i == 0)
    def _():
        ring(jnp.int32(0), 0, start)
        ring(jnp.int32(1), 1, start)

    def fetch_ahead_and_wait(j):
        ahead = j + 2
        wraps = ahead >= nsub

        @pl.when(jnp.logical_or(jnp.logical_not(wraps), i < last_tile))
        def _():
            ring(jnp.where(wraps, ahead - nsub, ahead), ahead % FFN_RING_SLOTS, start)

        ring(j, j % FFN_RING_SLOTS, wait)

    def hidden(slot):
        gu = _dot(xb_ref[...], wgu_buf[slot])
        parts = []
        for n in range(FFN_COLS // FFN_GROUP):
            gate = gu[:, 2 * n * FFN_GROUP:(2 * n + 1) * FFN_GROUP]
            up = gu[:, (2 * n + 1) * FFN_GROUP:(2 * n + 2) * FFN_GROUP]
            parts.append((gate * jax.nn.sigmoid(gate) * up).astype(BF16))
        return jnp.concatenate(parts, axis=1)

    def down(j, slot):
        return _dot(h_ref[(j - 1) % 2], wd_buf[slot])

    fetch_ahead_and_wait(jnp.int32(0))
    h_ref[0] = hidden(0)

    fetch_ahead_and_wait(jnp.int32(1))
    o_ref[...] = down(1, 1)
    h_ref[1] = hidden(1)

    def body(j, carry):
        fetch_ahead_and_wait(j)
        slot = j % FFN_RING_SLOTS
        o_ref[...] += down(j, slot)
        h_ref[j % 2] = hidden(slot)
        return carry

    lax.fori_loop(2, chunks, body, 0)

    fetch_ahead_and_wait(jnp.int32(chunks))
    acc = o_ref[...] + down(chunks, chunks % FFN_RING_SLOTS)
    y = _layer_norm(ALPHA * x_ref[...] + 0.5 * acc, g_ref[...], b_ref[...])
    o_ref[...] = y
    ob_ref[...] = y.astype(BF16)


def _ffn(x, xb, wgu, wd, g, b):
    s = x.shape[0]
    row = pl.BlockSpec((FFN_ROWS, D_MODEL), lambda i: (i, 0))
    hbm = pl.BlockSpec(memory_space=pl.ANY)
    vec = pl.BlockSpec((1, D_MODEL), lambda i: (0, 0))
    scratch = [
        pltpu.VMEM((2, FFN_ROWS, FFN_COLS), BF16),
        pltpu.VMEM((FFN_RING_SLOTS, D_MODEL, 2 * FFN_COLS), BF16),
        pltpu.VMEM((FFN_RING_SLOTS, FFN_COLS, D_MODEL), BF16),
        pltpu.SemaphoreType.DMA((2, FFN_RING_SLOTS)),
    ]
    cast_input = xb is None
    return pl.pallas_call(
        functools.partial(_ffn_kernel, cast_input=cast_input),
        grid=(s // FFN_ROWS,),
        in_specs=[row] + ([] if cast_input else [row]) + [hbm, hbm, vec, vec],
        out_specs=[row, row],
        out_shape=[jax.ShapeDtypeStruct((s, D_MODEL), F32), jax.ShapeDtypeStruct((s, D_MODEL), BF16)],
        scratch_shapes=scratch + ([pltpu.VMEM((FFN_ROWS, D_MODEL), BF16)] if cast_input else []),
        compiler_params=_params(("arbitrary",)),
        name="ffn",
    )(*([x] if cast_input else [x, xb]), wgu, wd, g, b)


def _cast_kernel(wg_ref, wu_ref, wd_ref, ogu_ref, od_ref):
    for n in range(D_FF // FFN_GROUP):
        c, k = divmod(n, FFN_COLS // FFN_GROUP)
        src = slice(n * FFN_GROUP, (n + 1) * FFN_GROUP)
        ogu_ref[c, :, 2 * k * FFN_GROUP:(2 * k + 1) * FFN_GROUP] = wg_ref[:, src].astype(BF16)
        ogu_ref[c, :, (2 * k + 1) * FFN_GROUP:(2 * k + 2) * FFN_GROUP] = wu_ref[:, src].astype(BF16)
    od_ref[...] = wd_ref[...].astype(BF16)


def _ffn_weights_bf16(w_gate, w_up, w_down, layer):
    steps = CAST_STEPS
    up_rows, down_rows = D_MODEL // steps, D_FF // steps
    up_in = pl.BlockSpec((None, up_rows, D_FF), lambda r: (layer, r, 0))
    up_out = pl.BlockSpec((FFN_CHUNKS, up_rows, 2 * FFN_COLS), lambda r: (0, r, 0))
    up_shape = jax.ShapeDtypeStruct((FFN_CHUNKS, D_MODEL, 2 * FFN_COLS), BF16)
    wgu, wd = pl.pallas_call(
        _cast_kernel,
        grid=(steps,),
        in_specs=[up_in, up_in, pl.BlockSpec((None, down_rows, D_MODEL), lambda r: (layer, r, 0))],
        out_specs=[up_out, pl.BlockSpec((down_rows, D_MODEL), lambda r: (r, 0))],
        out_shape=[up_shape, jax.ShapeDtypeStruct((D_FF, D_MODEL), BF16)],
        compiler_params=_params(("parallel",)),
        name="ffn_weights_bf16",
    )(w_gate, w_up, w_down)
    return wgu, wd.reshape(FFN_CHUNKS, FFN_COLS, D_MODEL)


def _attn_in_kernel(x_ref, w_ref, wf_ref, o_ref, f_ref):
    x = x_ref[...]
    o_ref[...] = _dot(x, w_ref[...]).astype(o_ref.dtype)

    @pl.when(pl.program_id(1) == 0)
    def _():
        f_ref[...] = _dot(x, wf_ref[...])


def _attn_in(xb, w, wf):
    s, n = xb.shape[0], w.shape[1]
    return pl.pallas_call(
        _attn_in_kernel,
        grid=(s // MM_ROWS, n // MM_COLS),
        in_specs=[
            pl.BlockSpec((MM_ROWS, D_MODEL), lambda i, j: (i, 0)),
            pl.BlockSpec((D_MODEL, MM_COLS), lambda i, j: (0, j)),
            pl.BlockSpec((D_MODEL, V7X_LANES), lambda i, j: (0, 0)),
        ],
        out_specs=[
            pl.BlockSpec((MM_ROWS, MM_COLS), lambda i, j: (i, j)),
            pl.BlockSpec((MM_ROWS, V7X_LANES), lambda i, j: (i, 0)),
        ],
        out_shape=[jax.ShapeDtypeStruct((s, n), BF16), jax.ShapeDtypeStruct((s, V7X_LANES), F32)],
        compiler_params=_params(("parallel", "arbitrary")),
        name="attn_in",
    )(xb, w, wf)


def _mm_kernel(x_ref, w_ref, o_ref):
    o_ref[...] = _dot(x_ref[...], w_ref[...]).astype(o_ref.dtype)


def _matmul(xb, w, out_dtype):
    s, n = xb.shape[0], w.shape[1]
    return pl.pallas_call(
        _mm_kernel,
        grid=(s // MM_ROWS, n // MM_COLS),
        in_specs=[
            pl.BlockSpec((MM_ROWS, xb.shape[1]), lambda i, j: (i, 0)),
            pl.BlockSpec((xb.shape[1], MM_COLS), lambda i, j: (0, j)),
        ],
        out_specs=pl.BlockSpec((MM_ROWS, MM_COLS), lambda i, j: (i, j)),
        out_shape=jax.ShapeDtypeStruct((s, n), out_dtype),
        compiler_params=_params(("parallel", "parallel")),
        name="matmul",
    )(xb, w)


def _split3(v):
    hi = v.astype(BF16)
    r1 = v - hi.astype(F32)
    mid = r1.astype(BF16)
    lo = (r1 - mid.astype(F32)).astype(BF16)
    return hi, mid, lo


def _cumsum_kernel(f_ref, bf_ref, c_ref, carry_ref):
    @pl.when(pl.program_id(0) == 0)
    def _():
        carry_ref[...] = jnp.zeros_like(carry_ref)

    z = f_ref[...] + bf_ref[...]
    log_f = jnp.minimum(z, 0.0) - jnp.log1p(jnp.exp(-jnp.abs(z)))
    n = z.shape[0]
    tri = (lax.broadcasted_iota(jnp.int32, (n, n), 0) >= lax.broadcasted_iota(jnp.int32, (n, n), 1)).astype(BF16)
    hi, mid, lo = _split3(log_f)
    c = _dot(tri, hi) + _dot(tri, mid) + _dot(tri, lo) + carry_ref[0:1, :]
    c_ref[...] = c
    carry_ref[...] = jnp.broadcast_to(c[n - 1:n, :], carry_ref.shape)


def _forget_cumsum(f_logit, b_f_row):
    s = f_logit.shape[0]
    return pl.pallas_call(
        _cumsum_kernel,
        grid=(s // CUMSUM_ROWS,),
        in_specs=[
            pl.BlockSpec((CUMSUM_ROWS, V7X_LANES), lambda i: (i, 0)),
            pl.BlockSpec((1, V7X_LANES), lambda i: (0, 0)),
        ],
        out_specs=pl.BlockSpec((CUMSUM_ROWS, V7X_LANES), lambda i: (i, 0)),
        out_shape=jax.ShapeDtypeStruct((s, V7X_LANES), F32),
        scratch_shapes=[pltpu.VMEM((V7X_SUBLANES, V7X_LANES), F32)],
        compiler_params=_params(("arbitrary",)),
        name="forget_cumsum",
    )(f_logit, b_f_row)


def _fox_prep_kernel(qkv_ref, c_ref, qt_ref, ka_ref, vt_ref):
    rows = qkv_ref.shape[0]
    lane = lax.broadcasted_iota(jnp.int32, (rows, V7X_LANES), 1)
    bias = c_ref[...] * (-LOG2E)
    ones_rows = (lax.broadcasted_iota(jnp.int32, (FOX_AUG, rows), 0) < 3).astype(BF16)
    sum_rows = (lax.broadcasted_iota(jnp.int32, (FOX_VROWS - HEAD_DIM, rows), 0) < 1).astype(BF16)
    for h in range(N_FOX_HEADS):
        hs = slice(h * HEAD_DIM, (h + 1) * HEAD_DIM)
        a0 = h * (HEAD_DIM + FOX_AUG)
        qt_ref[a0:a0 + HEAD_DIM, :] = qkv_ref[:, hs].astype(F32).T.astype(BF16)
        qt_ref[a0 + HEAD_DIM:a0 + HEAD_DIM + FOX_AUG, :] = ones_rows
        v0 = h * FOX_VROWS
        vt_ref[v0:v0 + HEAD_DIM, :] = (
            qkv_ref[:, 2 * FOX_W + h * HEAD_DIM:2 * FOX_W + (h + 1) * HEAD_DIM].astype(F32).T.astype(BF16))
        vt_ref[v0 + HEAD_DIM:v0 + FOX_VROWS, :] = sum_rows
        hi, mid, lo = _split3(jnp.broadcast_to(bias[:, h:h + 1], (rows, V7X_LANES)))
        aug = jnp.where(lane == 0, hi.astype(F32), jnp.where(lane == 1, mid.astype(F32),
                                                              jnp.where(lane == 2, lo.astype(F32), 0.0)))
        ka_ref[:, a0:a0 + HEAD_DIM] = qkv_ref[:, FOX_W + h * HEAD_DIM:FOX_W + (h + 1) * HEAD_DIM]
        ka_ref[:, a0 + HEAD_DIM:a0 + HEAD_DIM + FOX_AUG] = aug.astype(BF16)


def _fox_prep(qkv, c_rows):
    s = qkv.shape[0]
    wide = N_FOX_HEADS * (HEAD_DIM + FOX_AUG)
    vrows = N_FOX_HEADS * FOX_VROWS
    return pl.pallas_call(
        _fox_prep_kernel,
        grid=(s // FOX_BLOCK,),
        in_specs=[pl.BlockSpec((FOX_BLOCK, 3 * FOX_W), lambda i: (i, 0)),
                  pl.BlockSpec((FOX_BLOCK, V7X_LANES), lambda i: (i, 0))],
        out_specs=[pl.BlockSpec((wide, FOX_BLOCK), lambda i: (0, i)),
                   pl.BlockSpec((FOX_BLOCK, wide), lambda i: (i, 0)),
                   pl.BlockSpec((vrows, FOX_BLOCK), lambda i: (0, i))],
        out_shape=[jax.ShapeDtypeStruct((wide, s), BF16), jax.ShapeDtypeStruct((s, wide), BF16),
                   jax.ShapeDtypeStruct((vrows, s), BF16)],
        compiler_params=_params(("parallel",)),
        name="fox_prep",
    )(qkv, c_rows)


def _fox_kernel(qi_ref, kj_ref, qt_ref, ka_ref, vt_ref, o_ref, m_ref, acc_ref, s_ref):
    t = pl.program_id(0)
    i = qi_ref[t]
    j = kj_ref[t]
    blk = ka_ref.shape[0]
    aw = HEAD_DIM + FOX_AUG

    @pl.when(j == 0)
    def _():
        m_ref[...] = jnp.full_like(m_ref, NEG)
        acc_ref[...] = jnp.zeros_like(acc_ref)

    def scores(h):
        return _dot(ka_ref[:, h * aw:(h + 1) * aw], qt_ref[h * aw:(h + 1) * aw, :])

    def step(diagonal):
        if diagonal:
            keep = lax.broadcasted_iota(jnp.int32, (blk, blk), 0) <= lax.broadcasted_iota(jnp.int32, (blk, blk), 1)
        s_ref[0] = scores(0)
        for h in range(N_FOX_HEADS):
            if h + 1 < N_FOX_HEADS:
                s_ref[(h + 1) % 2] = scores(h + 1)
            s = s_ref[h % 2]
            if diagonal:
                s = jnp.where(keep, s, NEG)
            m_prev = m_ref[h:h + 1, :]
            m_new = jnp.maximum(m_prev, jnp.max(s, axis=0, keepdims=True))
            p = jnp.exp2(s - m_new).astype(BF16)
            alpha = jnp.exp2(m_prev - m_new)
            m_ref[h:h + 1, :] = m_new
            vs = slice(h * FOX_VROWS, (h + 1) * FOX_VROWS)
            acc_ref[vs, :] = alpha * acc_ref[vs, :] + _dot(vt_ref[vs, :], p)

    @pl.when(j < i)
    def _():
        step(False)

    @pl.when(j == i)
    def _():
        step(True)
        for h in range(N_FOX_HEADS):
            v0 = h * FOX_VROWS
            out_t = acc_ref[v0:v0 + HEAD_DIM, :] / acc_ref[v0 + HEAD_DIM:v0 + HEAD_DIM + 1, :]
            o_ref[:, h * HEAD_DIM:(h + 1) * HEAD_DIM] = out_t.T.astype(o_ref.dtype)


def _fox_attention(qt, ka, vt):
    s = ka.shape[0]
    nb = s // FOX_BLOCK
    qi = jnp.asarray([i for i in range(nb) for _ in range(i + 1)], jnp.int32)
    kj = jnp.asarray([j for i in range(nb) for j in range(i + 1)], jnp.int32)
    wide = ka.shape[1]
    vrows = vt.shape[0]
    grid_spec = pltpu.PrefetchScalarGridSpec(
        num_scalar_prefetch=2,
        grid=(qi.shape[0],),
        in_specs=[
            pl.BlockSpec((wide, FOX_BLOCK), lambda t, qi, kj: (0, qi[t])),
            pl.BlockSpec((FOX_BLOCK, wide), lambda t, qi, kj: (kj[t], 0)),
            pl.BlockSpec((vrows, FOX_BLOCK), lambda t, qi, kj: (0, kj[t])),
        ],
        out_specs=pl.BlockSpec((FOX_BLOCK, FOX_W), lambda t, qi, kj: (qi[t], 0)),
        scratch_shapes=[
            pltpu.VMEM((N_FOX_HEADS, FOX_BLOCK), F32),
            pltpu.VMEM((vrows, FOX_BLOCK), F32),
            pltpu.VMEM((2, FOX_BLOCK, FOX_BLOCK), F32),
        ],
    )
    return pl.pallas_call(
        _fox_kernel,
        grid_spec=grid_spec,
        out_shape=jax.ShapeDtypeStruct((s, FOX_W), BF16),
        compiler_params=_params(("arbitrary",)),
        name="fox_attention",
    )(qi, kj, qt, ka, vt)


def _dil_kernel(q_ref, k_ref, v_ref, kp_ref, vp_ref, y_ref, o_scr, l_scr):
    has_prev_tile = pl.program_id(0) > 0
    b = DIL_BLOCK
    tile = q_ref.shape[0]
    jr = lax.broadcasted_iota(jnp.int32, (b, 2 * b), 0)
    mc = lax.broadcasted_iota(jnp.int32, (b, 2 * b), 1)
    band = (mc >= jr) & (mc <= jr + b)
    band_first = band & ((mc >= b) | has_prev_tile)

    def rows(ref, start, d):
        return ref[pl.ds(start, b), :] if d == 1 else ref[pl.ds(start, b, stride=d), :]

    for g, (_, d) in enumerate(DIL_PAIRS):
        for r in range(d):
            first = tile - b * d + r
            k_blk = [rows(kp_ref, first, d).astype(BF16)]
            v_blk = [rows(vp_ref, first, d).astype(BF16)]
            for n in range(tile // (b * d)):
                cur = r + n * b * d
                k_blk.append(rows(k_ref, cur, d).astype(BF16))
                v_blk.append(rows(v_ref, cur, d).astype(BF16))
                k2 = jnp.concatenate(k_blk[n:n + 2], axis=0)
                v2 = jnp.concatenate(v_blk[n:n + 2], axis=0)
                keep = band if n > 0 else band_first
                s = jnp.where(keep, _dot_nt(rows(q_ref, cur, d).astype(BF16), k2), NEG)
                m = jnp.max(s, axis=1, keepdims=True)
                p = jnp.exp(s - m)
                l = jnp.sum(p, axis=1, keepdims=True)
                o = _dot(p.astype(BF16), v2) / l
                lse = jnp.broadcast_to(m + jnp.log(l), (b, HEAD_DIM))
                if d == 1:
                    o_scr[g, pl.ds(cur, b), :] = o
                    l_scr[g, pl.ds(cur, b), :] = lse
                else:
                    o_scr[g, pl.ds(cur, b, stride=d), :] = o
                    l_scr[g, pl.ds(cur, b, stride=d), :] = lse

    l1, l2, l3 = l_scr[0], l_scr[1], l_scr[2]
    mx = jnp.maximum(jnp.maximum(l1, l2), l3)
    e1, e2, e3 = jnp.exp(l1 - mx), jnp.exp(l2 - mx), jnp.exp(l3 - mx)
    den = e1 + e2 + e3
    y_ref[...] = ((e1 / den) * o_scr[0] + (e2 / den) * o_scr[1] + (e3 / den) * o_scr[2]).astype(y_ref.dtype)


def _dilated_attention(qkv):
    s = qkv.shape[0]
    cur = lambda c: pl.BlockSpec((DIL_TILE, HEAD_DIM), lambda t, h: (t, c * N_DIL_HEADS + h))
    prev = lambda c: pl.BlockSpec((DIL_TILE, HEAD_DIM), lambda t, h: (jnp.maximum(t - 1, 0), c * N_DIL_HEADS + h))
    return pl.pallas_call(
        _dil_kernel,
        grid=(s // DIL_TILE, N_DIL_HEADS),
        in_specs=[cur(0), cur(1), cur(2), prev(1), prev(2)],
        out_specs=pl.BlockSpec((DIL_TILE, HEAD_DIM), lambda t, h: (t, h)),
        out_shape=jax.ShapeDtypeStruct((s, DIL_W), BF16),
        scratch_shapes=[pltpu.VMEM((len(DIL_PAIRS), DIL_TILE, HEAD_DIM), F32),
                        pltpu.VMEM((len(DIL_PAIRS), DIL_TILE, HEAD_DIM), F32)],
        compiler_params=_params(("parallel", "parallel")),
        name="dilated_attention",
    )(qkv, qkv, qkv, qkv, qkv)


def _attn_out_kernel(x_ref, ya_ref, yb_ref, w_ref, g_ref, b_ref, o_ref, ob_ref):
    half = x_ref.shape[0] // 2
    for r in range(2):
        rows = slice(r * half, (r + 1) * half)
        y = _dot(jnp.concatenate([ya_ref[rows, :], yb_ref[rows, :]], axis=1), w_ref[...])
        out = _layer_norm(ALPHA * x_ref[rows, :] + y, g_ref[...], b_ref[...])
        o_ref[rows, :] = out
        ob_ref[rows, :] = out.astype(BF16)


def _attn_out(x, ya, yb, w, g, b):
    s = x.shape[0]
    row = lambda width: pl.BlockSpec((OUT_ROWS, width), lambda i: (i, 0))
    const = lambda shape: pl.BlockSpec(shape, lambda i: (0, 0))
    return pl.pallas_call(
        _attn_out_kernel,
        grid=(s // OUT_ROWS,),
        in_specs=[row(D_MODEL), row(FOX_W), row(DIL_W),
                  const((FOX_W + DIL_W, D_MODEL)), const((1, D_MODEL)), const((1, D_MODEL))],
        out_specs=[row(D_MODEL), row(D_MODEL)],
        out_shape=[jax.ShapeDtypeStruct((s, D_MODEL), F32), jax.ShapeDtypeStruct((s, D_MODEL), BF16)],
        compiler_params=_params(("parallel",)),
        name="attn_out",
    )(x, ya, yb, w, g, b)


def _cmul(ar, ai, br, bi):
    return ar * br - ai * bi, ar * bi + ai * br


def _s5_prep_kernel(lrg_ref, lig_ref, ldtg_ref, lr_ref, li_ref, ldt_ref, br_ref, bi_ref,
                    pr_ref, pi_ref, qr_ref, qi_ref, bbr_ref, bbi_ref):
    def a_bar(lr, li, ldt):
        dt = jnp.exp(ldt)
        mag = jnp.exp(lr * dt)
        return mag * jnp.cos(li * dt), mag * jnp.sin(li * dt)

    lr, li = lr_ref[...], li_ref[...]
    ar, ai = a_bar(lr, li, ldt_ref[...])
    den = lr * lr + li * li
    cr = ((ar - 1.0) * lr + ai * li) / den
    ci = (ai * lr - (ar - 1.0) * li) / den
    bbr_ref[...], bbi_ref[...] = _cmul(cr, ci, br_ref[...], bi_ref[...])

    ar, ai = a_bar(lrg_ref[...], lig_ref[...], ldtg_ref[...])
    pr, pi = ar, ai
    pr_ref[0], pi_ref[0] = pr, pi
    for k in range(1, S5_SEG):
        pr, pi = _cmul(pr, pi, ar, ai)
        pr_ref[k], pi_ref[k] = pr, pi
    sr, si = pr, pi
    qr_ref[0], qi_ref[0] = sr, si
    for k in range(1, V7X_SUBLANES):
        pr, pi = _cmul(pr, pi, sr, si)
        qr_ref[k], qi_ref[k] = pr, pi


def _s5_prep(lam_re, lam_im, log_dt, b_re, b_im):
    ldt = jnp.broadcast_to(log_dt[:, None], lam_re.shape)
    rep = lambda v: jnp.repeat(v, S5_GROUP, axis=0)
    rows = S5_GROUPS * S5_GROUP
    bt = lambda v: v.transpose(0, 2, 1).reshape(rows, S5_STATE)
    mat = jax.ShapeDtypeStruct((rows, S5_STATE), F32)
    pw = jax.ShapeDtypeStruct((S5_SEG, S5_GROUPS, S5_STATE), F32)
    qw = jax.ShapeDtypeStruct((V7X_SUBLANES, S5_GROUPS, S5_STATE), F32)
    return pl.pallas_call(_s5_prep_kernel, out_shape=[pw, pw, qw, qw, mat, mat], name="s5_prep")(
        lam_re, lam_im, ldt, rep(lam_re), rep(lam_im), rep(ldt), bt(b_re), bt(b_im))


def _s5_core_kernel(u_ref, bblk_ref, cblk_ref, pr_ref, pi_ref, qr_ref, qi_ref, d_ref, z_ref,
                    up_ref, sr_ref, si_ref, cr_ref, ci_ref):
    @pl.when(pl.program_id(0) == 0)
    def _():
        cr_ref[...] = jnp.zeros_like(cr_ref)
        ci_ref[...] = jnp.zeros_like(ci_ref)

    sl = V7X_SUBLANES
    rows = u_ref.shape[0]
    ri = lax.broadcasted_iota(jnp.int32, (rows, rows), 0)
    ci = lax.broadcasted_iota(jnp.int32, (rows, rows), 1)
    perm = ((ri // sl == ci % S5_SEG) & (ri % sl == ci // S5_SEG)).astype(BF16)
    unperm = ((ci // sl == ri % S5_SEG) & (ci % sl == ri // S5_SEG)).astype(BF16)
    hi, mid, lo = _split3(u_ref[...])
    up_ref[...] = _dot(perm, hi) + _dot(perm, mid) + _dot(perm, lo)

    sub = lax.broadcasted_iota(jnp.int32, (sl, S5_LANES), 0)
    width = S5_GB * S5_GROUP
    ys = []
    for jb in range(S5_NB):
        bu = _dot(up_ref[:, jb * width:(jb + 1) * width].astype(BF16), bblk_ref[jb])
        a_re, a_im = pr_ref[jb, 0:1, :], pi_ref[jb, 0:1, :]
        xr, xi = bu[0:sl, :S5_LANES], bu[0:sl, S5_LANES:]
        sr_ref[0:sl, :], si_ref[0:sl, :] = xr, xi
        for j in range(1, S5_SEG):
            tr, ti = _cmul(a_re, a_im, xr, xi)
            xr, xi = tr + bu[j * sl:(j + 1) * sl, :S5_LANES], ti + bu[j * sl:(j + 1) * sl, S5_LANES:]
            sr_ref[j * sl:(j + 1) * sl, :], si_ref[j * sl:(j + 1) * sl, :] = xr, xi

        q_re, q_im = qr_ref[jb], qi_ref[jb]
        er, ei = xr, xi
        for k in (1, 2, 4):
            tr = jnp.where(sub >= k, pltpu.roll(er, k, 0), 0.0)
            ti = jnp.where(sub >= k, pltpu.roll(ei, k, 0), 0.0)
            mr, mi = _cmul(q_re[k - 1:k, :], q_im[k - 1:k, :], tr, ti)
            er, ei = er + mr, ei + mi
        in_re, in_im = cr_ref[jb:jb + 1, :], ci_ref[jb:jb + 1, :]
        mr, mi = _cmul(q_re, q_im, in_re, in_im)
        er, ei = er + mr, ei + mi
        cr_ref[jb:jb + 1, :], ci_ref[jb:jb + 1, :] = er[sl - 1:, :], ei[sl - 1:, :]
        cin_re = jnp.where(sub >= 1, pltpu.roll(er, 1, 0), in_re)
        cin_im = jnp.where(sub >= 1, pltpu.roll(ei, 1, 0), in_im)
        fr, fi = _cmul(pr_ref[jb][:, None, :], pi_ref[jb][:, None, :], cin_re[None], cin_im[None])
        shape3 = (S5_SEG, sl, S5_LANES)
        state = jnp.concatenate([(sr_ref[...].reshape(shape3) + fr).reshape(-1, S5_LANES),
                                 (si_ref[...].reshape(shape3) + fi).reshape(-1, S5_LANES)], axis=1)
        ys.append(_dot(state.astype(BF16), cblk_ref[jb]))
    y = jnp.concatenate(ys, axis=1) + d_ref[...] * up_ref[...]
    z = 0.5 * y * (1.0 + jnp.tanh(math.sqrt(2.0 / math.pi) * (y + 0.044715 * (y * y * y))))
    z_ref[...] = _dot(unperm, z.astype(BF16)).astype(z_ref.dtype)


def _s5_core(u, bblk, cblk, p_re, p_im, q_re, q_im, d_row):
    s = u.shape[0]
    const = lambda shape: pl.BlockSpec(shape, lambda t: (0,) * len(shape))
    return pl.pallas_call(
        _s5_core_kernel,
        grid=(s // S5_ROWS,),
        in_specs=[pl.BlockSpec((S5_ROWS, S5_WIDTH), lambda t: (t, 0)),
                  const(bblk.shape), const(cblk.shape), const(p_re.shape), const(p_im.shape),
                  const(q_re.shape), const(q_im.shape), const((1, S5_WIDTH))],
        out_specs=pl.BlockSpec((S5_ROWS, S5_WIDTH), lambda t: (t, 0)),
        out_shape=jax.ShapeDtypeStruct((s, S5_WIDTH), BF16),
        scratch_shapes=[pltpu.VMEM((S5_ROWS, S5_WIDTH), F32),
                        pltpu.VMEM((S5_ROWS, S5_LANES), F32), pltpu.VMEM((S5_ROWS, S5_LANES), F32),
                        pltpu.VMEM((V7X_SUBLANES, S5_LANES), F32), pltpu.VMEM((V7X_SUBLANES, S5_LANES), F32)],
        compiler_params=_params(("arbitrary",)),
        name="s5_core",
    )(u, bblk, cblk, p_re, p_im, q_re, q_im, d_row)


def _block_diag(blocks):
    nb, gb, r, c = blocks.shape
    eye = jnp.eye(gb, dtype=blocks.dtype)
    return jnp.einsum("ngrc,gk->ngrkc", blocks, eye).reshape(nb, gb * r, gb * c)


def _glu_kernel(x_ref, z_ref, wo_ref, wg_ref, g_ref, b_ref, o_ref, ob_ref):
    half = z_ref.shape[0] // 2
    for r in range(2):
        rows = slice(r * half, (r + 1) * half)
        z = z_ref[rows, :]
        y = _dot(z, wo_ref[...]) * jax.nn.sigmoid(_dot(z, wg_ref[...]))
        out = _layer_norm(ALPHA * x_ref[rows, :] + y, g_ref[...], b_ref[...])
        o_ref[rows, :] = out
        ob_ref[rows, :] = out.astype(BF16)


def _glu(x, z, wo, wg, g, b):
    s = x.shape[0]
    row = lambda width: pl.BlockSpec((GLU_ROWS, width), lambda i: (i, 0))
    const = lambda shape: pl.BlockSpec(shape, lambda i: (0, 0))
    return pl.pallas_call(
        _glu_kernel,
        grid=(s // GLU_ROWS,),
        in_specs=[row(D_MODEL), row(S5_WIDTH), const((S5_WIDTH, D_MODEL)), const((S5_WIDTH, D_MODEL)),
                  const((1, D_MODEL)), const((1, D_MODEL))],
        out_specs=[row(D_MODEL), row(D_MODEL)],
        out_shape=[jax.ShapeDtypeStruct((s, D_MODEL), F32), jax.ShapeDtypeStruct((s, D_MODEL), BF16)],
        compiler_params=_params(("parallel",)),
        name="glu",
    )(x, z, wo, wg, g, b)


def _attention_mixer(x, xb, w_in, b_f, w_out, g, b):
    scale = HEAD_DIM ** -0.5
    fox_q, fox_k, fox_v = (w_in[:, n * FOX_W:(n + 1) * FOX_W] for n in range(3))
    w_f = w_in[:, 3 * FOX_W:3 * FOX_W + N_FOX_HEADS]
    d0 = 3 * FOX_W + N_FOX_HEADS
    dil_q, dil_k, dil_v = (w_in[:, d0 + n * DIL_W:d0 + (n + 1) * DIL_W] for n in range(3))
    w_fox = jnp.concatenate([fox_q * (scale * LOG2E), fox_k, fox_v], axis=1).astype(BF16)
    w_dil = jnp.concatenate([dil_q * scale, dil_k, dil_v], axis=1).astype(BF16)
    w_f = jnp.pad(w_f, ((0, 0), (0, V7X_LANES - N_FOX_HEADS))).astype(BF16)
    qkv_fox, f_logit = _attn_in(xb, w_fox, w_f)
    c_rows = _forget_cumsum(f_logit, jnp.pad(b_f, (0, V7X_LANES - N_FOX_HEADS))[None, :])
    ya = _fox_attention(*_fox_prep(qkv_fox, c_rows))
    yb = _dilated_attention(_matmul(xb, w_dil, F32))
    return _attn_out(x, ya, yb, w_out.astype(BF16), g, b)


def _s5_mixer(x, xb, w_in, lam_re, lam_im, log_dt, b_re, b_im, c_re, c_im, d_skip, w_glu_out, w_glu_gate, g, b):
    u = _matmul(xb, w_in.astype(BF16), F32)
    p_re, p_im, q_re, q_im, bb_re, bb_im = _s5_prep(lam_re, lam_im, log_dt, b_re, b_im)
    lanes = lambda p: p.reshape(p.shape[0], S5_NB, S5_LANES).transpose(1, 0, 2)
    blk = lambda v: v.reshape(S5_NB, S5_GB, S5_GROUP, S5_STATE)
    bblk = jnp.concatenate([_block_diag(blk(bb_re)), _block_diag(blk(bb_im))], axis=2).astype(BF16)
    ct = lambda v: v.transpose(0, 2, 1).reshape(S5_NB, S5_GB, S5_STATE, S5_GROUP)
    cblk = jnp.concatenate([_block_diag(ct(c_re)), -_block_diag(ct(c_im))], axis=1).astype(BF16)
    z = _s5_core(u, bblk, cblk, lanes(p_re), lanes(p_im), lanes(q_re), lanes(q_im), d_skip[None, :])
    return _glu(x, z, w_glu_out.astype(BF16), w_glu_gate.astype(BF16), g, b)


def kernel(x, ffn1_w_gate, ffn1_w_up, ffn1_w_down, ffn2_w_gate, ffn2_w_up, ffn2_w_down, ln_gain, ln_bias,
           attn_w_in, attn_b_f, attn_w_out, s5_w_in, s5_lambda_re, s5_lambda_im, s5_log_dt, s5_b_re, s5_b_im,
           s5_c_re, s5_c_im, s5_d, s5_w_glu_out, s5_w_glu_gate):
    batch, seq, _ = x.shape
    assert batch == 1
    h = x.reshape(seq, D_MODEL)
    hb = None
    ln = lambda i, k: (ln_gain[i, k][None, :], ln_bias[i, k][None, :])
    for i in range(DEPTH):
        h, hb = _ffn(h, hb, *_ffn_weights_bf16(ffn1_w_gate, ffn1_w_up, ffn1_w_down, i), *ln(i, 0))
        j = i // 2
        if i % 2 == 0:
            h, hb = _attention_mixer(h, hb, attn_w_in[j], attn_b_f[j], attn_w_out[j], *ln(i, 1))
        else:
            h, hb = _s5_mixer(h, hb, s5_w_in[j], s5_lambda_re[j], s5_lambda_im[j], s5_log_dt[j], s5_b_re[j],
                              s5_b_im[j], s5_c_re[j], s5_c_im[j], s5_d[j], s5_w_glu_out[j], s5_w_glu_gate[j],
                              *ln(i, 1))
        h, hb = _ffn(h, hb, *_ffn_weights_bf16(ffn2_w_gate, ffn2_w_up, ffn2_w_down, i), *ln(i, 2))
    return h.reshape(batch, seq, D_MODEL)
```

```python
import functools
import math

import jax
import jax.numpy as jnp
from jax import lax
from jax.experimental import pallas as pl
from jax.experimental.pallas import tpu as pltpu

D_MODEL = 2048
DEPTH = 2
HEAD_DIM = 128
N_FOX_HEADS = 8
N_DIL_HEADS = 8
DIL_PAIRS = ((128, 1), (512, 4), (2048, 16))
DIL_BLOCK = 128
D_FF = 5632
S5_GROUP = 16
S5_WIDTH = 1024
S5_GROUPS = S5_WIDTH // S5_GROUP
S5_STATE = 64
ALPHA = (2.0 * DEPTH) ** 0.25
LN_EPS = 1e-5
FOX_W = N_FOX_HEADS * HEAD_DIM
DIL_W = N_DIL_HEADS * HEAD_DIM

F32 = jnp.float32
BF16 = jnp.bfloat16
NEG = -1e30
LOG2E = math.log2(math.e)
FOX_AUG = 128
FOX_VROWS = HEAD_DIM + 16

V7X_LANES = 128
V7X_SUBLANES = 8
V7X_VMEM_LIMIT_BYTES = 56 * 1024 * 1024

FFN_ROWS = 512
FFN_COLS = 512
FFN_CHUNKS = D_FF // FFN_COLS
FFN_GROUP = 256
FFN_RING_SLOTS = 3
assert (FFN_CHUNKS + 1) % FFN_RING_SLOTS == 0
MM_ROWS = 2048
MM_COLS = 1024
FOX_BLOCK = 1024
DIL_TILE = 2048
OUT_ROWS = 512
GLU_ROWS = 512
CUMSUM_ROWS = 512
CAST_STEPS = 8
S5_ROWS = 256
S5_SEG = S5_ROWS // V7X_SUBLANES
S5_GB = 16
S5_NB = S5_GROUPS // S5_GB
S5_LANES = S5_GB * S5_STATE


def _params(sem):
    return pltpu.CompilerParams(dimension_semantics=sem, vmem_limit_bytes=V7X_VMEM_LIMIT_BYTES)


def _layer_norm(v, g, b):
    mu = jnp.mean(v, axis=-1, keepdims=True)
    d = v - mu
    var = jnp.mean(d * d, axis=-1, keepdims=True)
    return d * lax.rsqrt(var + LN_EPS) * g + b


def _dot(a, b):
    return jnp.dot(a, b, preferred_element_type=F32)


def _dot_nt(a, b):
    return lax.dot_general(a, b, (((1,), (1,)), ((), ())), preferred_element_type=F32)


def _ffn_kernel(*refs, cast_input):
    if cast_input:
        x_ref, wgu_hbm, wd_hbm, g_ref, b_ref, o_ref, ob_ref, h_ref, wgu_buf, wd_buf, sem, xb_ref = refs
        xb_ref[...] = x_ref[...].astype(BF16)
    else:
        x_ref, xb_ref, wgu_hbm, wd_hbm, g_ref, b_ref, o_ref, ob_ref, h_ref, wgu_buf, wd_buf, sem = refs
    i = pl.program_id(0)
    last_tile = pl.num_programs(0) - 1
    chunks = FFN_CHUNKS
    nsub = chunks + 1

    def ring(sub, slot, op):
        @pl.when(sub < chunks)
        def _():
            c = jnp.minimum(sub, chunks - 1)
            op(pltpu.make_async_copy(wgu_hbm.at[c], wgu_buf.at[slot], sem.at[0, slot]))

        @pl.when(sub >= 1)
        def _():
            c = jnp.maximum(sub - 1, 0)
            op(pltpu.make_async_copy(wd_hbm.at[c], wd_buf.at[slot], sem.at[1, slot]))

    start = lambda cp: cp.start()
    wait = lambda cp: cp.wait()

    @pl.when(i == 0)
    def _():
        ring(jnp.int32(0), 0, start)
        ring(jnp.int32(1), 1, start)

    def fetch_ahead_and_wait(j):
        ahead = j + 2
        wraps = ahead >= nsub

        @pl.when(jnp.logical_or(jnp.logical_not(wraps), i < last_tile))
        def _():
            ring(jnp.where(wraps, ahead - nsub, ahead), ahead % FFN_RING_SLOTS, start)

        ring(j, j % FFN_RING_SLOTS, wait)

    def hidden(slot):
        gu = _dot(xb_ref[...], wgu_buf[slot])
        parts = []
        for n in range(FFN_COLS // FFN_GROUP):
            gate = gu[:, 2 * n * FFN_GROUP:(2 * n + 1) * FFN_GROUP]
            up = gu[:, (2 * n + 1) * FFN_GROUP:(2 * n + 2) * FFN_GROUP]
            parts.append((gate * jax.nn.sigmoid(gate) * up).astype(BF16))
        return jnp.concatenate(parts, axis=1)

    def down(j, slot):
        return _dot(h_ref[(j - 1) % 2], wd_buf[slot])

    fetch_ahead_and_wait(jnp.int32(0))
    h_ref[0] = hidden(0)

    fetch_ahead_and_wait(jnp.int32(1))
    o_ref[...] = down(1, 1)
    h_ref[1] = hidden(1)

    def body(j, carry):
        fetch_ahead_and_wait(j)
        slot = j % FFN_RING_SLOTS
        o_ref[...] += down(j, slot)
        h_ref[j % 2] = hidden(slot)
        return carry

    lax.fori_loop(2, chunks, body, 0)

    fetch_ahead_and_wait(jnp.int32(chunks))
    acc = o_ref[...] + down(chunks, chunks % FFN_RING_SLOTS)
    y = _layer_norm(ALPHA * x_ref[...] + 0.5 * acc, g_ref[...], b_ref[...])
    o_ref[...] = y
    ob_ref[...] = y.astype(BF16)


def _ffn(x, xb, wgu, wd, g, b):
    s = x.shape[0]
    row = pl.BlockSpec((FFN_ROWS, D_MODEL), lambda i: (i, 0))
    hbm = pl.BlockSpec(memory_space=pl.ANY)
    vec = pl.BlockSpec((1, D_MODEL), lambda i: (0, 0))
    scratch = [
        pltpu.VMEM((2, FFN_ROWS, FFN_COLS), BF16),
        pltpu.VMEM((FFN_RING_SLOTS, D_MODEL, 2 * FFN_COLS), BF16),
        pltpu.VMEM((FFN_RING_SLOTS, FFN_COLS, D_MODEL), BF16),
        pltpu.SemaphoreType.DMA((2, FFN_RING_SLOTS)),
    ]
    cast_input = xb is None
    return pl.pallas_call(
        functools.partial(_ffn_kernel, cast_input=cast_input),
        grid=(s // FFN_ROWS,),
        in_specs=[row] + ([] if cast_input else [row]) + [hbm, hbm, vec, vec],
        out_specs=[row, row],
        out_shape=[jax.ShapeDtypeStruct((s, D_MODEL), F32), jax.ShapeDtypeStruct((s, D_MODEL), BF16)],
        scratch_shapes=scratch + ([pltpu.VMEM((FFN_ROWS, D_MODEL), BF16)] if cast_input else []),
        compiler_params=_params(("arbitrary",)),
        name="ffn",
    )(*([x] if cast_input else [x, xb]), wgu, wd, g, b)


def _cast_kernel(wg_ref, wu_ref, wd_ref, ogu_ref, od_ref):
    for n in range(D_FF // FFN_GROUP):
        c, k = divmod(n, FFN_COLS // FFN_GROUP)
        src = slice(n * FFN_GROUP, (n + 1) * FFN_GROUP)
        ogu_ref[c, :, 2 * k * FFN_GROUP:(2 * k + 1) * FFN_GROUP] = wg_ref[:, src].astype(BF16)
        ogu_ref[c, :, (2 * k + 1) * FFN_GROUP:(2 * k + 2) * FFN_GROUP] = wu_ref[:, src].astype(BF16)
    od_ref[...] = wd_ref[...].astype(BF16)


def _ffn_weights_bf16(w_gate, w_up, w_down, layer):
    steps = CAST_STEPS
    up_rows, down_rows = D_MODEL // steps, D_FF // steps
    up_in = pl.BlockSpec((None, up_rows, D_FF), lambda r: (layer, r, 0))
    up_out = pl.BlockSpec((FFN_CHUNKS, up_rows, 2 * FFN_COLS), lambda r: (0, r, 0))
    up_shape = jax.ShapeDtypeStruct((FFN_CHUNKS, D_MODEL, 2 * FFN_COLS), BF16)
    wgu, wd = pl.pallas_call(
        _cast_kernel,
        grid=(steps,),
        in_specs=[up_in, up_in, pl.BlockSpec((None, down_rows, D_MODEL), lambda r: (layer, r, 0))],
        out_specs=[up_out, pl.BlockSpec((down_rows, D_MODEL), lambda r: (r, 0))],
        out_shape=[up_shape, jax.ShapeDtypeStruct((D_FF, D_MODEL), BF16)],
        compiler_params=_params(("parallel",)),
        name="ffn_weights_bf16",
    )(w_gate, w_up, w_down)
    return wgu, wd.reshape(FFN_CHUNKS, FFN_COLS, D_MODEL)


def _attn_in_kernel(x_ref, w_ref, wf_ref, o_ref, f_ref):
    x = x_ref[...]
    o_ref[...] = _dot(x, w_ref[...]).astype(o_ref.dtype)

    @pl.when(pl.program_id(1) == 0)
    def _():
        f_ref[...] = _dot(x, wf_ref[...])


def _attn_in(xb, w, wf):
    s, n = xb.shape[0], w.shape[1]
    return pl.pallas_call(
        _attn_in_kernel,
        grid=(s // MM_ROWS, n // MM_COLS),
        in_specs=[
            pl.BlockSpec((MM_ROWS, D_MODEL), lambda i, j: (i, 0)),
            pl.BlockSpec((D_MODEL, MM_COLS), lambda i, j: (0, j)),
            pl.BlockSpec((D_MODEL, V7X_LANES), lambda i, j: (0, 0)),
        ],
        out_specs=[
            pl.BlockSpec((MM_ROWS, MM_COLS), lambda i, j: (i, j)),
            pl.BlockSpec((MM_ROWS, V7X_LANES), lambda i, j: (i, 0)),
        ],
        out_shape=[jax.ShapeDtypeStruct((s, n), BF16), jax.ShapeDtypeStruct((s, V7X_LANES), F32)],
        compiler_params=_params(("parallel", "arbitrary")),
        name="attn_in",
    )(xb, w, wf)


def _mm_kernel(x_ref, w_ref, o_ref):
    o_ref[...] = _dot(x_ref[...], w_ref[...]).astype(o_ref.dtype)


def _matmul(xb, w, out_dtype):
    s, n = xb.shape[0], w.shape[1]
    return pl.pallas_call(
        _mm_kernel,
        grid=(s // MM_ROWS, n // MM_COLS),
        in_specs=[
            pl.BlockSpec((MM_ROWS, xb.shape[1]), lambda i, j: (i, 0)),
            pl.BlockSpec((xb.shape[1], MM_COLS), lambda i, j: (0, j)),
        ],
        out_specs=pl.BlockSpec((MM_ROWS, MM_COLS), lambda i, j: (i, j)),
        out_shape=jax.ShapeDtypeStruct((s, n), out_dtype),
        compiler_params=_params(("parallel", "parallel")),
        name="matmul",
    )(xb, w)


def _split3(v):
    hi = v.astype(BF16)
    r1 = v - hi.astype(F32)
    mid = r1.astype(BF16)
    lo = (r1 - mid.astype(F32)).astype(BF16)
    return hi, mid, lo


def _cumsum_kernel(f_ref, bf_ref, c_ref, carry_ref):
    @pl.when(pl.program_id(0) == 0)
    def _():
        carry_ref[...] = jnp.zeros_like(carry_ref)

    z = f_ref[...] + bf_ref[...]
    log_f = jnp.minimum(z, 0.0) - jnp.log1p(jnp.exp(-jnp.abs(z)))
    n = z.shape[0]
    tri = (lax.broadcasted_iota(jnp.int32, (n, n), 0) >= lax.broadcasted_iota(jnp.int32, (n, n), 1)).astype(BF16)
    hi, mid, lo = _split3(log_f)
    c = _dot(tri, hi) + _dot(tri, mid) + _dot(tri, lo) + carry_ref[0:1, :]
    c_ref[...] = c
    carry_ref[...] = jnp.broadcast_to(c[n - 1:n, :], carry_ref.shape)


def _forget_cumsum(f_logit, b_f_row):
    s = f_logit.shape[0]
    return pl.pallas_call(
        _cumsum_kernel,
        grid=(s // CUMSUM_ROWS,),
        in_specs=[
            pl.BlockSpec((CUMSUM_ROWS, V7X_LANES), lambda i: (i, 0)),
            pl.BlockSpec((1, V7X_LANES), lambda i: (0, 0)),
        ],
        out_specs=pl.BlockSpec((CUMSUM_ROWS, V7X_LANES), lambda i: (i, 0)),
        out_shape=jax.ShapeDtypeStruct((s, V7X_LANES), F32),
        scratch_shapes=[pltpu.VMEM((V7X_SUBLANES, V7X_LANES), F32)],
        compiler_params=_params(("arbitrary",)),
        name="forget_cumsum",
    )(f_logit, b_f_row)


def _fox_prep_kernel(qkv_ref, c_ref, qt_ref, ka_ref, vt_ref):
    rows = qkv_ref.shape[0]
    lane = lax.broadcasted_iota(jnp.int32, (rows, V7X_LANES), 1)
    bias = c_ref[...] * (-LOG2E)
    ones_rows = (lax.broadcasted_iota(jnp.int32, (FOX_AUG, rows), 0) < 3).astype(BF16)
    sum_rows = (lax.broadcasted_iota(jnp.int32, (FOX_VROWS - HEAD_DIM, rows), 0) < 1).astype(BF16)
    for h in range(N_FOX_HEADS):
        hs = slice(h * HEAD_DIM, (h + 1) * HEAD_DIM)
        a0 = h * (HEAD_DIM + FOX_AUG)
        qt_ref[a0:a0 + HEAD_DIM, :] = qkv_ref[:, hs].astype(F32).T.astype(BF16)
        qt_ref[a0 + HEAD_DIM:a0 + HEAD_DIM + FOX_AUG, :] = ones_rows
        v0 = h * FOX_VROWS
        vt_ref[v0:v0 + HEAD_DIM, :] = (
            qkv_ref[:, 2 * FOX_W + h * HEAD_DIM:2 * FOX_W + (h + 1) * HEAD_DIM].astype(F32).T.astype(BF16))
        vt_ref[v0 + HEAD_DIM:v0 + FOX_VROWS, :] = sum_rows
        hi, mid, lo = _split3(jnp.broadcast_to(bias[:, h:h + 1], (rows, V7X_LANES)))
        aug = jnp.where(lane == 0, hi.astype(F32), jnp.where(lane == 1, mid.astype(F32),
                                                              jnp.where(lane == 2, lo.astype(F32), 0.0)))
        ka_ref[:, a0:a0 + HEAD_DIM] = qkv_ref[:, FOX_W + h * HEAD_DIM:FOX_W + (h + 1) * HEAD_DIM]
        ka_ref[:, a0 + HEAD_DIM:a0 + HEAD_DIM + FOX_AUG] = aug.astype(BF16)


def _fox_prep(qkv, c_rows):
    s = qkv.shape[0]
    wide = N_FOX_HEADS * (HEAD_DIM + FOX_AUG)
    vrows = N_FOX_HEADS * FOX_VROWS
    return pl.pallas_call(
        _fox_prep_kernel,
        grid=(s // FOX_BLOCK,),
        in_specs=[pl.BlockSpec((FOX_BLOCK, 3 * FOX_W), lambda i: (i, 0)),
                  pl.BlockSpec((FOX_BLOCK, V7X_LANES), lambda i: (i, 0))],
        out_specs=[pl.BlockSpec((wide, FOX_BLOCK), lambda i: (0, i)),
                   pl.BlockSpec((FOX_BLOCK, wide), lambda i: (i, 0)),
                   pl.BlockSpec((vrows, FOX_BLOCK), lambda i: (0, i))],
        out_shape=[jax.ShapeDtypeStruct((wide, s), BF16), jax.ShapeDtypeStruct((s, wide), BF16),
                   jax.ShapeDtypeStruct((vrows, s), BF16)],
        compiler_params=_params(("parallel",)),
        name="fox_prep",
    )(qkv, c_rows)


def _fox_kernel(qi_ref, kj_ref, qt_ref, ka_ref, vt_ref, o_ref, m_ref, acc_ref, s_ref):
    t = pl.program_id(0)
    i = qi_ref[t]
    j = kj_ref[t]
    blk = ka_ref.shape[0]
    aw = HEAD_DIM + FOX_AUG

    @pl.when(j == 0)
    def _():
        m_ref[...] = jnp.full_like(m_ref, NEG)
        acc_ref[...] = jnp.zeros_like(acc_ref)

    def scores(h):
        return _dot(ka_ref[:, h * aw:(h + 1) * aw], qt_ref[h * aw:(h + 1) * aw, :])

    def step(diagonal):
        if diagonal:
            keep = lax.broadcasted_iota(jnp.int32, (blk, blk), 0) <= lax.broadcasted_iota(jnp.int32, (blk, blk), 1)
        s_ref[0] = scores(0)
        for h in range(N_FOX_HEADS):
            if h + 1 < N_FOX_HEADS:
                s_ref[(h + 1) % 2] = scores(h + 1)
            s = s_ref[h % 2]
            if diagonal:
                s = jnp.where(keep, s, NEG)
            m_prev = m_ref[h:h + 1, :]
            m_new = jnp.maximum(m_prev, jnp.max(s, axis=0, keepdims=True))
            p = jnp.exp2(s - m_new).astype(BF16)
            alpha = jnp.exp2(m_prev - m_new)
            m_ref[h:h + 1, :] = m_new
            vs = slice(h * FOX_VROWS, (h + 1) * FOX_VROWS)
            acc_ref[vs, :] = alpha * acc_ref[vs, :] + _dot(vt_ref[vs, :], p)

    @pl.when(j < i)
    def _():
        step(False)

    @pl.when(j == i)
    def _():
        step(True)
        for h in range(N_FOX_HEADS):
            v0 = h * FOX_VROWS
            out_t = acc_ref[v0:v0 + HEAD_DIM, :] / acc_ref[v0 + HEAD_DIM:v0 + HEAD_DIM + 1, :]
            o_ref[:, h * HEAD_DIM:(h + 1) * HEAD_DIM] = out_t.T.astype(o_ref.dtype)


def _fox_attention(qt, ka, vt):
    s = ka.shape[0]
    nb = s // FOX_BLOCK
    qi = jnp.asarray([i for i in range(nb) for _ in range(i + 1)], jnp.int32)
    kj = jnp.asarray([j for i in range(nb) for j in range(i + 1)], jnp.int32)
    wide = ka.shape[1]
    vrows = vt.shape[0]
    grid_spec = pltpu.PrefetchScalarGridSpec(
        num_scalar_prefetch=2,
        grid=(qi.shape[0],),
        in_specs=[
            pl.BlockSpec((wide, FOX_BLOCK), lambda t, qi, kj: (0, qi[t])),
            pl.BlockSpec((FOX_BLOCK, wide), lambda t, qi, kj: (kj[t], 0)),
            pl.BlockSpec((vrows, FOX_BLOCK), lambda t, qi, kj: (0, kj[t])),
        ],
        out_specs=pl.BlockSpec((FOX_BLOCK, FOX_W), lambda t, qi, kj: (qi[t], 0)),
        scratch_shapes=[
            pltpu.VMEM((N_FOX_HEADS, FOX_BLOCK), F32),
            pltpu.VMEM((vrows, FOX_BLOCK), F32),
            pltpu.VMEM((2, FOX_BLOCK, FOX_BLOCK), F32),
        ],
    )
    return pl.pallas_call(
        _fox_kernel,
        grid_spec=grid_spec,
        out_shape=jax.ShapeDtypeStruct((s, FOX_W), BF16),
        compiler_params=_params(("arbitrary",)),
        name="fox_attention",
    )(qi, kj, qt, ka, vt)


def _dil_kernel(q_ref, k_ref, v_ref, kp_ref, vp_ref, y_ref, o_scr, l_scr):
    has_prev_tile = pl.program_id(0) > 0
    b = DIL_BLOCK
    tile = q_ref.shape[0]
    jr = lax.broadcasted_iota(jnp.int32, (b, 2 * b), 0)
    mc = lax.broadcasted_iota(jnp.int32, (b, 2 * b), 1)
    band = (mc >= jr) & (mc <= jr + b)
    band_first = band & ((mc >= b) | has_prev_tile)

    def rows(ref, start, d):
        return ref[pl.ds(start, b), :] if d == 1 else ref[pl.ds(start, b, stride=d), :]

    for g, (_, d) in enumerate(DIL_PAIRS):
        for r in range(d):
            first = tile - b * d + r
            k_blk = [rows(kp_ref, first, d).astype(BF16)]
            v_blk = [rows(vp_ref, first, d).astype(BF16)]
            for n in range(tile // (b * d)):
                cur = r + n * b * d
                k_blk.append(rows(k_ref, cur, d).astype(BF16))
                v_blk.append(rows(v_ref, cur, d).astype(BF16))
                k2 = jnp.concatenate(k_blk[n:n + 2], axis=0)
                v2 = jnp.concatenate(v_blk[n:n + 2], axis=0)
                keep = band if n > 0 else band_first
                s = jnp.where(keep, _dot_nt(rows(q_ref, cur, d).astype(BF16), k2), NEG)
                m = jnp.max(s, axis=1, keepdims=True)
                p = jnp.exp(s - m)
                l = jnp.sum(p, axis=1, keepdims=True)
                o = _dot(p.astype(BF16), v2) / l
                lse = jnp.broadcast_to(m + jnp.log(l), (b, HEAD_DIM))
                if d == 1:
                    o_scr[g, pl.ds(cur, b), :] = o
                    l_scr[g, pl.ds(cur, b), :] = lse
                else:
                    o_scr[g, pl.ds(cur, b, stride=d), :] = o
                    l_scr[g, pl.ds(cur, b, stride=d), :] = lse

    l1, l2, l3 = l_scr[0], l_scr[1], l_scr[2]
    mx = jnp.maximum(jnp.maximum(l1, l2), l3)
    e1, e2, e3 = jnp.exp(l1 - mx), jnp.exp(l2 - mx), jnp.exp(l3 - mx)
    den = e1 + e2 + e3
    y_ref[...] = ((e1 / den) * o_scr[0] + (e2 / den) * o_scr[1] + (e3 / den) * o_scr[2]).astype(y_ref.dtype)


def _dilated_attention(qkv):
    s = qkv.shape[0]
    cur = lambda c: pl.BlockSpec((DIL_TILE, HEAD_DIM), lambda t, h: (t, c * N_DIL_HEADS + h))
    prev = lambda c: pl.BlockSpec((DIL_TILE, HEAD_DIM), lambda t, h: (jnp.maximum(t - 1, 0), c * N_DIL_HEADS + h))
    return pl.pallas_call(
        _dil_kernel,
        grid=(s // DIL_TILE, N_DIL_HEADS),
        in_specs=[cur(0), cur(1), cur(2), prev(1), prev(2)],
        out_specs=pl.BlockSpec((DIL_TILE, HEAD_DIM), lambda t, h: (t, h)),
        out_shape=jax.ShapeDtypeStruct((s, DIL_W), BF16),
        scratch_shapes=[pltpu.VMEM((len(DIL_PAIRS), DIL_TILE, HEAD_DIM), F32),
                        pltpu.VMEM((len(DIL_PAIRS), DIL_TILE, HEAD_DIM), F32)],
        compiler_params=_params(("parallel", "parallel")),
        name="dilated_attention",
    )(qkv, qkv, qkv, qkv, qkv)


def _attn_out_kernel(x_ref, ya_ref, yb_ref, w_ref, g_ref, b_ref, o_ref, ob_ref):
    half = x_ref.shape[0] // 2
    for r in range(2):
        rows = slice(r * half, (r + 1) * half)
        y = _dot(jnp.concatenate([ya_ref[rows, :], yb_ref[rows, :]], axis=1), w_ref[...])
        out = _layer_norm(ALPHA * x_ref[rows, :] + y, g_ref[...], b_ref[...])
        o_ref[rows, :] = out
        ob_ref[rows, :] = out.astype(BF16)


def _attn_out(x, ya, yb, w, g, b):
    s = x.shape[0]
    row = lambda width: pl.BlockSpec((OUT_ROWS, width), lambda i: (i, 0))
    const = lambda shape: pl.BlockSpec(shape, lambda i: (0, 0))
    return pl.pallas_call(
        _attn_out_kernel,
        grid=(s // OUT_ROWS,),
        in_specs=[row(D_MODEL), row(FOX_W), row(DIL_W),
                  const((FOX_W + DIL_W, D_MODEL)), const((1, D_MODEL)), const((1, D_MODEL))],
        out_specs=[row(D_MODEL), row(D_MODEL)],
        out_shape=[jax.ShapeDtypeStruct((s, D_MODEL), F32), jax.ShapeDtypeStruct((s, D_MODEL), BF16)],
        compiler_params=_params(("parallel",)),
        name="attn_out",
    )(x, ya, yb, w, g, b)


def _cmul(ar, ai, br, bi):
    return ar * br - ai * bi, ar * bi + ai * br


def _s5_prep_kernel(lrg_ref, lig_ref, ldtg_ref, lr_ref, li_ref, ldt_ref, br_ref, bi_ref,
                    pr_ref, pi_ref, qr_ref, qi_ref, bbr_ref, bbi_ref):
    def a_bar(lr, li, ldt):
        dt = jnp.exp(ldt)
        mag = jnp.exp(lr * dt)
        return mag * jnp.cos(li * dt), mag * jnp.sin(li * dt)

    lr, li = lr_ref[...], li_ref[...]
    ar, ai = a_bar(lr, li, ldt_ref[...])
    den = lr * lr + li * li
    cr = ((ar - 1.0) * lr + ai * li) / den
    ci = (ai * lr - (ar - 1.0) * li) / den
    bbr_ref[...], bbi_ref[...] = _cmul(cr, ci, br_ref[...], bi_ref[...])

    ar, ai = a_bar(lrg_ref[...], lig_ref[...], ldtg_ref[...])
    pr, pi = ar, ai
    pr_ref[0], pi_ref[0] = pr, pi
    for k in range(1, S5_SEG):
        pr, pi = _cmul(pr, pi, ar, ai)
        pr_ref[k], pi_ref[k] = pr, pi
    sr, si = pr, pi
    qr_ref[0], qi_ref[0] = sr, si
    for k in range(1, V7X_SUBLANES):
        pr, pi = _cmul(pr, pi, sr, si)
        qr_ref[k], qi_ref[k] = pr, pi


def _s5_prep(lam_re, lam_im, log_dt, b_re, b_im):
    ldt = jnp.broadcast_to(log_dt[:, None], lam_re.shape)
    rep = lambda v: jnp.repeat(v, S5_GROUP, axis=0)
    rows = S5_GROUPS * S5_GROUP
    bt = lambda v: v.transpose(0, 2, 1).reshape(rows, S5_STATE)
    mat = jax.ShapeDtypeStruct((rows, S5_STATE), F32)
    pw = jax.ShapeDtypeStruct((S5_SEG, S5_GROUPS, S5_STATE), F32)
    qw = jax.ShapeDtypeStruct((V7X_SUBLANES, S5_GROUPS, S5_STATE), F32)
    return pl.pallas_call(_s5_prep_kernel, out_shape=[pw, pw, qw, qw, mat, mat], name="s5_prep")(
        lam_re, lam_im, ldt, rep(lam_re), rep(lam_im), rep(ldt), bt(b_re), bt(b_im))


def _s5_core_kernel(u_ref, bblk_ref, cblk_ref, pr_ref, pi_ref, qr_ref, qi_ref, d_ref, z_ref,
                    up_ref, sr_ref, si_ref, cr_ref, ci_ref):
    @pl.when(pl.program_id(0) == 0)
    def _():
        cr_ref[...] = jnp.zeros_like(cr_ref)
        ci_ref[...] = jnp.zeros_like(ci_ref)

    sl = V7X_SUBLANES
    rows = u_ref.shape[0]
    ri = lax.broadcasted_iota(jnp.int32, (rows, rows), 0)
    ci = lax.broadcasted_iota(jnp.int32, (rows, rows), 1)
    perm = ((ri // sl == ci % S5_SEG) & (ri % sl == ci // S5_SEG)).astype(BF16)
    unperm = ((ci // sl == ri % S5_SEG) & (ci % sl == ri // S5_SEG)).astype(BF16)
    hi, mid, lo = _split3(u_ref[...])
    up_ref[...] = _dot(perm, hi) + _dot(perm, mid) + _dot(perm, lo)

    sub = lax.broadcasted_iota(jnp.int32, (sl, S5_LANES), 0)
    width = S5_GB * S5_GROUP
    ys = []
    for jb in range(S5_NB):
        bu = _dot(up_ref[:, jb * width:(jb + 1) * width].astype(BF16), bblk_ref[jb])
        a_re, a_im = pr_ref[jb, 0:1, :], pi_ref[jb, 0:1, :]
        xr, xi = bu[0:sl, :S5_LANES], bu[0:sl, S5_LANES:]
        sr_ref[0:sl, :], si_ref[0:sl, :] = xr, xi
        for j in range(1, S5_SEG):
            tr, ti = _cmul(a_re, a_im, xr, xi)
            xr, xi = tr + bu[j * sl:(j + 1) * sl, :S5_LANES], ti + bu[j * sl:(j + 1) * sl, S5_LANES:]
            sr_ref[j * sl:(j + 1) * sl, :], si_ref[j * sl:(j + 1) * sl, :] = xr, xi

        q_re, q_im = qr_ref[jb], qi_ref[jb]
        er, ei = xr, xi
        for k in (1, 2, 4):
            tr = jnp.where(sub >= k, pltpu.roll(er, k, 0), 0.0)
            ti = jnp.where(sub >= k, pltpu.roll(ei, k, 0), 0.0)
            mr, mi = _cmul(q_re[k - 1:k, :], q_im[k - 1:k, :], tr, ti)
            er, ei = er + mr, ei + mi
        in_re, in_im = cr_ref[jb:jb + 1, :], ci_ref[jb:jb + 1, :]
        mr, mi = _cmul(q_re, q_im, in_re, in_im)
        er, ei = er + mr, ei + mi
        cr_ref[jb:jb + 1, :], ci_ref[jb:jb + 1, :] = er[sl - 1:, :], ei[sl - 1:, :]
        cin_re = jnp.where(sub >= 1, pltpu.roll(er, 1, 0), in_re)
        cin_im = jnp.where(sub >= 1, pltpu.roll(ei, 1, 0), in_im)
        fr, fi = _cmul(pr_ref[jb][:, None, :], pi_ref[jb][:, None, :], cin_re[None], cin_im[None])
        shape3 = (S5_SEG, sl, S5_LANES)
        state = jnp.concatenate([(sr_ref[...].reshape(shape3) + fr).reshape(-1, S5_LANES),
                                 (si_ref[...].reshape(shape3) + fi).reshape(-1, S5_LANES)], axis=1)
        ys.append(_dot(state.astype(BF16), cblk_ref[jb]))
    y = jnp.concatenate(ys, axis=1) + d_ref[...] * up_ref[...]
    z = 0.5 * y * (1.0 + jnp.tanh(math.sqrt(2.0 / math.pi) * (y + 0.044715 * (y * y * y))))
    z_ref[...] = _dot(unperm, z.astype(BF16)).astype(z_ref.dtype)


def _s5_core(u, bblk, cblk, p_re, p_im, q_re, q_im, d_row):
    s = u.shape[0]
    const = lambda shape: pl.BlockSpec(shape, lambda t: (0,) * len(shape))
    return pl.pallas_call(
        _s5_core_kernel,
        grid=(s // S5_ROWS,),
        in_specs=[pl.BlockSpec((S5_ROWS, S5_WIDTH), lambda t: (t, 0)),
                  const(bblk.shape), const(cblk.shape), const(p_re.shape), const(p_im.shape),
                  const(q_re.shape), const(q_im.shape), const((1, S5_WIDTH))],
        out_specs=pl.BlockSpec((S5_ROWS, S5_WIDTH), lambda t: (t, 0)),
        out_shape=jax.ShapeDtypeStruct((s, S5_WIDTH), BF16),
        scratch_shapes=[pltpu.VMEM((S5_ROWS, S5_WIDTH), F32),
                        pltpu.VMEM((S5_ROWS, S5_LANES), F32), pltpu.VMEM((S5_ROWS, S5_LANES), F32),
                        pltpu.VMEM((V7X_SUBLANES, S5_LANES), F32), pltpu.VMEM((V7X_SUBLANES, S5_LANES), F32)],
        compiler_params=_params(("arbitrary",)),
        name="s5_core",
    )(u, bblk, cblk, p_re, p_im, q_re, q_im, d_row)


def _block_diag(blocks):
    nb, gb, r, c = blocks.shape
    eye = jnp.eye(gb, dtype=blocks.dtype)
    return jnp.einsum("ngrc,gk->ngrkc", blocks, eye).reshape(nb, gb * r, gb * c)


def _glu_kernel(x_ref, z_ref, wo_ref, wg_ref, g_ref, b_ref, o_ref, ob_ref):
    half = z_ref.shape[0] // 2
    for r in range(2):
        rows = slice(r * half, (r + 1) * half)
        z = z_ref[rows, :]
        y = _dot(z, wo_ref[...]) * jax.nn.sigmoid(_dot(z, wg_ref[...]))
        out = _layer_norm(ALPHA * x_ref[rows, :] + y, g_ref[...], b_ref[...])
        o_ref[rows, :] = out
        ob_ref[rows, :] = out.astype(BF16)


def _glu(x, z, wo, wg, g, b):
    s = x.shape[0]
    row = lambda width: pl.BlockSpec((GLU_ROWS, width), lambda i: (i, 0))
    const = lambda shape: pl.BlockSpec(shape, lambda i: (0, 0))
    return pl.pallas_call(
        _glu_kernel,
        grid=(s // GLU_ROWS,),
        in_specs=[row(D_MODEL), row(S5_WIDTH), const((S5_WIDTH, D_MODEL)), const((S5_WIDTH, D_MODEL)),
                  const((1, D_MODEL)), const((1, D_MODEL))],
        out_specs=[row(D_MODEL), row(D_MODEL)],
        out_shape=[jax.ShapeDtypeStruct((s, D_MODEL), F32), jax.ShapeDtypeStruct((s, D_MODEL), BF16)],
        compiler_params=_params(("parallel",)),
        name="glu",
    )(x, z, wo, wg, g, b)


def _attn_weights_kernel(w_ref, fox_ref, f_ref, dil_ref):
    scale = HEAD_DIM ** -0.5
    w = w_ref[...]
    d0 = 3 * FOX_W + N_FOX_HEADS
    fox_ref[:, 0:FOX_W] = (w[:, 0:FOX_W] * (scale * LOG2E)).astype(BF16)
    fox_ref[:, FOX_W:3 * FOX_W] = w[:, FOX_W:3 * FOX_W].astype(BF16)
    f_ref[...] = jnp.zeros_like(f_ref)
    f_ref[:, 0:N_FOX_HEADS] = w[:, 3 * FOX_W:d0].astype(BF16)
    dil_ref[:, 0:DIL_W] = (w[:, d0:d0 + DIL_W] * scale).astype(BF16)
    dil_ref[:, DIL_W:3 * DIL_W] = w[:, d0 + DIL_W:d0 + 3 * DIL_W].astype(BF16)


def _attn_weights(w_in):
    rows = D_MODEL // CAST_STEPS
    out = lambda width: pl.BlockSpec((rows, width), lambda r: (r, 0))
    return pl.pallas_call(
        _attn_weights_kernel,
        grid=(CAST_STEPS,),
        in_specs=[pl.BlockSpec((rows, w_in.shape[1]), lambda r: (r, 0))],
        out_specs=[out(3 * FOX_W), out(V7X_LANES), out(3 * DIL_W)],
        out_shape=[jax.ShapeDtypeStruct((D_MODEL, 3 * FOX_W), BF16), jax.ShapeDtypeStruct((D_MODEL, V7X_LANES), BF16),
                   jax.ShapeDtypeStruct((D_MODEL, 3 * DIL_W), BF16)],
        compiler_params=_params(("parallel",)),
        name="attn_weights",
    )(w_in)


def _attention_mixer(x, xb, w_in, b_f, w_out, g, b):
    w_fox, w_f, w_dil = _attn_weights(w_in)
    qkv_fox, f_logit = _attn_in(xb, w_fox, w_f)
    c_rows = _forget_cumsum(f_logit, jnp.pad(b_f, (0, V7X_LANES - N_FOX_HEADS))[None, :])
    ya = _fox_attention(*_fox_prep(qkv_fox, c_rows))
    yb = _dilated_attention(_matmul(xb, w_dil, F32))
    return _attn_out(x, ya, yb, w_out.astype(BF16), g, b)


def _s5_mixer(x, xb, w_in, lam_re, lam_im, log_dt, b_re, b_im, c_re, c_im, d_skip, w_glu_out, w_glu_gate, g, b):
    u = _matmul(xb, w_in.astype(BF16), F32)
    p_re, p_im, q_re, q_im, bb_re, bb_im = _s5_prep(lam_re, lam_im, log_dt, b_re, b_im)
    lanes = lambda p: p.reshape(p.shape[0], S5_NB, S5_LANES).transpose(1, 0, 2)
    blk = lambda v: v.reshape(S5_NB, S5_GB, S5_GROUP, S5_STATE)
    bblk = jnp.concatenate([_block_diag(blk(bb_re)), _block_diag(blk(bb_im))], axis=2).astype(BF16)
    ct = lambda v: v.transpose(0, 2, 1).reshape(S5_NB, S5_GB, S5_STATE, S5_GROUP)
    cblk = jnp.concatenate([_block_diag(ct(c_re)), -_block_diag(ct(c_im))], axis=1).astype(BF16)
    z = _s5_core(u, bblk, cblk, lanes(p_re), lanes(p_im), lanes(q_re), lanes(q_im), d_skip[None, :])
    return _glu(x, z, w_glu_out.astype(BF16), w_glu_gate.astype(BF16), g, b)


def kernel(x, ffn1_w_gate, ffn1_w_up, ffn1_w_down, ffn2_w_gate, ffn2_w_up, ffn2_w_down, ln_gain, ln_bias,
           attn_w_in, attn_b_f, attn_w_out, s5_w_in, s5_lambda_re, s5_lambda_im, s5_log_dt, s5_b_re, s5_b_im,
           s5_c_re, s5_c_im, s5_d, s5_w_glu_out, s5_w_glu_gate):
    batch, seq, _ = x.shape
    assert batch == 1
    h = x.reshape(seq, D_MODEL)
    hb = None
    ln = lambda i, k: (ln_gain[i, k][None, :], ln_bias[i, k][None, :])
    for i in range(DEPTH):
        h, hb = _ffn(h, hb, *_ffn_weights_bf16(ffn1_w_gate, ffn1_w_up, ffn1_w_down, i), *ln(i, 0))
        j = i // 2
        if i % 2 == 0:
            h, hb = _attention_mixer(h, hb, attn_w_in[j], attn_b_f[j], attn_w_out[j], *ln(i, 1))
        else:
            h, hb = _s5_mixer(h, hb, s5_w_in[j], s5_lambda_re[j], s5_lambda_im[j], s5_log_dt[j], s5_b_re[j],
                              s5_b_im[j], s5_c_re[j], s5_c_im[j], s5_d[j], s5_w_glu_out[j], s5_w_glu_gate[j],
                              *ln(i, 1))
        h, hb = _ffn(h, hb, *_ffn_weights_bf16(ffn2_w_gate, ffn2_w_up, ffn2_w_down, i), *ln(i, 2))
    return h.reshape(batch, seq, D_MODEL)
```

```python
import functools
import math

import jax
import jax.numpy as jnp
from jax import lax
from jax.experimental import pallas as pl
from jax.experimental.pallas import tpu as pltpu

D_MODEL = 2048
DEPTH = 2
HEAD_DIM = 128
N_FOX_HEADS = 8
N_DIL_HEADS = 8
DIL_PAIRS = ((128, 1), (512, 4), (2048, 16))
DIL_BLOCK = 128
D_FF = 5632
S5_GROUP = 16
S5_WIDTH = 1024
S5_GROUPS = S5_WIDTH // S5_GROUP
S5_STATE = 64
ALPHA = (2.0 * DEPTH) ** 0.25
LN_EPS = 1e-5
FOX_W = N_FOX_HEADS * HEAD_DIM
DIL_W = N_DIL_HEADS * HEAD_DIM

F32 = jnp.float32
BF16 = jnp.bfloat16
NEG = -1e30
LOG2E = math.log2(math.e)
FOX_AUG = 128
FOX_VROWS = HEAD_DIM + 16

V7X_LANES = 128
V7X_SUBLANES = 8
V7X_VMEM_LIMIT_BYTES = 56 * 1024 * 1024

FFN_ROWS = 512
FFN_COLS = 512
FFN_CHUNKS = D_FF // FFN_COLS
FFN_GROUP = 256
FFN_RING_SLOTS = 3
FFN_UNROLL = 3
assert (FFN_CHUNKS - 2) % FFN_UNROLL == 0
assert (FFN_CHUNKS + 1) % FFN_RING_SLOTS == 0
MM_ROWS = 2048
MM_COLS = 1024
FOX_BLOCK = 1024
DIL_TILE = 2048
OUT_ROWS = 512
GLU_ROWS = 512
CUMSUM_ROWS = 512
CAST_STEPS = 8
S5_ROWS = 256
S5_SEG = S5_ROWS // V7X_SUBLANES
S5_GB = 16
S5_NB = S5_GROUPS // S5_GB
S5_LANES = S5_GB * S5_STATE


def _params(sem):
    return pltpu.CompilerParams(dimension_semantics=sem, vmem_limit_bytes=V7X_VMEM_LIMIT_BYTES)


def _layer_norm(v, g, b):
    mu = jnp.mean(v, axis=-1, keepdims=True)
    d = v - mu
    var = jnp.mean(d * d, axis=-1, keepdims=True)
    return d * lax.rsqrt(var + LN_EPS) * g + b


def _dot(a, b):
    return jnp.dot(a, b, preferred_element_type=F32)


def _dot_nt(a, b):
    return lax.dot_general(a, b, (((1,), (1,)), ((), ())), preferred_element_type=F32)


def _ffn_kernel(*refs, cast_input):
    if cast_input:
        x_ref, wgu_hbm, wd_hbm, g_ref, b_ref, o_ref, ob_ref, h_ref, wgu_buf, wd_buf, sem, xb_ref = refs
        xb_ref[...] = x_ref[...].astype(BF16)
    else:
        x_ref, xb_ref, wgu_hbm, wd_hbm, g_ref, b_ref, o_ref, ob_ref, h_ref, wgu_buf, wd_buf, sem = refs
    i = pl.program_id(0)
    last_tile = pl.num_programs(0) - 1
    chunks = FFN_CHUNKS
    nsub = chunks + 1

    def ring(sub, op):
        slot = sub % FFN_RING_SLOTS
        sub = jnp.where(sub >= nsub, sub - nsub, sub)
        op(pltpu.make_async_copy(wgu_hbm.at[jnp.minimum(sub, chunks - 1)], wgu_buf.at[slot], sem.at[0, slot]))
        op(pltpu.make_async_copy(wd_hbm.at[jnp.maximum(sub - 1, 0)], wd_buf.at[slot], sem.at[1, slot]))

    start = lambda cp: cp.start()
    wait = lambda cp: cp.wait()

    def hidden(slot):
        gu = _dot(xb_ref[...], wgu_buf[slot])
        parts = []
        for n in range(FFN_COLS // FFN_GROUP):
            gate = gu[:, 2 * n * FFN_GROUP:(2 * n + 1) * FFN_GROUP]
            up = gu[:, (2 * n + 1) * FFN_GROUP:(2 * n + 2) * FFN_GROUP]
            parts.append((gate * jax.nn.sigmoid(gate) * up).astype(BF16))
        return jnp.concatenate(parts, axis=1)

    def down(j, slot):
        return _dot(h_ref[(j - 1) % 2], wd_buf[slot])

    @pl.when(i == 0)
    def _():
        ring(jnp.int32(0), start)
        ring(jnp.int32(1), start)

    ring(jnp.int32(2), start)
    ring(jnp.int32(0), wait)
    h_ref[0] = hidden(0)

    ring(jnp.int32(3), start)
    ring(jnp.int32(1), wait)
    o_ref[...] = down(1, 1)
    h_ref[1] = hidden(1)

    def body(j, carry):
        ring(j, wait)
        slot = j % FFN_RING_SLOTS
        o_ref[...] += down(j, slot)
        h_ref[j % 2] = hidden(slot)
        ring(j + 2, start)
        return carry

    lax.fori_loop(2, chunks, body, 0, unroll=FFN_UNROLL)

    @pl.when(i < last_tile)
    def _():
        ring(jnp.int32(nsub + 1), start)

    ring(jnp.int32(chunks), wait)
    acc = o_ref[...] + down(chunks, chunks % FFN_RING_SLOTS)
    y = _layer_norm(ALPHA * x_ref[...] + 0.5 * acc, g_ref[...], b_ref[...])
    o_ref[...] = y
    ob_ref[...] = y.astype(BF16)

    @pl.when(i == last_tile)
    def _():
        ring(jnp.int32(nsub), wait)


def _ffn(x, xb, wgu, wd, g, b):
    s = x.shape[0]
    row = pl.BlockSpec((FFN_ROWS, D_MODEL), lambda i: (i, 0))
    hbm = pl.BlockSpec(memory_space=pl.ANY)
    vec = pl.BlockSpec((1, D_MODEL), lambda i: (0, 0))
    scratch = [
        pltpu.VMEM((2, FFN_ROWS, FFN_COLS), BF16),
        pltpu.VMEM((FFN_RING_SLOTS, D_MODEL, 2 * FFN_COLS), BF16),
        pltpu.VMEM((FFN_RING_SLOTS, FFN_COLS, D_MODEL), BF16),
        pltpu.SemaphoreType.DMA((2, FFN_RING_SLOTS)),
    ]
    cast_input = xb is None
    return pl.pallas_call(
        functools.partial(_ffn_kernel, cast_input=cast_input),
        grid=(s // FFN_ROWS,),
        in_specs=[row] + ([] if cast_input else [row]) + [hbm, hbm, vec, vec],
        out_specs=[row, row],
        out_shape=[jax.ShapeDtypeStruct((s, D_MODEL), F32), jax.ShapeDtypeStruct((s, D_MODEL), BF16)],
        scratch_shapes=scratch + ([pltpu.VMEM((FFN_ROWS, D_MODEL), BF16)] if cast_input else []),
        compiler_params=_params(("arbitrary",)),
        name="ffn",
    )(*([x] if cast_input else [x, xb]), wgu, wd, g, b)


def _cast_kernel(wg_ref, wu_ref, wd_ref, ogu_ref, od_ref):
    for n in range(D_FF // FFN_GROUP):
        c, k = divmod(n, FFN_COLS // FFN_GROUP)
        src = slice(n * FFN_GROUP, (n + 1) * FFN_GROUP)
        ogu_ref[c, :, 2 * k * FFN_GROUP:(2 * k + 1) * FFN_GROUP] = wg_ref[:, src].astype(BF16)
        ogu_ref[c, :, (2 * k + 1) * FFN_GROUP:(2 * k + 2) * FFN_GROUP] = wu_ref[:, src].astype(BF16)
    od_ref[...] = wd_ref[...].astype(BF16)


def _ffn_weights_bf16(w_gate, w_up, w_down, layer):
    steps = CAST_STEPS
    up_rows, down_rows = D_MODEL // steps, D_FF // steps
    up_in = pl.BlockSpec((None, up_rows, D_FF), lambda r: (layer, r, 0))
    up_out = pl.BlockSpec((FFN_CHUNKS, up_rows, 2 * FFN_COLS), lambda r: (0, r, 0))
    up_shape = jax.ShapeDtypeStruct((FFN_CHUNKS, D_MODEL, 2 * FFN_COLS), BF16)
    wgu, wd = pl.pallas_call(
        _cast_kernel,
        grid=(steps,),
        in_specs=[up_in, up_in, pl.BlockSpec((None, down_rows, D_MODEL), lambda r: (layer, r, 0))],
        out_specs=[up_out, pl.BlockSpec((down_rows, D_MODEL), lambda r: (r, 0))],
        out_shape=[up_shape, jax.ShapeDtypeStruct((D_FF, D_MODEL), BF16)],
        compiler_params=_params(("parallel",)),
        name="ffn_weights_bf16",
    )(w_gate, w_up, w_down)
    return wgu, wd.reshape(FFN_CHUNKS, FFN_COLS, D_MODEL)


def _attn_in_kernel(x_ref, w_ref, wf_ref, o_ref, f_ref):
    x = x_ref[...]
    o_ref[...] = _dot(x, w_ref[...]).astype(o_ref.dtype)

    @pl.when(pl.program_id(1) == 0)
    def _():
        f_ref[...] = _dot(x, wf_ref[...])


def _attn_in(xb, w, wf):
    s, n = xb.shape[0], w.shape[1]
    return pl.pallas_call(
        _attn_in_kernel,
        grid=(s // MM_ROWS, n // MM_COLS),
        in_specs=[
            pl.BlockSpec((MM_ROWS, D_MODEL), lambda i, j: (i, 0)),
            pl.BlockSpec((D_MODEL, MM_COLS), lambda i, j: (0, j)),
            pl.BlockSpec((D_MODEL, V7X_LANES), lambda i, j: (0, 0)),
        ],
        out_specs=[
            pl.BlockSpec((MM_ROWS, MM_COLS), lambda i, j: (i, j)),
            pl.BlockSpec((MM_ROWS, V7X_LANES), lambda i, j: (i, 0)),
        ],
        out_shape=[jax.ShapeDtypeStruct((s, n), BF16), jax.ShapeDtypeStruct((s, V7X_LANES), F32)],
        compiler_params=_params(("parallel", "arbitrary")),
        name="attn_in",
    )(xb, w, wf)


def _mm_kernel(x_ref, w_ref, o_ref):
    o_ref[...] = _dot(x_ref[...], w_ref[...]).astype(o_ref.dtype)


def _matmul(xb, w, out_dtype):
    s, n = xb.shape[0], w.shape[1]
    return pl.pallas_call(
        _mm_kernel,
        grid=(s // MM_ROWS, n // MM_COLS),
        in_specs=[
            pl.BlockSpec((MM_ROWS, xb.shape[1]), lambda i, j: (i, 0)),
            pl.BlockSpec((xb.shape[1], MM_COLS), lambda i, j: (0, j)),
        ],
        out_specs=pl.BlockSpec((MM_ROWS, MM_COLS), lambda i, j: (i, j)),
        out_shape=jax.ShapeDtypeStruct((s, n), out_dtype),
        compiler_params=_params(("parallel", "parallel")),
        name="matmul",
    )(xb, w)


def _split3(v):
    hi = v.astype(BF16)
    r1 = v - hi.astype(F32)
    mid = r1.astype(BF16)
    lo = (r1 - mid.astype(F32)).astype(BF16)
    return hi, mid, lo


def _cumsum_kernel(f_ref, bf_ref, c_ref, carry_ref):
    @pl.when(pl.program_id(0) == 0)
    def _():
        carry_ref[...] = jnp.zeros_like(carry_ref)

    z = f_ref[...] + bf_ref[...]
    log_f = jnp.minimum(z, 0.0) - jnp.log1p(jnp.exp(-jnp.abs(z)))
    n = z.shape[0]
    tri = (lax.broadcasted_iota(jnp.int32, (n, n), 0) >= lax.broadcasted_iota(jnp.int32, (n, n), 1)).astype(BF16)
    hi, mid, lo = _split3(log_f)
    c = _dot(tri, hi) + _dot(tri, mid) + _dot(tri, lo) + carry_ref[0:1, :]
    c_ref[...] = c
    carry_ref[...] = jnp.broadcast_to(c[n - 1:n, :], carry_ref.shape)


def _forget_cumsum(f_logit, b_f_row):
    s = f_logit.shape[0]
    return pl.pallas_call(
        _cumsum_kernel,
        grid=(s // CUMSUM_ROWS,),
        in_specs=[
            pl.BlockSpec((CUMSUM_ROWS, V7X_LANES), lambda i: (i, 0)),
            pl.BlockSpec((1, V7X_LANES), lambda i: (0, 0)),
        ],
        out_specs=pl.BlockSpec((CUMSUM_ROWS, V7X_LANES), lambda i: (i, 0)),
        out_shape=jax.ShapeDtypeStruct((s, V7X_LANES), F32),
        scratch_shapes=[pltpu.VMEM((V7X_SUBLANES, V7X_LANES), F32)],
        compiler_params=_params(("arbitrary",)),
        name="forget_cumsum",
    )(f_logit, b_f_row)


def _fox_prep_kernel(qkv_ref, c_ref, qt_ref, ka_ref, vt_ref):
    rows = qkv_ref.shape[0]
    lane = lax.broadcasted_iota(jnp.int32, (rows, V7X_LANES), 1)
    bias = c_ref[...] * (-LOG2E)
    ones_rows = (lax.broadcasted_iota(jnp.int32, (FOX_AUG, rows), 0) < 3).astype(BF16)
    sum_rows = (lax.broadcasted_iota(jnp.int32, (FOX_VROWS - HEAD_DIM, rows), 0) < 1).astype(BF16)
    for h in range(N_FOX_HEADS):
        hs = slice(h * HEAD_DIM, (h + 1) * HEAD_DIM)
        a0 = h * (HEAD_DIM + FOX_AUG)
        qt_ref[a0:a0 + HEAD_DIM, :] = qkv_ref[:, hs].astype(F32).T.astype(BF16)
        qt_ref[a0 + HEAD_DIM:a0 + HEAD_DIM + FOX_AUG, :] = ones_rows
        v0 = h * FOX_VROWS
        vt_ref[v0:v0 + HEAD_DIM, :] = (
            qkv_ref[:, 2 * FOX_W + h * HEAD_DIM:2 * FOX_W + (h + 1) * HEAD_DIM].astype(F32).T.astype(BF16))
        vt_ref[v0 + HEAD_DIM:v0 + FOX_VROWS, :] = sum_rows
        hi, mid, lo = _split3(jnp.broadcast_to(bias[:, h:h + 1], (rows, V7X_LANES)))
        aug = jnp.where(lane == 0, hi.astype(F32), jnp.where(lane == 1, mid.astype(F32),
                                                              jnp.where(lane == 2, lo.astype(F32), 0.0)))
        ka_ref[:, a0:a0 + HEAD_DIM] = qkv_ref[:, FOX_W + h * HEAD_DIM:FOX_W + (h + 1) * HEAD_DIM]
        ka_ref[:, a0 + HEAD_DIM:a0 + HEAD_DIM + FOX_AUG] = aug.astype(BF16)


def _fox_prep(qkv, c_rows):
    s = qkv.shape[0]
    wide = N_FOX_HEADS * (HEAD_DIM + FOX_AUG)
    vrows = N_FOX_HEADS * FOX_VROWS
    return pl.pallas_call(
        _fox_prep_kernel,
        grid=(s // FOX_BLOCK,),
        in_specs=[pl.BlockSpec((FOX_BLOCK, 3 * FOX_W), lambda i: (i, 0)),
                  pl.BlockSpec((FOX_BLOCK, V7X_LANES), lambda i: (i, 0))],
        out_specs=[pl.BlockSpec((wide, FOX_BLOCK), lambda i: (0, i)),
                   pl.BlockSpec((FOX_BLOCK, wide), lambda i: (i, 0)),
                   pl.BlockSpec((vrows, FOX_BLOCK), lambda i: (0, i))],
        out_shape=[jax.ShapeDtypeStruct((wide, s), BF16), jax.ShapeDtypeStruct((s, wide), BF16),
                   jax.ShapeDtypeStruct((vrows, s), BF16)],
        compiler_params=_params(("parallel",)),
        name="fox_prep",
    )(qkv, c_rows)


def _fox_kernel(qi_ref, kj_ref, qt_ref, ka_ref, vt_ref, o_ref, m_ref, acc_ref, s_ref):
    t = pl.program_id(0)
    i = qi_ref[t]
    j = kj_ref[t]
    blk = ka_ref.shape[0]
    aw = HEAD_DIM + FOX_AUG

    @pl.when(j == 0)
    def _():
        m_ref[...] = jnp.full_like(m_ref, NEG)
        acc_ref[...] = jnp.zeros_like(acc_ref)

    def scores(h):
        return _dot(ka_ref[:, h * aw:(h + 1) * aw], qt_ref[h * aw:(h + 1) * aw, :])

    def step(diagonal):
        if diagonal:
            keep = lax.broadcasted_iota(jnp.int32, (blk, blk), 0) <= lax.broadcasted_iota(jnp.int32, (blk, blk), 1)
        s_ref[0] = scores(0)
        for h in range(N_FOX_HEADS):
            if h + 1 < N_FOX_HEADS:
                s_ref[(h + 1) % 2] = scores(h + 1)
            s = s_ref[h % 2]
            if diagonal:
                s = jnp.where(keep, s, NEG)
            m_prev = m_ref[h:h + 1, :]
            m_new = jnp.maximum(m_prev, jnp.max(s, axis=0, keepdims=True))
            p = jnp.exp2(s - m_new).astype(BF16)
            alpha = jnp.exp2(m_prev - m_new)
            m_ref[h:h + 1, :] = m_new
            vs = slice(h * FOX_VROWS, (h + 1) * FOX_VROWS)
            acc_ref[vs, :] = alpha * acc_ref[vs, :] + _dot(vt_ref[vs, :], p)

    @pl.when(j < i)
    def _():
        step(False)

    @pl.when(j == i)
    def _():
        step(True)
        for h in range(N_FOX_HEADS):
            v0 = h * FOX_VROWS
            out_t = acc_ref[v0:v0 + HEAD_DIM, :] / acc_ref[v0 + HEAD_DIM:v0 + HEAD_DIM + 1, :]
            o_ref[:, h * HEAD_DIM:(h + 1) * HEAD_DIM] = out_t.T.astype(o_ref.dtype)


def _fox_attention(qt, ka, vt):
    s = ka.shape[0]
    nb = s // FOX_BLOCK
    qi = jnp.asarray([i for i in range(nb) for _ in range(i + 1)], jnp.int32)
    kj = jnp.asarray([j for i in range(nb) for j in range(i + 1)], jnp.int32)
    wide = ka.shape[1]
    vrows = vt.shape[0]
    grid_spec = pltpu.PrefetchScalarGridSpec(
        num_scalar_prefetch=2,
        grid=(qi.shape[0],),
        in_specs=[
            pl.BlockSpec((wide, FOX_BLOCK), lambda t, qi, kj: (0, qi[t])),
            pl.BlockSpec((FOX_BLOCK, wide), lambda t, qi, kj: (kj[t], 0)),
            pl.BlockSpec((vrows, FOX_BLOCK), lambda t, qi, kj: (0, kj[t])),
        ],
        out_specs=pl.BlockSpec((FOX_BLOCK, FOX_W), lambda t, qi, kj: (qi[t], 0)),
        scratch_shapes=[
            pltpu.VMEM((N_FOX_HEADS, FOX_BLOCK), F32),
            pltpu.VMEM((vrows, FOX_BLOCK), F32),
            pltpu.VMEM((2, FOX_BLOCK, FOX_BLOCK), F32),
        ],
    )
    return pl.pallas_call(
        _fox_kernel,
        grid_spec=grid_spec,
        out_shape=jax.ShapeDtypeStruct((s, FOX_W), BF16),
        compiler_params=_params(("arbitrary",)),
        name="fox_attention",
    )(qi, kj, qt, ka, vt)


def _dil_kernel(q_ref, k_ref, v_ref, kp_ref, vp_ref, y_ref, o_scr, l_scr):
    has_prev_tile = pl.program_id(0) > 0
    b = DIL_BLOCK
    tile = q_ref.shape[0]
    jr = lax.broadcasted_iota(jnp.int32, (b, 2 * b), 0)
    mc = lax.broadcasted_iota(jnp.int32, (b, 2 * b), 1)
    band = (mc >= jr) & (mc <= jr + b)
    band_first = band & ((mc >= b) | has_prev_tile)

    def rows(ref, start, d):
        return ref[pl.ds(start, b), :] if d == 1 else ref[pl.ds(start, b, stride=d), :]

    for g, (_, d) in enumerate(DIL_PAIRS):
        for r in range(d):
            first = tile - b * d + r
            k_blk = [rows(kp_ref, first, d).astype(BF16)]
            v_blk = [rows(vp_ref, first, d).astype(BF16)]
            for n in range(tile // (b * d)):
                cur = r + n * b * d
                k_blk.append(rows(k_ref, cur, d).astype(BF16))
                v_blk.append(rows(v_ref, cur, d).astype(BF16))
                k2 = jnp.concatenate(k_blk[n:n + 2], axis=0)
                v2 = jnp.concatenate(v_blk[n:n + 2], axis=0)
                keep = band if n > 0 else band_first
                s = jnp.where(keep, _dot_nt(rows(q_ref, cur, d).astype(BF16), k2), NEG)
                m = jnp.max(s, axis=1, keepdims=True)
                p = jnp.exp(s - m)
                l = jnp.sum(p, axis=1, keepdims=True)
                o = _dot(p.astype(BF16), v2) / l
                lse = jnp.broadcast_to(m + jnp.log(l), (b, HEAD_DIM))
                if d == 1:
                    o_scr[g, pl.ds(cur, b), :] = o
                    l_scr[g, pl.ds(cur, b), :] = lse
                else:
                    o_scr[g, pl.ds(cur, b, stride=d), :] = o
                    l_scr[g, pl.ds(cur, b, stride=d), :] = lse

    l1, l2, l3 = l_scr[0], l_scr[1], l_scr[2]
    mx = jnp.maximum(jnp.maximum(l1, l2), l3)
    e1, e2, e3 = jnp.exp(l1 - mx), jnp.exp(l2 - mx), jnp.exp(l3 - mx)
    den = e1 + e2 + e3
    y_ref[...] = ((e1 / den) * o_scr[0] + (e2 / den) * o_scr[1] + (e3 / den) * o_scr[2]).astype(y_ref.dtype)


def _dilated_attention(qkv):
    s = qkv.shape[0]
    cur = lambda c: pl.BlockSpec((DIL_TILE, HEAD_DIM), lambda t, h: (t, c * N_DIL_HEADS + h))
    prev = lambda c: pl.BlockSpec((DIL_TILE, HEAD_DIM), lambda t, h: (jnp.maximum(t - 1, 0), c * N_DIL_HEADS + h))
    return pl.pallas_call(
        _dil_kernel,
        grid=(s // DIL_TILE, N_DIL_HEADS),
        in_specs=[cur(0), cur(1), cur(2), prev(1), prev(2)],
        out_specs=pl.BlockSpec((DIL_TILE, HEAD_DIM), lambda t, h: (t, h)),
        out_shape=jax.ShapeDtypeStruct((s, DIL_W), BF16),
        scratch_shapes=[pltpu.VMEM((len(DIL_PAIRS), DIL_TILE, HEAD_DIM), F32),
                        pltpu.VMEM((len(DIL_PAIRS), DIL_TILE, HEAD_DIM), F32)],
        compiler_params=_params(("parallel", "parallel")),
        name="dilated_attention",
    )(qkv, qkv, qkv, qkv, qkv)


def _attn_out_kernel(x_ref, ya_ref, yb_ref, w_ref, g_ref, b_ref, o_ref, ob_ref):
    half = x_ref.shape[0] // 2
    for r in range(2):
        rows = slice(r * half, (r + 1) * half)
        y = _dot(jnp.concatenate([ya_ref[rows, :], yb_ref[rows, :]], axis=1), w_ref[...])
        out = _layer_norm(ALPHA * x_ref[rows, :] + y, g_ref[...], b_ref[...])
        o_ref[rows, :] = out
        ob_ref[rows, :] = out.astype(BF16)


def _attn_out(x, ya, yb, w, g, b):
    s = x.shape[0]
    row = lambda width: pl.BlockSpec((OUT_ROWS, width), lambda i: (i, 0))
    const = lambda shape: pl.BlockSpec(shape, lambda i: (0, 0))
    return pl.pallas_call(
        _attn_out_kernel,
        grid=(s // OUT_ROWS,),
        in_specs=[row(D_MODEL), row(FOX_W), row(DIL_W),
                  const((FOX_W + DIL_W, D_MODEL)), const((1, D_MODEL)), const((1, D_MODEL))],
        out_specs=[row(D_MODEL), row(D_MODEL)],
        out_shape=[jax.ShapeDtypeStruct((s, D_MODEL), F32), jax.ShapeDtypeStruct((s, D_MODEL), BF16)],
        compiler_params=_params(("parallel",)),
        name="attn_out",
    )(x, ya, yb, w, g, b)


def _cmul(ar, ai, br, bi):
    return ar * br - ai * bi, ar * bi + ai * br


def _s5_prep_kernel(lrg_ref, lig_ref, ldtg_ref, lr_ref, li_ref, ldt_ref, br_ref, bi_ref,
                    pr_ref, pi_ref, qr_ref, qi_ref, bbr_ref, bbi_ref):
    def a_bar(lr, li, ldt):
        dt = jnp.exp(ldt)
        mag = jnp.exp(lr * dt)
        return mag * jnp.cos(li * dt), mag * jnp.sin(li * dt)

    lr, li = lr_ref[...], li_ref[...]
    ar, ai = a_bar(lr, li, ldt_ref[...])
    den = lr * lr + li * li
    cr = ((ar - 1.0) * lr + ai * li) / den
    ci = (ai * lr - (ar - 1.0) * li) / den
    bbr_ref[...], bbi_ref[...] = _cmul(cr, ci, br_ref[...], bi_ref[...])

    ar, ai = a_bar(lrg_ref[...], lig_ref[...], ldtg_ref[...])
    pr, pi = ar, ai
    pr_ref[0], pi_ref[0] = pr, pi
    for k in range(1, S5_SEG):
        pr, pi = _cmul(pr, pi, ar, ai)
        pr_ref[k], pi_ref[k] = pr, pi
    sr, si = pr, pi
    qr_ref[0], qi_ref[0] = sr, si
    for k in range(1, V7X_SUBLANES):
        pr, pi = _cmul(pr, pi, sr, si)
        qr_ref[k], qi_ref[k] = pr, pi


def _s5_prep(lam_re, lam_im, log_dt, b_re, b_im):
    ldt = jnp.broadcast_to(log_dt[:, None], lam_re.shape)
    rep = lambda v: jnp.repeat(v, S5_GROUP, axis=0)
    rows = S5_GROUPS * S5_GROUP
    bt = lambda v: v.transpose(0, 2, 1).reshape(rows, S5_STATE)
    mat = jax.ShapeDtypeStruct((rows, S5_STATE), F32)
    pw = jax.ShapeDtypeStruct((S5_SEG, S5_GROUPS, S5_STATE), F32)
    qw = jax.ShapeDtypeStruct((V7X_SUBLANES, S5_GROUPS, S5_STATE), F32)
    return pl.pallas_call(_s5_prep_kernel, out_shape=[pw, pw, qw, qw, mat, mat], name="s5_prep")(
        lam_re, lam_im, ldt, rep(lam_re), rep(lam_im), rep(ldt), bt(b_re), bt(b_im))


def _s5_core_kernel(u_ref, bblk_ref, cblk_ref, pr_ref, pi_ref, qr_ref, qi_ref, d_ref, z_ref,
                    up_ref, sr_ref, si_ref, cr_ref, ci_ref):
    @pl.when(pl.program_id(0) == 0)
    def _():
        cr_ref[...] = jnp.zeros_like(cr_ref)
        ci_ref[...] = jnp.zeros_like(ci_ref)

    sl = V7X_SUBLANES
    rows = u_ref.shape[0]
    ri = lax.broadcasted_iota(jnp.int32, (rows, rows), 0)
    ci = lax.broadcasted_iota(jnp.int32, (rows, rows), 1)
    perm = ((ri // sl == ci % S5_SEG) & (ri % sl == ci // S5_SEG)).astype(BF16)
    unperm = ((ci // sl == ri % S5_SEG) & (ci % sl == ri // S5_SEG)).astype(BF16)
    hi, mid, lo = _split3(u_ref[...])
    up_ref[...] = _dot(perm, hi) + _dot(perm, mid) + _dot(perm, lo)

    sub = lax.broadcasted_iota(jnp.int32, (sl, S5_LANES), 0)
    width = S5_GB * S5_GROUP
    ys = []
    for jb in range(S5_NB):
        bu = _dot(up_ref[:, jb * width:(jb + 1) * width].astype(BF16), bblk_ref[jb])
        a_re, a_im = pr_ref[jb, 0:1, :], pi_ref[jb, 0:1, :]
        xr, xi = bu[0:sl, :S5_LANES], bu[0:sl, S5_LANES:]
        sr_ref[0:sl, :], si_ref[0:sl, :] = xr, xi
        for j in range(1, S5_SEG):
            tr, ti = _cmul(a_re, a_im, xr, xi)
            xr, xi = tr + bu[j * sl:(j + 1) * sl, :S5_LANES], ti + bu[j * sl:(j + 1) * sl, S5_LANES:]
            sr_ref[j * sl:(j + 1) * sl, :], si_ref[j * sl:(j + 1) * sl, :] = xr, xi

        q_re, q_im = qr_ref[jb], qi_ref[jb]
        er, ei = xr, xi
        for k in (1, 2, 4):
            tr = jnp.where(sub >= k, pltpu.roll(er, k, 0), 0.0)
            ti = jnp.where(sub >= k, pltpu.roll(ei, k, 0), 0.0)
            mr, mi = _cmul(q_re[k - 1:k, :], q_im[k - 1:k, :], tr, ti)
            er, ei = er + mr, ei + mi
        in_re, in_im = cr_ref[jb:jb + 1, :], ci_ref[jb:jb + 1, :]
        mr, mi = _cmul(q_re, q_im, in_re, in_im)
        er, ei = er + mr, ei + mi
        cr_ref[jb:jb + 1, :], ci_ref[jb:jb + 1, :] = er[sl - 1:, :], ei[sl - 1:, :]
        cin_re = jnp.where(sub >= 1, pltpu.roll(er, 1, 0), in_re)
        cin_im = jnp.where(sub >= 1, pltpu.roll(ei, 1, 0), in_im)
        fr, fi = _cmul(pr_ref[jb][:, None, :], pi_ref[jb][:, None, :], cin_re[None], cin_im[None])
        shape3 = (S5_SEG, sl, S5_LANES)
        state = jnp.concatenate([(sr_ref[...].reshape(shape3) + fr).reshape(-1, S5_LANES),
                                 (si_ref[...].reshape(shape3) + fi).reshape(-1, S5_LANES)], axis=1)
        ys.append(_dot(state.astype(BF16), cblk_ref[jb]))
    y = jnp.concatenate(ys, axis=1) + d_ref[...] * up_ref[...]
    z = 0.5 * y * (1.0 + jnp.tanh(math.sqrt(2.0 / math.pi) * (y + 0.044715 * (y * y * y))))
    z_ref[...] = _dot(unperm, z.astype(BF16)).astype(z_ref.dtype)


def _s5_core(u, bblk, cblk, p_re, p_im, q_re, q_im, d_row):
    s = u.shape[0]
    const = lambda shape: pl.BlockSpec(shape, lambda t: (0,) * len(shape))
    return pl.pallas_call(
        _s5_core_kernel,
        grid=(s // S5_ROWS,),
        in_specs=[pl.BlockSpec((S5_ROWS, S5_WIDTH), lambda t: (t, 0)),
                  const(bblk.shape), const(cblk.shape), const(p_re.shape), const(p_im.shape),
                  const(q_re.shape), const(q_im.shape), const((1, S5_WIDTH))],
        out_specs=pl.BlockSpec((S5_ROWS, S5_WIDTH), lambda t: (t, 0)),
        out_shape=jax.ShapeDtypeStruct((s, S5_WIDTH), BF16),
        scratch_shapes=[pltpu.VMEM((S5_ROWS, S5_WIDTH), F32),
                        pltpu.VMEM((S5_ROWS, S5_LANES), F32), pltpu.VMEM((S5_ROWS, S5_LANES), F32),
                        pltpu.VMEM((V7X_SUBLANES, S5_LANES), F32), pltpu.VMEM((V7X_SUBLANES, S5_LANES), F32)],
        compiler_params=_params(("arbitrary",)),
        name="s5_core",
    )(u, bblk, cblk, p_re, p_im, q_re, q_im, d_row)


def _block_diag(blocks):
    nb, gb, r, c = blocks.shape
    eye = jnp.eye(gb, dtype=blocks.dtype)
    return jnp.einsum("ngrc,gk->ngrkc", blocks, eye).reshape(nb, gb * r, gb * c)


def _glu_kernel(x_ref, z_ref, wo_ref, wg_ref, g_ref, b_ref, o_ref, ob_ref):
    half = z_ref.shape[0] // 2
    for r in range(2):
        rows = slice(r * half, (r + 1) * half)
        z = z_ref[rows, :]
        y = _dot(z, wo_ref[...]) * jax.nn.sigmoid(_dot(z, wg_ref[...]))
        out = _layer_norm(ALPHA * x_ref[rows, :] + y, g_ref[...], b_ref[...])
        o_ref[rows, :] = out
        ob_ref[rows, :] = out.astype(BF16)


def _glu(x, z, wo, wg, g, b):
    s = x.shape[0]
    row = lambda width: pl.BlockSpec((GLU_ROWS, width), lambda i: (i, 0))
    const = lambda shape: pl.BlockSpec(shape, lambda i: (0, 0))
    return pl.pallas_call(
        _glu_kernel,
        grid=(s // GLU_ROWS,),
        in_specs=[row(D_MODEL), row(S5_WIDTH), const((S5_WIDTH, D_MODEL)), const((S5_WIDTH, D_MODEL)),
                  const((1, D_MODEL)), const((1, D_MODEL))],
        out_specs=[row(D_MODEL), row(D_MODEL)],
        out_shape=[jax.ShapeDtypeStruct((s, D_MODEL), F32), jax.ShapeDtypeStruct((s, D_MODEL), BF16)],
        compiler_params=_params(("parallel",)),
        name="glu",
    )(x, z, wo, wg, g, b)


def _attn_weights_kernel(w_ref, fox_ref, f_ref, dil_ref):
    scale = HEAD_DIM ** -0.5
    w = w_ref[...]
    d0 = 3 * FOX_W + N_FOX_HEADS
    fox_ref[:, 0:FOX_W] = (w[:, 0:FOX_W] * (scale * LOG2E)).astype(BF16)
    fox_ref[:, FOX_W:3 * FOX_W] = w[:, FOX_W:3 * FOX_W].astype(BF16)
    f_ref[...] = jnp.zeros_like(f_ref)
    f_ref[:, 0:N_FOX_HEADS] = w[:, 3 * FOX_W:d0].astype(BF16)
    dil_ref[:, 0:DIL_W] = (w[:, d0:d0 + DIL_W] * scale).astype(BF16)
    dil_ref[:, DIL_W:3 * DIL_W] = w[:, d0 + DIL_W:d0 + 3 * DIL_W].astype(BF16)


def _attn_weights(w_in):
    rows = D_MODEL // CAST_STEPS
    out = lambda width: pl.BlockSpec((rows, width), lambda r: (r, 0))
    return pl.pallas_call(
        _attn_weights_kernel,
        grid=(CAST_STEPS,),
        in_specs=[pl.BlockSpec((rows, w_in.shape[1]), lambda r: (r, 0))],
        out_specs=[out(3 * FOX_W), out(V7X_LANES), out(3 * DIL_W)],
        out_shape=[jax.ShapeDtypeStruct((D_MODEL, 3 * FOX_W), BF16), jax.ShapeDtypeStruct((D_MODEL, V7X_LANES), BF16),
                   jax.ShapeDtypeStruct((D_MODEL, 3 * DIL_W), BF16)],
        compiler_params=_params(("parallel",)),
        name="attn_weights",
    )(w_in)


def _attention_mixer(x, xb, w_in, b_f, w_out, g, b):
    w_fox, w_f, w_dil = _attn_weights(w_in)
    qkv_fox, f_logit = _attn_in(xb, w_fox, w_f)
    c_rows = _forget_cumsum(f_logit, jnp.pad(b_f, (0, V7X_LANES - N_FOX_HEADS))[None, :])
    ya = _fox_attention(*_fox_prep(qkv_fox, c_rows))
    yb = _dilated_attention(_matmul(xb, w_dil, F32))
    return _attn_out(x, ya, yb, w_out.astype(BF16), g, b)


def _s5_mixer(x, xb, w_in, lam_re, lam_im, log_dt, b_re, b_im, c_re, c_im, d_skip, w_glu_out, w_glu_gate, g, b):
    u = _matmul(xb, w_in.astype(BF16), F32)
    p_re, p_im, q_re, q_im, bb_re, bb_im = _s5_prep(lam_re, lam_im, log_dt, b_re, b_im)
    lanes = lambda p: p.reshape(p.shape[0], S5_NB, S5_LANES).transpose(1, 0, 2)
    blk = lambda v: v.reshape(S5_NB, S5_GB, S5_GROUP, S5_STATE)
    bblk = jnp.concatenate([_block_diag(blk(bb_re)), _block_diag(blk(bb_im))], axis=2).astype(BF16)
    ct = lambda v: v.transpose(0, 2, 1).reshape(S5_NB, S5_GB, S5_STATE, S5_GROUP)
    cblk = jnp.concatenate([_block_diag(ct(c_re)), -_block_diag(ct(c_im))], axis=1).astype(BF16)
    z = _s5_core(u, bblk, cblk, lanes(p_re), lanes(p_im), lanes(q_re), lanes(q_im), d_skip[None, :])
    return _glu(x, z, w_glu_out.astype(BF16), w_glu_gate.astype(BF16), g, b)


def kernel(x, ffn1_w_gate, ffn1_w_up, ffn1_w_down, ffn2_w_gate, ffn2_w_up, ffn2_w_down, ln_gain, ln_bias,
           attn_w_in, attn_b_f, attn_w_out, s5_w_in, s5_lambda_re, s5_lambda_im, s5_log_dt, s5_b_re, s5_b_im,
           s5_c_re, s5_c_im, s5_d, s5_w_glu_out, s5_w_glu_gate):
    batch, seq, _ = x.shape
    assert batch == 1
    h = x.reshape(seq, D_MODEL)
    hb = None
    ln = lambda i, k: (ln_gain[i, k][None, :], ln_bias[i, k][None, :])
    for i in range(DEPTH):
        h, hb = _ffn(h, hb, *_ffn_weights_bf16(ffn1_w_gate, ffn1_w_up, ffn1_w_down, i), *ln(i, 0))
        j = i // 2
        if i % 2 == 0:
            h, hb = _attention_mixer(h, hb, attn_w_in[j], attn_b_f[j], attn_w_out[j], *ln(i, 1))
        else:
            h, hb = _s5_mixer(h, hb, s5_w_in[j], s5_lambda_re[j], s5_lambda_im[j], s5_log_dt[j], s5_b_re[j],
                              s5_b_im[j], s5_c_re[j], s5_c_im[j], s5_d[j], s5_w_glu_out[j], s5_w_glu_gate[j],
                              *ln(i, 1))
        h, hb = _ffn(h, hb, *_ffn_weights_bf16(ffn2_w_gate, ffn2_w_up, ffn2_w_down, i), *ln(i, 2))
    return h.reshape(batch, seq, D_MODEL)
```

```python
import functools
import math

import jax
import jax.numpy as jnp
from jax import lax
from jax.experimental import pallas as pl
from jax.experimental.pallas import tpu as pltpu

D_MODEL = 2048
DEPTH = 2
HEAD_DIM = 128
N_FOX_HEADS = 8
N_DIL_HEADS = 8
DIL_PAIRS = ((128, 1), (512, 4), (2048, 16))
DIL_BLOCK = 128
D_FF = 5632
S5_GROUP = 16
S5_WIDTH = 1024
S5_GROUPS = S5_WIDTH // S5_GROUP
S5_STATE = 64
ALPHA = (2.0 * DEPTH) ** 0.25
LN_EPS = 1e-5
FOX_W = N_FOX_HEADS * HEAD_DIM
DIL_W = N_DIL_HEADS * HEAD_DIM

F32 = jnp.float32
BF16 = jnp.bfloat16
NEG = -1e30
LOG2E = math.log2(math.e)
FOX_AUG = 128
FOX_VROWS = HEAD_DIM + 16

V7X_LANES = 128
V7X_SUBLANES = 8
V7X_VMEM_LIMIT_BYTES = 56 * 1024 * 1024

FFN_ROWS = 512
FFN_COLS = 512
FFN_CHUNKS = D_FF // FFN_COLS
FFN_GROUP = 256
FFN_RING_SLOTS = 3
assert (FFN_CHUNKS + 1) % FFN_RING_SLOTS == 0
MM_ROWS = 2048
MM_COLS = 1024
FOX_BLOCK = 1024
DIL_TILE = 2048
OUT_ROWS = 512
GLU_ROWS = 512
CUMSUM_ROWS = 512
CAST_STEPS = 8
S5_ROWS = 256
S5_SEG = S5_ROWS // V7X_SUBLANES
S5_GB = 16
S5_NB = S5_GROUPS // S5_GB
S5_LANES = S5_GB * S5_STATE


def _params(sem):
    return pltpu.CompilerParams(dimension_semantics=sem, vmem_limit_bytes=V7X_VMEM_LIMIT_BYTES)


def _layer_norm(v, g, b):
    mu = jnp.mean(v, axis=-1, keepdims=True)
    d = v - mu
    var = jnp.mean(d * d, axis=-1, keepdims=True)
    return d * lax.rsqrt(var + LN_EPS) * g + b


def _dot(a, b):
    return jnp.dot(a, b, preferred_element_type=F32)


def _dot_nt(a, b):
    return lax.dot_general(a, b, (((1,), (1,)), ((), ())), preferred_element_type=F32)


def _ffn_kernel(*refs, cast_input):
    if cast_input:
        x_ref, wgu_hbm, wd_hbm, g_ref, b_ref, o_ref, ob_ref, h_ref, wgu_buf, wd_buf, sem, xb_ref = refs
        xb_ref[...] = x_ref[...].astype(BF16)
    else:
        x_ref, xb_ref, wgu_hbm, wd_hbm, g_ref, b_ref, o_ref, ob_ref, h_ref, wgu_buf, wd_buf, sem = refs
    i = pl.program_id(0)
    last_tile = pl.num_programs(0) - 1
    chunks = FFN_CHUNKS
    nsub = chunks + 1

    def ring(sub, slot, op):
        @pl.when(sub < chunks)
        def _():
            c = jnp.minimum(sub, chunks - 1)
            op(pltpu.make_async_copy(wgu_hbm.at[c], wgu_buf.at[slot], sem.at[0, slot]))

        @pl.when(sub >= 1)
        def _():
            c = jnp.maximum(sub - 1, 0)
            op(pltpu.make_async_copy(wd_hbm.at[c], wd_buf.at[slot], sem.at[1, slot]))

    start = lambda cp: cp.start()
    wait = lambda cp: cp.wait()

    @pl.when(i == 0)
    def _():
        ring(jnp.int32(0), 0, start)
        ring(jnp.int32(1), 1, start)

    def fetch_ahead_and_wait(j):
        ahead = j + 2
        wraps = ahead >= nsub

        @pl.when(jnp.logical_or(jnp.logical_not(wraps), i < last_tile))
        def _():
            ring(jnp.where(wraps, ahead - nsub, ahead), ahead % FFN_RING_SLOTS, start)

        ring(j, j % FFN_RING_SLOTS, wait)

    def hidden(slot):
        gu = _dot(xb_ref[...], wgu_buf[slot])
        parts = []
        for n in range(FFN_COLS // FFN_GROUP):
            gate = gu[:, 2 * n * FFN_GROUP:(2 * n + 1) * FFN_GROUP]
            up = gu[:, (2 * n + 1) * FFN_GROUP:(2 * n + 2) * FFN_GROUP]
            parts.append((gate * jax.nn.sigmoid(gate) * up).astype(BF16))
        return jnp.concatenate(parts, axis=1)

    def down(j, slot):
        return _dot(h_ref[(j - 1) % 2], wd_buf[slot])

    fetch_ahead_and_wait(jnp.int32(0))
    h_ref[0] = hidden(0)

    fetch_ahead_and_wait(jnp.int32(1))
    o_ref[...] = down(1, 1)
    h_ref[1] = hidden(1)

    def body(j, carry):
        fetch_ahead_and_wait(j)
        slot = j % FFN_RING_SLOTS
        o_ref[...] += down(j, slot)
        h_ref[j % 2] = hidden(slot)
        return carry

    lax.fori_loop(2, chunks, body, 0)

    fetch_ahead_and_wait(jnp.int32(chunks))
    acc = o_ref[...] + down(chunks, chunks % FFN_RING_SLOTS)
    y = _layer_norm(ALPHA * x_ref[...] + 0.5 * acc, g_ref[...], b_ref[...])
    o_ref[...] = y
    ob_ref[...] = y.astype(BF16)


def _ffn(x, xb, wgu, wd, g, b):
    s = x.shape[0]
    row = pl.BlockSpec((FFN_ROWS, D_MODEL), lambda i: (i, 0))
    hbm = pl.BlockSpec(memory_space=pl.ANY)
    vec = pl.BlockSpec((1, D_MODEL), lambda i: (0, 0))
    scratch = [
        pltpu.VMEM((2, FFN_ROWS, FFN_COLS), BF16),
        pltpu.VMEM((FFN_RING_SLOTS, D_MODEL, 2 * FFN_COLS), BF16),
        pltpu.VMEM((FFN_RING_SLOTS, FFN_COLS, D_MODEL), BF16),
        pltpu.SemaphoreType.DMA((2, FFN_RING_SLOTS)),
    ]
    cast_input = xb is None
    return pl.pallas_call(
        functools.partial(_ffn_kernel, cast_input=cast_input),
        grid=(s // FFN_ROWS,),
        in_specs=[row] + ([] if cast_input else [row]) + [hbm, hbm, vec, vec],
        out_specs=[row, row],
        out_shape=[jax.ShapeDtypeStruct((s, D_MODEL), F32), jax.ShapeDtypeStruct((s, D_MODEL), BF16)],
        scratch_shapes=scratch + ([pltpu.VMEM((FFN_ROWS, D_MODEL), BF16)] if cast_input else []),
        compiler_params=_params(("arbitrary",)),
        name="ffn",
    )(*([x] if cast_input else [x, xb]), wgu, wd, g, b)


def _cast_kernel(wg_ref, wu_ref, wd_ref, ogu_ref, od_ref):
    for n in range(D_FF // FFN_GROUP):
        c, k = divmod(n, FFN_COLS // FFN_GROUP)
        src = slice(n * FFN_GROUP, (n + 1) * FFN_GROUP)
        ogu_ref[c, :, 2 * k * FFN_GROUP:(2 * k + 1) * FFN_GROUP] = wg_ref[:, src].astype(BF16)
        ogu_ref[c, :, (2 * k + 1) * FFN_GROUP:(2 * k + 2) * FFN_GROUP] = wu_ref[:, src].astype(BF16)
    od_ref[...] = wd_ref[...].astype(BF16)


def _ffn_weights_bf16(w_gate, w_up, w_down, layer):
    steps = CAST_STEPS
    up_rows, down_rows = D_MODEL // steps, D_FF // steps
    up_in = pl.BlockSpec((None, up_rows, D_FF), lambda r: (layer, r, 0))
    up_out = pl.BlockSpec((FFN_CHUNKS, up_rows, 2 * FFN_COLS), lambda r: (0, r, 0))
    up_shape = jax.ShapeDtypeStruct((FFN_CHUNKS, D_MODEL, 2 * FFN_COLS), BF16)
    wgu, wd = pl.pallas_call(
        _cast_kernel,
        grid=(steps,),
        in_specs=[up_in, up_in, pl.BlockSpec((None, down_rows, D_MODEL), lambda r: (layer, r, 0))],
        out_specs=[up_out, pl.BlockSpec((down_rows, D_MODEL), lambda r: (r, 0))],
        out_shape=[up_shape, jax.ShapeDtypeStruct((D_FF, D_MODEL), BF16)],
        compiler_params=_params(("parallel",)),
        name="ffn_weights_bf16",
    )(w_gate, w_up, w_down)
    return wgu, wd.reshape(FFN_CHUNKS, FFN_COLS, D_MODEL)


def _attn_in_kernel(x_ref, w_ref, wf_ref, o_ref, f_ref):
    x = x_ref[...]
    o_ref[...] = _dot(x, w_ref[...]).astype(o_ref.dtype)

    @pl.when(pl.program_id(1) == 0)
    def _():
        f_ref[...] = _dot(x, wf_ref[...])


def _attn_in(xb, w, wf):
    s, n = xb.shape[0], w.shape[1]
    return pl.pallas_call(
        _attn_in_kernel,
        grid=(s // MM_ROWS, n // MM_COLS),
        in_specs=[
            pl.BlockSpec((MM_ROWS, D_MODEL), lambda i, j: (i, 0)),
            pl.BlockSpec((D_MODEL, MM_COLS), lambda i, j: (0, j)),
            pl.BlockSpec((D_MODEL, V7X_LANES), lambda i, j: (0, 0)),
        ],
        out_specs=[
            pl.BlockSpec((MM_ROWS, MM_COLS), lambda i, j: (i, j)),
            pl.BlockSpec((MM_ROWS, V7X_LANES), lambda i, j: (i, 0)),
        ],
        out_shape=[jax.ShapeDtypeStruct((s, n), BF16), jax.ShapeDtypeStruct((s, V7X_LANES), F32)],
        compiler_params=_params(("parallel", "arbitrary")),
        name="attn_in",
    )(xb, w, wf)


def _mm_kernel(x_ref, w_ref, o_ref):
    o_ref[...] = _dot(x_ref[...], w_ref[...]).astype(o_ref.dtype)


def _matmul(xb, w, out_dtype):
    s, n = xb.shape[0], w.shape[1]
    return pl.pallas_call(
        _mm_kernel,
        grid=(s // MM_ROWS, n // MM_COLS),
        in_specs=[
            pl.BlockSpec((MM_ROWS, xb.shape[1]), lambda i, j: (i, 0)),
            pl.BlockSpec((xb.shape[1], MM_COLS), lambda i, j: (0, j)),
        ],
        out_specs=pl.BlockSpec((MM_ROWS, MM_COLS), lambda i, j: (i, j)),
        out_shape=jax.ShapeDtypeStruct((s, n), out_dtype),
        compiler_params=_params(("parallel", "parallel")),
        name="matmul",
    )(xb, w)


def _split3(v):
    hi = v.astype(BF16)
    r1 = v - hi.astype(F32)
    mid = r1.astype(BF16)
    lo = (r1 - mid.astype(F32)).astype(BF16)
    return hi, mid, lo


def _cumsum_kernel(f_ref, bf_ref, c_ref, carry_ref):
    @pl.when(pl.program_id(0) == 0)
    def _():
        carry_ref[...] = jnp.zeros_like(carry_ref)

    z = f_ref[...] + bf_ref[...]
    log_f = jnp.minimum(z, 0.0) - jnp.log1p(jnp.exp(-jnp.abs(z)))
    n = z.shape[0]
    tri = (lax.broadcasted_iota(jnp.int32, (n, n), 0) >= lax.broadcasted_iota(jnp.int32, (n, n), 1)).astype(BF16)
    hi, mid, lo = _split3(log_f)
    c = _dot(tri, hi) + _dot(tri, mid) + _dot(tri, lo) + carry_ref[0:1, :]
    c_ref[...] = c
    carry_ref[...] = jnp.broadcast_to(c[n - 1:n, :], carry_ref.shape)


def _forget_cumsum(f_logit, b_f_row):
    s = f_logit.shape[0]
    return pl.pallas_call(
        _cumsum_kernel,
        grid=(s // CUMSUM_ROWS,),
        in_specs=[
            pl.BlockSpec((CUMSUM_ROWS, V7X_LANES), lambda i: (i, 0)),
            pl.BlockSpec((1, V7X_LANES), lambda i: (0, 0)),
        ],
        out_specs=pl.BlockSpec((CUMSUM_ROWS, V7X_LANES), lambda i: (i, 0)),
        out_shape=jax.ShapeDtypeStruct((s, V7X_LANES), F32),
        scratch_shapes=[pltpu.VMEM((V7X_SUBLANES, V7X_LANES), F32)],
        compiler_params=_params(("arbitrary",)),
        name="forget_cumsum",
    )(f_logit, b_f_row)


def _fox_prep_kernel(qkv_ref, c_ref, qt_ref, ka_ref, vt_ref):
    rows = qkv_ref.shape[0]
    lane = lax.broadcasted_iota(jnp.int32, (rows, V7X_LANES), 1)
    bias = c_ref[...] * (-LOG2E)
    ones_rows = (lax.broadcasted_iota(jnp.int32, (FOX_AUG, rows), 0) < 3).astype(BF16)
    sum_rows = (lax.broadcasted_iota(jnp.int32, (FOX_VROWS - HEAD_DIM, rows), 0) < 1).astype(BF16)
    for h in range(N_FOX_HEADS):
        hs = slice(h * HEAD_DIM, (h + 1) * HEAD_DIM)
        a0 = h * (HEAD_DIM + FOX_AUG)
        qt_ref[a0:a0 + HEAD_DIM, :] = qkv_ref[:, hs].astype(F32).T.astype(BF16)
        qt_ref[a0 + HEAD_DIM:a0 + HEAD_DIM + FOX_AUG, :] = ones_rows
        v0 = h * FOX_VROWS
        vt_ref[v0:v0 + HEAD_DIM, :] = (
            qkv_ref[:, 2 * FOX_W + h * HEAD_DIM:2 * FOX_W + (h + 1) * HEAD_DIM].astype(F32).T.astype(BF16))
        vt_ref[v0 + HEAD_DIM:v0 + FOX_VROWS, :] = sum_rows
        hi, mid, lo = _split3(jnp.broadcast_to(bias[:, h:h + 1], (rows, V7X_LANES)))
        aug = jnp.where(lane == 0, hi.astype(F32), jnp.where(lane == 1, mid.astype(F32),
                                                              jnp.where(lane == 2, lo.astype(F32), 0.0)))
        ka_ref[:, a0:a0 + HEAD_DIM] = qkv_ref[:, FOX_W + h * HEAD_DIM:FOX_W + (h + 1) * HEAD_DIM]
        ka_ref[:, a0 + HEAD_DIM:a0 + HEAD_DIM + FOX_AUG] = aug.astype(BF16)


def _fox_prep(qkv, c_rows):
    s = qkv.shape[0]
    wide = N_FOX_HEADS * (HEAD_DIM + FOX_AUG)
    vrows = N_FOX_HEADS * FOX_VROWS
    return pl.pallas_call(
        _fox_prep_kernel,
        grid=(s // FOX_BLOCK,),
        in_specs=[pl.BlockSpec((FOX_BLOCK, 3 * FOX_W), lambda i: (i, 0)),
                  pl.BlockSpec((FOX_BLOCK, V7X_LANES), lambda i: (i, 0))],
        out_specs=[pl.BlockSpec((wide, FOX_BLOCK), lambda i: (0, i)),
                   pl.BlockSpec((FOX_BLOCK, wide), lambda i: (i, 0)),
                   pl.BlockSpec((vrows, FOX_BLOCK), lambda i: (0, i))],
        out_shape=[jax.ShapeDtypeStruct((wide, s), BF16), jax.ShapeDtypeStruct((s, wide), BF16),
                   jax.ShapeDtypeStruct((vrows, s), BF16)],
        compiler_params=_params(("parallel",)),
        name="fox_prep",
    )(qkv, c_rows)


def _fox_kernel(qi_ref, kj_ref, qt_ref, ka_ref, vt_ref, o_ref, m_ref, acc_ref, s_ref):
    t = pl.program_id(0)
    i = qi_ref[t]
    j = kj_ref[t]
    blk = ka_ref.shape[0]
    aw = HEAD_DIM + FOX_AUG

    @pl.when(j == 0)
    def _():
        m_ref[...] = jnp.full_like(m_ref, NEG)
        acc_ref[...] = jnp.zeros_like(acc_ref)

    def scores(h):
        return _dot(ka_ref[:, h * aw:(h + 1) * aw], qt_ref[h * aw:(h + 1) * aw, :])

    def step(diagonal):
        if diagonal:
            keep = lax.broadcasted_iota(jnp.int32, (blk, blk), 0) <= lax.broadcasted_iota(jnp.int32, (blk, blk), 1)
        s_ref[0] = scores(0)
        for h in range(N_FOX_HEADS):
            if h + 1 < N_FOX_HEADS:
                s_ref[(h + 1) % 2] = scores(h + 1)
            s = s_ref[h % 2]
            if diagonal:
                s = jnp.where(keep, s, NEG)
            m_prev = m_ref[h:h + 1, :]
            m_new = jnp.maximum(m_prev, jnp.max(s, axis=0, keepdims=True))
            p = jnp.exp2(s - m_new).astype(BF16)
            alpha = jnp.exp2(m_prev - m_new)
            m_ref[h:h + 1, :] = m_new
            vs = slice(h * FOX_VROWS, (h + 1) * FOX_VROWS)
            acc_ref[vs, :] = alpha * acc_ref[vs, :] + _dot(vt_ref[vs, :], p)

    @pl.when(j < i)
    def _():
        step(False)

    @pl.when(j == i)
    def _():
        step(True)
        for h in range(N_FOX_HEADS):
            v0 = h * FOX_VROWS
            out_t = acc_ref[v0:v0 + HEAD_DIM, :] / acc_ref[v0 + HEAD_DIM:v0 + HEAD_DIM + 1, :]
            o_ref[:, h * HEAD_DIM:(h + 1) * HEAD_DIM] = out_t.T.astype(o_ref.dtype)


def _fox_attention(qt, ka, vt):
    s = ka.shape[0]
    nb = s // FOX_BLOCK
    qi = jnp.asarray([i for i in range(nb) for _ in range(i + 1)], jnp.int32)
    kj = jnp.asarray([j for i in range(nb) for j in range(i + 1)], jnp.int32)
    wide = ka.shape[1]
    vrows = vt.shape[0]
    grid_spec = pltpu.PrefetchScalarGridSpec(
        num_scalar_prefetch=2,
        grid=(qi.shape[0],),
        in_specs=[
            pl.BlockSpec((wide, FOX_BLOCK), lambda t, qi, kj: (0, qi[t])),
            pl.BlockSpec((FOX_BLOCK, wide), lambda t, qi, kj: (kj[t], 0)),
            pl.BlockSpec((vrows, FOX_BLOCK), lambda t, qi, kj: (0, kj[t])),
        ],
        out_specs=pl.BlockSpec((FOX_BLOCK, FOX_W), lambda t, qi, kj: (qi[t], 0)),
        scratch_shapes=[
            pltpu.VMEM((N_FOX_HEADS, FOX_BLOCK), F32),
            pltpu.VMEM((vrows, FOX_BLOCK), F32),
            pltpu.VMEM((2, FOX_BLOCK, FOX_BLOCK), F32),
        ],
    )
    return pl.pallas_call(
        _fox_kernel,
        grid_spec=grid_spec,
        out_shape=jax.ShapeDtypeStruct((s, FOX_W), BF16),
        compiler_params=_params(("arbitrary",)),
        name="fox_attention",
    )(qi, kj, qt, ka, vt)


def _dil_kernel(q_ref, k_ref, v_ref, kp_ref, vp_ref, y_ref, o_scr, l_scr):
    has_prev_tile = pl.program_id(0) > 0
    b = DIL_BLOCK
    tile = q_ref.shape[0]
    jr = lax.broadcasted_iota(jnp.int32, (b, 2 * b), 0)
    mc = lax.broadcasted_iota(jnp.int32, (b, 2 * b), 1)
    band = (mc >= jr) & (mc <= jr + b)
    band_first = band & ((mc >= b) | has_prev_tile)

    def rows(ref, start, d):
        return ref[pl.ds(start, b), :] if d == 1 else ref[pl.ds(start, b, stride=d), :]

    for g, (_, d) in enumerate(DIL_PAIRS):
        for r in range(d):
            first = tile - b * d + r
            k_blk = [rows(kp_ref, first, d).astype(BF16)]
            v_blk = [rows(vp_ref, first, d).astype(BF16)]
            for n in range(tile // (b * d)):
                cur = r + n * b * d
                k_blk.append(rows(k_ref, cur, d).astype(BF16))
                v_blk.append(rows(v_ref, cur, d).astype(BF16))
                k2 = jnp.concatenate(k_blk[n:n + 2], axis=0)
                v2 = jnp.concatenate(v_blk[n:n + 2], axis=0)
                keep = band if n > 0 else band_first
                s = jnp.where(keep, _dot_nt(rows(q_ref, cur, d).astype(BF16), k2), NEG)
                m = jnp.max(s, axis=1, keepdims=True)
                p = jnp.exp(s - m)
                l = jnp.sum(p, axis=1, keepdims=True)
                o = _dot(p.astype(BF16), v2) / l
                lse = jnp.broadcast_to(m + jnp.log(l), (b, HEAD_DIM))
                if d == 1:
                    o_scr[g, pl.ds(cur, b), :] = o
                    l_scr[g, pl.ds(cur, b), :] = lse
                else:
                    o_scr[g, pl.ds(cur, b, stride=d), :] = o
                    l_scr[g, pl.ds(cur, b, stride=d), :] = lse

    l1, l2, l3 = l_scr[0], l_scr[1], l_scr[2]
    mx = jnp.maximum(jnp.maximum(l1, l2), l3)
    e1, e2, e3 = jnp.exp(l1 - mx), jnp.exp(l2 - mx), jnp.exp(l3 - mx)
    den = e1 + e2 + e3
    y_ref[...] = ((e1 / den) * o_scr[0] + (e2 / den) * o_scr[1] + (e3 / den) * o_scr[2]).astype(y_ref.dtype)


def _dilated_attention(qkv):
    s = qkv.shape[0]
    cur = lambda c: pl.BlockSpec((DIL_TILE, HEAD_DIM), lambda t, h: (t, c * N_DIL_HEADS + h))
    prev = lambda c: pl.BlockSpec((DIL_TILE, HEAD_DIM), lambda t, h: (jnp.maximum(t - 1, 0), c * N_DIL_HEADS + h))
    return pl.pallas_call(
        _dil_kernel,
        grid=(s // DIL_TILE, N_DIL_HEADS),
        in_specs=[cur(0), cur(1), cur(2), prev(1), prev(2)],
        out_specs=pl.BlockSpec((DIL_TILE, HEAD_DIM), lambda t, h: (t, h)),
        out_shape=jax.ShapeDtypeStruct((s, DIL_W), BF16),
        scratch_shapes=[pltpu.VMEM((len(DIL_PAIRS), DIL_TILE, HEAD_DIM), F32),
                        pltpu.VMEM((len(DIL_PAIRS), DIL_TILE, HEAD_DIM), F32)],
        compiler_params=_params(("parallel", "parallel")),
        name="dilated_attention",
    )(qkv, qkv, qkv, qkv, qkv)


def _attn_out_kernel(x_ref, ya_ref, yb_ref, w_ref, g_ref, b_ref, o_ref, ob_ref):
    half = x_ref.shape[0] // 2
    for r in range(2):
        rows = slice(r * half, (r + 1) * half)
        y = _dot(jnp.concatenate([ya_ref[rows, :], yb_ref[rows, :]], axis=1), w_ref[...])
        out = _layer_norm(ALPHA * x_ref[rows, :] + y, g_ref[...], b_ref[...])
        o_ref[rows, :] = out
        ob_ref[rows, :] = out.astype(BF16)


def _attn_out(x, ya, yb, w, g, b):
    s = x.shape[0]
    row = lambda width: pl.BlockSpec((OUT_ROWS, width), lambda i: (i, 0))
    const = lambda shape: pl.BlockSpec(shape, lambda i: (0, 0))
    return pl.pallas_call(
        _attn_out_kernel,
        grid=(s // OUT_ROWS,),
        in_specs=[row(D_MODEL), row(FOX_W), row(DIL_W),
                  const((FOX_W + DIL_W, D_MODEL)), const((1, D_MODEL)), const((1, D_MODEL))],
        out_specs=[row(D_MODEL), row(D_MODEL)],
        out_shape=[jax.ShapeDtypeStruct((s, D_MODEL), F32), jax.ShapeDtypeStruct((s, D_MODEL), BF16)],
        compiler_params=_params(("parallel",)),
        name="attn_out",
    )(x, ya, yb, w, g, b)


def _cmul(ar, ai, br, bi):
    return ar * br - ai * bi, ar * bi + ai * br


def _s5_prep_kernel(lrg_ref, lig_ref, ldtg_ref, lr_ref, li_ref, ldt_ref, br_ref, bi_ref,
                    pr_ref, pi_ref, qr_ref, qi_ref, bbr_ref, bbi_ref):
    def a_bar(lr, li, ldt):
        dt = jnp.exp(ldt)
        mag = jnp.exp(lr * dt)
        return mag * jnp.cos(li * dt), mag * jnp.sin(li * dt)

    lr, li = lr_ref[...], li_ref[...]
    ar, ai = a_bar(lr, li, ldt_ref[...])
    den = lr * lr + li * li
    cr = ((ar - 1.0) * lr + ai * li) / den
    ci = (ai * lr - (ar - 1.0) * li) / den
    bbr_ref[...], bbi_ref[...] = _cmul(cr, ci, br_ref[...], bi_ref[...])

    ar, ai = a_bar(lrg_ref[...], lig_ref[...], ldtg_ref[...])
    pr, pi = ar, ai
    pr_ref[0], pi_ref[0] = pr, pi
    for k in range(1, S5_SEG):
        pr, pi = _cmul(pr, pi, ar, ai)
        pr_ref[k], pi_ref[k] = pr, pi
    sr, si = pr, pi
    qr_ref[0], qi_ref[0] = sr, si
    for k in range(1, V7X_SUBLANES):
        pr, pi = _cmul(pr, pi, sr, si)
        qr_ref[k], qi_ref[k] = pr, pi


def _s5_prep(lam_re, lam_im, log_dt, b_re, b_im):
    ldt = jnp.broadcast_to(log_dt[:, None], lam_re.shape)
    rep = lambda v: jnp.repeat(v, S5_GROUP, axis=0)
    rows = S5_GROUPS * S5_GROUP
    bt = lambda v: v.transpose(0, 2, 1).reshape(rows, S5_STATE)
    mat = jax.ShapeDtypeStruct((rows, S5_STATE), F32)
    pw = jax.ShapeDtypeStruct((S5_SEG, S5_GROUPS, S5_STATE), F32)
    qw = jax.ShapeDtypeStruct((V7X_SUBLANES, S5_GROUPS, S5_STATE), F32)
    return pl.pallas_call(_s5_prep_kernel, out_shape=[pw, pw, qw, qw, mat, mat], name="s5_prep")(
        lam_re, lam_im, ldt, rep(lam_re), rep(lam_im), rep(ldt), bt(b_re), bt(b_im))


def _s5_core_kernel(xb_ref, win_ref, bblk_ref, cblk_ref, pr_ref, pi_ref, qr_ref, qi_ref, d_ref, z_ref,
                    up_ref, sr_ref, si_ref, cr_ref, ci_ref):
    @pl.when(pl.program_id(0) == 0)
    def _():
        cr_ref[...] = jnp.zeros_like(cr_ref)
        ci_ref[...] = jnp.zeros_like(ci_ref)

    sl = V7X_SUBLANES
    rows = xb_ref.shape[0]
    u = _dot(xb_ref[...], win_ref[...])
    ri = lax.broadcasted_iota(jnp.int32, (rows, rows), 0)
    ci = lax.broadcasted_iota(jnp.int32, (rows, rows), 1)
    perm = ((ri // sl == ci % S5_SEG) & (ri % sl == ci // S5_SEG)).astype(BF16)
    unperm = ((ci // sl == ri % S5_SEG) & (ci % sl == ri // S5_SEG)).astype(BF16)
    hi, mid, lo = _split3(u)
    up_ref[...] = _dot(perm, hi) + _dot(perm, mid) + _dot(perm, lo)

    sub = lax.broadcasted_iota(jnp.int32, (sl, S5_LANES), 0)
    width = S5_GB * S5_GROUP
    ys = []
    for jb in range(S5_NB):
        bu = _dot(up_ref[:, jb * width:(jb + 1) * width].astype(BF16), bblk_ref[jb])
        a_re, a_im = pr_ref[jb, 0:1, :], pi_ref[jb, 0:1, :]
        xr, xi = bu[0:sl, :S5_LANES], bu[0:sl, S5_LANES:]
        sr_ref[0:sl, :], si_ref[0:sl, :] = xr, xi
        for j in range(1, S5_SEG):
            tr, ti = _cmul(a_re, a_im, xr, xi)
            xr, xi = tr + bu[j * sl:(j + 1) * sl, :S5_LANES], ti + bu[j * sl:(j + 1) * sl, S5_LANES:]
            sr_ref[j * sl:(j + 1) * sl, :], si_ref[j * sl:(j + 1) * sl, :] = xr, xi

        q_re, q_im = qr_ref[jb], qi_ref[jb]
        er, ei = xr, xi
        for k in (1, 2, 4):
            tr = jnp.where(sub >= k, pltpu.roll(er, k, 0), 0.0)
            ti = jnp.where(sub >= k, pltpu.roll(ei, k, 0), 0.0)
            mr, mi = _cmul(q_re[k - 1:k, :], q_im[k - 1:k, :], tr, ti)
            er, ei = er + mr, ei + mi
        in_re, in_im = cr_ref[jb:jb + 1, :], ci_ref[jb:jb + 1, :]
        mr, mi = _cmul(q_re, q_im, in_re, in_im)
        er, ei = er + mr, ei + mi
        cr_ref[jb:jb + 1, :], ci_ref[jb:jb + 1, :] = er[sl - 1:, :], ei[sl - 1:, :]
        cin_re = jnp.where(sub >= 1, pltpu.roll(er, 1, 0), in_re)
        cin_im = jnp.where(sub >= 1, pltpu.roll(ei, 1, 0), in_im)
        fr, fi = _cmul(pr_ref[jb][:, None, :], pi_ref[jb][:, None, :], cin_re[None], cin_im[None])
        shape3 = (S5_SEG, sl, S5_LANES)
        state = jnp.concatenate([(sr_ref[...].reshape(shape3) + fr).reshape(-1, S5_LANES),
                                 (si_ref[...].reshape(shape3) + fi).reshape(-1, S5_LANES)], axis=1)
        ys.append(_dot(state.astype(BF16), cblk_ref[jb]))
    y = jnp.concatenate(ys, axis=1) + d_ref[...] * up_ref[...]
    z = 0.5 * y * (1.0 + jnp.tanh(math.sqrt(2.0 / math.pi) * (y + 0.044715 * (y * y * y))))
    z_ref[...] = _dot(unperm, z.astype(BF16)).astype(z_ref.dtype)


def _s5_core(xb, w_in, bblk, cblk, p_re, p_im, q_re, q_im, d_row):
    s = xb.shape[0]
    const = lambda shape: pl.BlockSpec(shape, lambda t: (0,) * len(shape))
    return pl.pallas_call(
        _s5_core_kernel,
        grid=(s // S5_ROWS,),
        in_specs=[pl.BlockSpec((S5_ROWS, D_MODEL), lambda t: (t, 0)), const(w_in.shape),
                  const(bblk.shape), const(cblk.shape), const(p_re.shape), const(p_im.shape),
                  const(q_re.shape), const(q_im.shape), const((1, S5_WIDTH))],
        out_specs=pl.BlockSpec((S5_ROWS, S5_WIDTH), lambda t: (t, 0)),
        out_shape=jax.ShapeDtypeStruct((s, S5_WIDTH), BF16),
        scratch_shapes=[pltpu.VMEM((S5_ROWS, S5_WIDTH), F32),
                        pltpu.VMEM((S5_ROWS, S5_LANES), F32), pltpu.VMEM((S5_ROWS, S5_LANES), F32),
                        pltpu.VMEM((V7X_SUBLANES, S5_LANES), F32), pltpu.VMEM((V7X_SUBLANES, S5_LANES), F32)],
        compiler_params=_params(("arbitrary",)),
        name="s5_core",
    )(xb, w_in, bblk, cblk, p_re, p_im, q_re, q_im, d_row)


def _block_diag(blocks):
    nb, gb, r, c = blocks.shape
    eye = jnp.eye(gb, dtype=blocks.dtype)
    return jnp.einsum("ngrc,gk->ngrkc", blocks, eye).reshape(nb, gb * r, gb * c)


def _glu_kernel(x_ref, z_ref, wo_ref, wg_ref, g_ref, b_ref, o_ref, ob_ref):
    half = z_ref.shape[0] // 2
    for r in range(2):
        rows = slice(r * half, (r + 1) * half)
        z = z_ref[rows, :]
        y = _dot(z, wo_ref[...]) * jax.nn.sigmoid(_dot(z, wg_ref[...]))
        out = _layer_norm(ALPHA * x_ref[rows, :] + y, g_ref[...], b_ref[...])
        o_ref[rows, :] = out
        ob_ref[rows, :] = out.astype(BF16)


def _glu(x, z, wo, wg, g, b):
    s = x.shape[0]
    row = lambda width: pl.BlockSpec((GLU_ROWS, width), lambda i: (i, 0))
    const = lambda shape: pl.BlockSpec(shape, lambda i: (0, 0))
    return pl.pallas_call(
        _glu_kernel,
        grid=(s // GLU_ROWS,),
        in_specs=[row(D_MODEL), row(S5_WIDTH), const((S5_WIDTH, D_MODEL)), const((S5_WIDTH, D_MODEL)),
                  const((1, D_MODEL)), const((1, D_MODEL))],
        out_specs=[row(D_MODEL), row(D_MODEL)],
        out_shape=[jax.ShapeDtypeStruct((s, D_MODEL), F32), jax.ShapeDtypeStruct((s, D_MODEL), BF16)],
        compiler_params=_params(("parallel",)),
        name="glu",
    )(x, z, wo, wg, g, b)


def _attn_weights_kernel(w_ref, fox_ref, f_ref, dil_ref):
    scale = HEAD_DIM ** -0.5
    w = w_ref[...]
    d0 = 3 * FOX_W + N_FOX_HEADS
    fox_ref[:, 0:FOX_W] = (w[:, 0:FOX_W] * (scale * LOG2E)).astype(BF16)
    fox_ref[:, FOX_W:3 * FOX_W] = w[:, FOX_W:3 * FOX_W].astype(BF16)
    f_ref[...] = jnp.zeros_like(f_ref)
    f_ref[:, 0:N_FOX_HEADS] = w[:, 3 * FOX_W:d0].astype(BF16)
    dil_ref[:, 0:DIL_W] = (w[:, d0:d0 + DIL_W] * scale).astype(BF16)
    dil_ref[:, DIL_W:3 * DIL_W] = w[:, d0 + DIL_W:d0 + 3 * DIL_W].astype(BF16)


def _attn_weights(w_in):
    rows = D_MODEL // CAST_STEPS
    out = lambda width: pl.BlockSpec((rows, width), lambda r: (r, 0))
    return pl.pallas_call(
        _attn_weights_kernel,
        grid=(CAST_STEPS,),
        in_specs=[pl.BlockSpec((rows, w_in.shape[1]), lambda r: (r, 0))],
        out_specs=[out(3 * FOX_W), out(V7X_LANES), out(3 * DIL_W)],
        out_shape=[jax.ShapeDtypeStruct((D_MODEL, 3 * FOX_W), BF16), jax.ShapeDtypeStruct((D_MODEL, V7X_LANES), BF16),
                   jax.ShapeDtypeStruct((D_MODEL, 3 * DIL_W), BF16)],
        compiler_params=_params(("parallel",)),
        name="attn_weights",
    )(w_in)


def _attention_mixer(x, xb, w_in, b_f, w_out, g, b):
    w_fox, w_f, w_dil = _attn_weights(w_in)
    qkv_fox, f_logit = _attn_in(xb, w_fox, w_f)
    c_rows = _forget_cumsum(f_logit, jnp.pad(b_f, (0, V7X_LANES - N_FOX_HEADS))[None, :])
    ya = _fox_attention(*_fox_prep(qkv_fox, c_rows))
    yb = _dilated_attention(_matmul(xb, w_dil, F32))
    return _attn_out(x, ya, yb, w_out.astype(BF16), g, b)


def _s5_mixer(x, xb, w_in, lam_re, lam_im, log_dt, b_re, b_im, c_re, c_im, d_skip, w_glu_out, w_glu_gate, g, b):
    p_re, p_im, q_re, q_im, bb_re, bb_im = _s5_prep(lam_re, lam_im, log_dt, b_re, b_im)
    lanes = lambda p: p.reshape(p.shape[0], S5_NB, S5_LANES).transpose(1, 0, 2)
    blk = lambda v: v.reshape(S5_NB, S5_GB, S5_GROUP, S5_STATE)
    bblk = jnp.concatenate([_block_diag(blk(bb_re)), _block_diag(blk(bb_im))], axis=2).astype(BF16)
    ct = lambda v: v.transpose(0, 2, 1).reshape(S5_NB, S5_GB, S5_STATE, S5_GROUP)
    cblk = jnp.concatenate([_block_diag(ct(c_re)), -_block_diag(ct(c_im))], axis=1).astype(BF16)
    z = _s5_core(xb, w_in.astype(BF16), bblk, cblk, lanes(p_re), lanes(p_im), lanes(q_re), lanes(q_im), d_skip[None, :])
    return _glu(x, z, w_glu_out.astype(BF16), w_glu_gate.astype(BF16), g, b)


def kernel(x, ffn1_w_gate, ffn1_w_up, ffn1_w_down, ffn2_w_gate, ffn2_w_up, ffn2_w_down, ln_gain, ln_bias,
           attn_w_in, attn_b_f, attn_w_out, s5_w_in, s5_lambda_re, s5_lambda_im, s5_log_dt, s5_b_re, s5_b_im,
           s5_c_re, s5_c_im, s5_d, s5_w_glu_out, s5_w_glu_gate):
    batch, seq, _ = x.shape
    assert batch == 1
    h = x.reshape(seq, D_MODEL)
    hb = None
    ln = lambda i, k: (ln_gain[i, k][None, :], ln_bias[i, k][None, :])
    for i in range(DEPTH):
        h, hb = _ffn(h, hb, *_ffn_weights_bf16(ffn1_w_gate, ffn1_w_up, ffn1_w_down, i), *ln(i, 0))
        j = i // 2
        if i % 2 == 0:
            h, hb = _attention_mixer(h, hb, attn_w_in[j], attn_b_f[j], attn_w_out[j], *ln(i, 1))
        else:
            h, hb = _s5_mixer(h, hb, s5_w_in[j], s5_lambda_re[j], s5_lambda_im[j], s5_log_dt[j], s5_b_re[j],
                              s5_b_im[j], s5_c_re[j], s5_c_im[j], s5_d[j], s5_w_glu_out[j], s5_w_glu_gate[j],
                              *ln(i, 1))
        h, hb = _ffn(h, hb, *_ffn_weights_bf16(ffn2_w_gate, ffn2_w_up, ffn2_w_down, i), *ln(i, 2))
    return h.reshape(batch, seq, D_MODEL)
```
